```python
import math
import jax, jax.numpy as jnp
from jax import lax
import numpy as np

D_MODEL = 2048
BATCH = 2
SEQ = 8192
DEPTH = 4

HEAD_DIM = 64
N_MIXERS = 4
GROUP_HEADS = D_MODEL // (N_MIXERS * HEAD_DIM)
GROUP_WIDTH = GROUP_HEADS * HEAD_DIM
MIX_WIDTH = N_MIXERS * GROUP_WIDTH
IDX_HEADS = 16
IDX_DIM = 64
TOPK_MAX = 256
WINDOW = 128
BLOCK_Q = 128
REL_BUCKETS = 32
REL_MAX_DIST = 128
D_FF = 11 * D_MODEL // 4
EPS = 1e-6

SPLIT_SIZES = (
    GROUP_WIDTH, HEAD_DIM, HEAD_DIM, IDX_HEADS * IDX_DIM, IDX_DIM, IDX_HEADS,
    GROUP_WIDTH, HEAD_DIM, HEAD_DIM,
    GROUP_WIDTH, GROUP_WIDTH, GROUP_WIDTH, GROUP_HEADS, GROUP_WIDTH,
    GROUP_WIDTH, GROUP_WIDTH, GROUP_WIDTH,
)
IN_COLS = sum(SPLIT_SIZES)

kernel_name = 'hybrid_dsa_swa_fox_stickbreak_macaron'


def _rms(x, g):
    xf = x.astype(jnp.float32)
    y = xf * lax.rsqrt(jnp.mean(jnp.square(xf), axis=-1, keepdims=True) + EPS)
    return (y * g.astype(jnp.float32)).astype(x.dtype)


def _modulate(x, g, shift, scale):
    return _rms(x, g) * (1 + scale[:, None, :]) + shift[:, None, :]


def _swiglu(h, w_gate, w_up, w_down):
    return (jax.nn.silu(h @ w_gate) * (h @ w_up)) @ w_down


def _to_blocks(a):
    b, s = a.shape[:2]
    a = a.reshape((b, s // BLOCK_Q, BLOCK_Q) + a.shape[2:])
    return jnp.moveaxis(a, 1, 0)


def _from_blocks(a):
    a = jnp.moveaxis(a, 0, 1)
    return a.reshape((a.shape[0], a.shape[1] * a.shape[2]) + a.shape[3:])


def _rel_bucket(dist):
    n = jnp.maximum(dist, 0)
    max_exact = REL_BUCKETS // 2
    nf = jnp.maximum(n, 1).astype(jnp.float32)
    large = max_exact + (jnp.log(nf / max_exact) / math.log(REL_MAX_DIST / max_exact)
                         * (REL_BUCKETS - max_exact)).astype(jnp.int32)
    large = jnp.minimum(large, REL_BUCKETS - 1)
    return jnp.where(n < max_exact, n, large)


def _dsa_attention(q, k, v, iq, ik, iw, rel_tab):
    seq = k.shape[1]
    k_top = min(TOPK_MAX, seq // 4)
    kpos = jnp.arange(seq)
    scale = HEAD_DIM ** -0.5
    gather = jax.vmap(lambda t, i: t[i])

    def block(args):
        blk, q_b, iq_b, iw_b = args
        qpos = blk * BLOCK_Q + jnp.arange(BLOCK_Q)
        rel = jax.nn.relu(jnp.einsum('bqhd,bsd->bqhs', iq_b, ik).astype(jnp.float32) * IDX_DIM ** -0.5)
        score = jnp.einsum('bqhs,bqh->bqs', rel, iw_b.astype(jnp.float32))
        score = jnp.where(kpos[None, None, :] <= qpos[None, :, None], score, -jnp.inf)
        _, sel = lax.top_k(score, k_top)
        k_sel = gather(k, sel)
        v_sel = gather(v, sel)
        logits = jnp.einsum('bqhd,bqkd->bhqk', q_b, k_sel).astype(jnp.float32) * scale
        dist = qpos[None, :, None] - sel
        bias = jnp.moveaxis(rel_tab[_rel_bucket(dist)], -1, 1).astype(jnp.float32)
        logits = jnp.where((dist >= 0)[:, None], logits + bias, -jnp.inf)
        p = jax.nn.softmax(logits, axis=-1).astype(v.dtype)
        return jnp.einsum('bhqk,bqkd->bqhd', p, v_sel)

    nblk = seq // BLOCK_Q
    out = lax.map(block, (jnp.arange(nblk), _to_blocks(q), _to_blocks(iq), _to_blocks(iw)))
    return _from_blocks(out)


def _swa_sink_attention(q, k, v, sinks, rel_tab):
    b, seq, h, dh = q.shape
    n = seq // BLOCK_Q
    qb = q.reshape(b, n, BLOCK_Q, h, dh)

    def band(t):
        tb = t.reshape(b, n, BLOCK_Q, dh)
        prev = jnp.pad(tb, ((0, 0), (1, 0), (0, 0), (0, 0)))[:, :-1]
        return jnp.concatenate([prev, tb], axis=2)

    kk, vv = band(k), band(v)
    logits = jnp.einsum('bnqhd,bnkd->bnhqk', qb, kk).astype(jnp.float32) * HEAD_DIM ** -0.5
    dist = jnp.arange(BLOCK_Q)[:, None] + BLOCK_Q - jnp.arange(2 * BLOCK_Q)[None, :]
    kpos = jnp.arange(n)[:, None] * BLOCK_Q - BLOCK_Q + jnp.arange(2 * BLOCK_Q)[None, :]
    valid = ((dist >= 0) & (dist < WINDOW))[None] & (kpos >= 0)[:, None, :]
    bias = jnp.moveaxis(rel_tab[_rel_bucket(dist)], -1, 0).astype(jnp.float32)
    logits = jnp.where(valid[None, :, None], logits + bias, -jnp.inf)
    sink = jnp.broadcast_to(sinks.astype(jnp.float32)[None, None, :, None, None], logits.shape[:-1] + (1,))
    p = jax.nn.softmax(jnp.concatenate([logits, sink], axis=-1), axis=-1)[..., :-1]
    o = jnp.einsum('bnhqk,bnkd->bnqhd', p.astype(v.dtype), vv)
    return o.reshape(b, seq, h, dh)


def _forgetting_attention(q, k, v, log_f):
    seq = k.shape[1]
    cum = jnp.cumsum(log_f, axis=1)
    cum_k = jnp.moveaxis(cum, 1, 2)
    kpos = jnp.arange(seq)

    def block(args):
        blk, q_b, cum_q = args
        qpos = blk * BLOCK_Q + jnp.arange(BLOCK_Q)
        logits = jnp.einsum('bqhd,bshd->bhqs', q_b, k).astype(jnp.float32) * HEAD_DIM ** -0.5
        logits = logits + jnp.moveaxis(cum_q, 1, 2)[..., None] - cum_k[:, :, None, :]
        logits = jnp.where(kpos[None, :] <= qpos[:, None], logits, -jnp.inf)
        p = jax.nn.softmax(logits, axis=-1).astype(v.dtype)
        return jnp.einsum('bhqs,bshd->bqhd', p, v)

    nblk = seq // BLOCK_Q
    out = lax.map(block, (jnp.arange(nblk), _to_blocks(q), _to_blocks(cum)))
    return _from_blocks(out)


def _stick_breaking_attention(q, k, v):
    seq = k.shape[1]
    kpos = jnp.arange(seq)

    def block(args):
        blk, q_b = args
        qpos = blk * BLOCK_Q + jnp.arange(BLOCK_Q)
        z = jnp.einsum('bqhd,bshd->bhqs', q_b, k).astype(jnp.float32) * HEAD_DIM ** -0.5
        before = kpos[None, :] < qpos[:, None]
        u = jnp.where(before, jax.nn.log_sigmoid(-z), 0.0)
        between = lax.cumsum(u, axis=3, reverse=True) - u
        w = jnp.where(before, jnp.exp(jax.nn.log_sigmoid(z) + between), 0.0)
        return jnp.einsum('bhqs,bshd->bqhd', w.astype(v.dtype), v)

    nblk = seq // BLOCK_Q
    out = lax.map(block, (jnp.arange(nblk), _to_blocks(q)))
    return _from_blocks(out)


def _mixer(h, w_in, qk_g, forget_b, sinks, rel_table, group_g, w_out):
    b, seq, _ = h.shape
    points = [int(p) for p in np.cumsum(SPLIT_SIZES)[:-1]]
    (a_q, a_k, a_v, a_iq, a_ik, a_iw,
     b_q, b_k, b_v,
     c_q, c_k, c_v, c_f, c_g,
     d_q, d_k, d_v) = jnp.split(h @ w_in, points, axis=-1)
    heads = lambda t: t.reshape(b, seq, GROUP_HEADS, HEAD_DIM)
    o_a = _dsa_attention(_rms(heads(a_q), qk_g[0]), _rms(a_k, qk_g[1]), a_v,
                         a_iq.reshape(b, seq, IDX_HEADS, IDX_DIM), a_ik, a_iw * IDX_HEADS ** -0.5,
                         rel_table[:, :GROUP_HEADS])
    o_b = _swa_sink_attention(_rms(heads(b_q), qk_g[2]), _rms(b_k, qk_g[3]), b_v, sinks,
                              rel_table[:, GROUP_HEADS:])
    log_f = jax.nn.log_sigmoid(c_f.astype(jnp.float32) + forget_b.astype(jnp.float32))
    o_c = _forgetting_attention(_rms(heads(c_q), qk_g[4]), _rms(heads(c_k), qk_g[5]), heads(c_v), log_f)
    o_c = o_c * jax.nn.sigmoid(heads(c_g))
    o_d = _stick_breaking_attention(heads(d_q), heads(d_k), heads(d_v))
    y = jnp.stack([o.reshape(b, seq, GROUP_WIDTH) for o in (o_a, o_b, o_c, o_d)], axis=2)
    y = _rms(y, group_g.reshape(N_MIXERS, GROUP_WIDTH)).reshape(b, seq, MIX_WIDTH)
    return y @ w_out


def setup_inputs(seed: int = 0) -> dict:
    key = jax.random.key(seed)
    ks = jax.random.split(key, 16)
    nrm = lambda k, shape, s: jax.random.normal(k, shape, jnp.float32) * s
    return {
        'x': nrm(ks[0], (BATCH, SEQ, D_MODEL), 1.0),
        'c': nrm(ks[1], (BATCH, D_MODEL), 1.0),
        'w_ada': nrm(ks[2], (DEPTH, D_MODEL, 9 * D_MODEL), 0.5 * D_MODEL ** -0.5),
        'b_ada': nrm(ks[3], (DEPTH, 9 * D_MODEL), 0.02),
        'norm_g': 1.0 + nrm(ks[4], (DEPTH, 3, D_MODEL), 0.02),
        'w_in': nrm(ks[5], (DEPTH, D_MODEL, IN_COLS), D_MODEL ** -0.5),
        'qk_g': 1.0 + nrm(ks[6], (DEPTH, 6, HEAD_DIM), 0.02),
        'forget_b': 4.0 + nrm(ks[7], (DEPTH, GROUP_HEADS), 0.5),
        'sinks': nrm(ks[8], (DEPTH, GROUP_HEADS), 0.5),
        'rel_table': nrm(ks[9], (REL_BUCKETS, 2 * GROUP_HEADS), 0.5),
        'group_g': 1.0 + nrm(ks[10], (DEPTH, MIX_WIDTH), 0.02),
        'w_out': nrm(ks[11], (DEPTH, MIX_WIDTH, D_MODEL), MIX_WIDTH ** -0.5),
        'w_ffn_gate': nrm(ks[12], (DEPTH, 2, D_MODEL, D_FF), D_MODEL ** -0.5),
        'w_ffn_up': nrm(ks[13], (DEPTH, 2, D_MODEL, D_FF), D_MODEL ** -0.5),
        'w_ffn_down': nrm(ks[14], (DEPTH, 2, D_FF, D_MODEL), D_FF ** -0.5),
    }


def reference(x, c, w_ada, b_ada, norm_g, w_in, qk_g, forget_b, sinks, rel_table, group_g, w_out,
              w_ffn_gate, w_ffn_up, w_ffn_down):
    cond = jax.nn.silu(c)
    for l in range(DEPTH):
        mod = cond @ w_ada[l] + b_ada[l]
        sh1, sc1, g1, sh2, sc2, g2, sh3, sc3, g3 = jnp.split(mod, 9, axis=-1)
        h = _modulate(x, norm_g[l, 0], sh1, sc1)
        x = x + 0.5 * g1[:, None] * _swiglu(h, w_ffn_gate[l, 0], w_ffn_up[l, 0], w_ffn_down[l, 0])
        h = _modulate(x, norm_g[l, 1], sh2, sc2)
        x = x + g2[:, None] * _mixer(h, w_in[l], qk_g[l], forget_b[l], sinks[l], rel_table,
                                     group_g[l], w_out[l])
        h = _modulate(x, norm_g[l, 2], sh3, sc3)
        x = x + 0.5 * g3[:, None] * _swiglu(h, w_ffn_gate[l, 1], w_ffn_up[l, 1], w_ffn_down[l, 1])
    return x
```

```python
import functools
import math

import numpy as np
import jax
import jax.numpy as jnp
from jax import lax
from jax.experimental import pallas as pl
from jax.experimental.pallas import tpu as pltpu

F32 = jnp.float32
BF16 = jnp.bfloat16
I32 = jnp.int32

HEAD_DIM = 64
N_MIXERS = 4
GROUP_HEADS = 8
GROUP_WIDTH = GROUP_HEADS * HEAD_DIM
IDX_HEADS = 16
IDX_DIM = 64
TOPK_MAX = 256
WINDOW = 128
REL_BUCKETS = 32
REL_MAX_DIST = 128
EPS = 1e-6

LANES = 128
PAIRS = GROUP_HEADS // 2
NEG = -1e30
INT_MIN = -2 ** 31
STICK_EXIT = -110.0
VMEM_LIMIT = 56 * 1024 * 1024

PT = 512
T_AQ, T_BQ, T_CQ, T_CK, T_K2 = 0, 1, 2, 3, 4
N_NORM_TILES = 5
T_CV, T_CG, T_DQ, T_DK, T_DV, T_IQ, T_MISC = 5, 6, 7, 8, 9, 10, 12
N_TILES = 13
P_COLS = N_TILES * PT


def _cparams(sem, vmem=VMEM_LIMIT):
    return pltpu.CompilerParams(dimension_semantics=sem, vmem_limit_bytes=vmem)


def _nt_dot(a, b):
    return lax.dot_general(a, b, (((1,), (1,)), ((), ())), preferred_element_type=F32)


def _dot(a, b):
    return jnp.dot(a, b, preferred_element_type=F32)


def _sigmoid(x):
    return 1.0 / (1.0 + jnp.exp(-x))


def _lane_tile(x, n):
    return x if n == 1 else jnp.concatenate([x] * n, axis=1)


def _own_lane_mask(shape, h):
    lane = lax.broadcasted_iota(I32, shape, 1)
    return (lane < HEAD_DIM) if h % 2 == 0 else (lane >= HEAD_DIM)


def _mod_kernel(c_ref, w_ref, b_ref, o_ref):
    c = c_ref[...]
    cond = (c * _sigmoid(c)).astype(BF16)
    o_ref[0] = _dot(cond, w_ref[0].astype(BF16)) + b_ref[0]


def _modulation(c, w_ada, b_ada):
    depth, d, n = w_ada.shape
    b = c.shape[0]
    rows = 8
    tn = 1024
    c_pad = jnp.zeros((rows, d), F32).at[:b].set(c)
    out = pl.pallas_call(
        _mod_kernel,
        grid=(depth, n // tn),
        in_specs=[
            pl.BlockSpec((rows, d), lambda l, j: (0, 0)),
            pl.BlockSpec((1, d, tn), lambda l, j: (l, 0, j)),
            pl.BlockSpec((1, 1, tn), lambda l, j: (l, 0, j)),
        ],
        out_specs=pl.BlockSpec((1, rows, tn), lambda l, j: (l, 0, j)),
        out_shape=jax.ShapeDtypeStruct((depth, rows, n), F32),
        compiler_params=_cparams(("arbitrary", "arbitrary")),
        name="adaln_mod",
    )(c_pad, w_ada, b_ada.reshape(depth, 1, n))
    return out[:, :b]


def _norm_modulate(x, ng, sh, sc):
    ms = jnp.mean(x * x, axis=-1, keepdims=True)
    y = x * lax.rsqrt(ms + EPS) * ng
    return y * (1.0 + sc) + sh


def _ffn_kernel(x_ref, ng_ref, sh_ref, sc_ref, gt_ref, wg_ref, wu_ref, wd_ref, o_ref, h_ref, acc_ref):
    f = pl.program_id(1)

    @pl.when(f == 0)
    def _():
        h = _norm_modulate(x_ref[...], ng_ref[...], sh_ref[0], sc_ref[0])
        h_ref[...] = h.astype(BF16)
        acc_ref[...] = jnp.zeros_like(acc_ref)

    h = h_ref[...]
    g = _dot(h, wg_ref[...])
    u = _dot(h, wu_ref[...])
    a = (g * _sigmoid(g) * u).astype(BF16)
    acc_ref[...] += _dot(a, wd_ref[...])

    @pl.when(f == pl.num_programs(1) - 1)
    def _():
        o_ref[...] = x_ref[...] + 0.5 * gt_ref[0] * acc_ref[...]


def _ffn(x, ng, sh, sc, gt, wg, wu, wd, l, s, seq):
    t, d = x.shape
    dff = wg.shape[-1]
    tm, tf = 512, 512
    per_b = seq // tm
    vec = pl.BlockSpec((1, 1, d), lambda i, f: (i // per_b, 0, 0))
    return pl.pallas_call(
        _ffn_kernel,
        grid=(t // tm, dff // tf),
        in_specs=[
            pl.BlockSpec((tm, d), lambda i, f: (i, 0)),
            pl.BlockSpec((1, d), lambda i, f: (0, 0)),
            vec, vec, vec,
            pl.BlockSpec((None, None, d, tf), lambda i, f: (l, s, 0, f)),
            pl.BlockSpec((None, None, d, tf), lambda i, f: (l, s, 0, f)),
            pl.BlockSpec((None, None, tf, d), lambda i, f: (l, s, f, 0)),
        ],
        out_specs=pl.BlockSpec((tm, d), lambda i, f: (i, 0)),
        out_shape=jax.ShapeDtypeStruct((t, d), F32),
        scratch_shapes=[pltpu.VMEM((tm, d), BF16), pltpu.VMEM((tm, d), F32)],
        compiler_params=_cparams(("arbitrary", "arbitrary")),
        name="ffn",
    )(x, ng, sh, sc, gt, wg, wu, wd)


def _inproj_kernel(x_ref, ng_ref, sh_ref, sc_ref, w_ref, gain_ref, bd_ref, o_ref, h_ref, *, n_norm):
    j = pl.program_id(1)

    @pl.when(j == 0)
    def _():
        h = _norm_modulate(x_ref[...], ng_ref[...], sh_ref[0], sc_ref[0])
        h_ref[...] = h.astype(BF16)

    y = _dot(h_ref[...], w_ref[...])

    @pl.when(j < n_norm)
    def _():
        sq = y * y
        hi = sq.astype(BF16)
        lo = (sq - hi.astype(F32)).astype(BF16)
        ms = _dot(hi, bd_ref[...]) + _dot(lo, bd_ref[...])
        o_ref[...] = (y * lax.rsqrt(ms + EPS) * gain_ref[...]).astype(o_ref.dtype)

    @pl.when(j >= n_norm)
    def _():
        o_ref[...] = y.astype(o_ref.dtype)


def _inproj(x, ng, sh, sc, w, gain, bd, l, seq, n_norm, out_dtype):
    t, d = x.shape
    n = w.shape[-1]
    tm = 512
    tn = bd.shape[0]
    per_b = seq // tm
    vec = pl.BlockSpec((1, 1, d), lambda i, j: (i // per_b, 0, 0))
    return pl.pallas_call(
        functools.partial(_inproj_kernel, n_norm=n_norm),
        grid=(t // tm, n // tn),
        in_specs=[
            pl.BlockSpec((tm, d), lambda i, j: (i, 0)),
            pl.BlockSpec((1, d), lambda i, j: (0, 0)),
            vec, vec,
            pl.BlockSpec((None, d, tn), lambda i, j: (l, 0, j)),
            pl.BlockSpec((None, 1, tn), lambda i, j: (l, 0, j)),
            pl.BlockSpec((tn, tn), lambda i, j: (0, 0)),
        ],
        out_specs=pl.BlockSpec((tm, tn), lambda i, j: (i, j)),
        out_shape=jax.ShapeDtypeStruct((t, n), out_dtype),
        scratch_shapes=[pltpu.VMEM((tm, d), BF16)],
        compiler_params=_cparams(("arbitrary", "arbitrary")),
        name="inproj",
    )(x, ng, sh, sc, w, gain, bd)


def _rel_bucket_np(dist):
    n = np.maximum(dist, 0)
    max_exact = REL_BUCKETS // 2
    nf = np.maximum(n, 1).astype(np.float32)
    large = max_exact + (np.log(nf / np.float32(max_exact)) / np.float32(math.log(REL_MAX_DIST / max_exact))
                         * np.float32(REL_BUCKETS - max_exact)).astype(np.int32)
    large = np.minimum(large, REL_BUCKETS - 1)
    return np.where(n < max_exact, n, large).astype(np.int32)


def _bias_kernel(tab_ref, bkt_ref, o_ref, *, sub_far):
    h = pl.program_id(0)
    far = tab_ref[REL_BUCKETS - 1, h]
    for which in range(2):
        bkt = bkt_ref[which]
        acc = jnp.zeros(bkt.shape, F32)
        for b in range(REL_BUCKETS):
            acc = jnp.where(bkt == b, tab_ref[b, h], acc)
        o_ref[0, which] = acc - jnp.where(h < sub_far, far, 0.0)


def _bias_tiles(rel_table, tq):
    t_l = np.arange(tq)[:, None]
    s_l = np.arange(LANES)[None, :]
    bkt = np.stack([_rel_bucket_np(t_l + LANES - s_l), _rel_bucket_np(t_l - s_l)]).astype(np.int32)
    nh = rel_table.shape[1]
    return pl.pallas_call(
        functools.partial(_bias_kernel, sub_far=GROUP_HEADS),
        grid=(nh,),
        in_specs=[
            pl.BlockSpec(memory_space=pltpu.SMEM),
            pl.BlockSpec((2, tq, LANES), lambda h: (0, 0, 0)),
        ],
        out_specs=pl.BlockSpec((1, 2, tq, LANES), lambda h: (h, 0, 0, 0)),
        out_shape=jax.ShapeDtypeStruct((nh, 2, tq, LANES), F32),
        compiler_params=_cparams(("arbitrary",)),
        name="rel_bias_tiles",
    )(rel_table, jnp.asarray(bkt))


def _softmax_step(s, v, m_ref, l_ref, acc_ref, idx):
    reps = s.shape[1] // LANES
    m_prev = m_ref[idx]
    l_prev = l_ref[idx]
    m_cur = jnp.max(s, axis=1, keepdims=True)
    m_new = jnp.maximum(m_prev, m_cur)
    p = jnp.exp(s - _lane_tile(m_new, reps))
    alpha = jnp.exp(m_prev - m_new)
    l_ref[idx] = alpha * l_prev + jnp.sum(p, axis=1, keepdims=True)
    m_ref[idx] = m_new
    acc_ref[idx] = alpha * acc_ref[idx] + _dot(p.astype(BF16), v)


def _store_pairs(o_ref, per_head):
    for j in range(PAIRS):
        lane = lax.broadcasted_iota(I32, per_head[0].shape, 1)
        o_ref[:, j * LANES:(j + 1) * LANES] = jnp.where(
            lane < HEAD_DIM, per_head[2 * j], per_head[2 * j + 1]).astype(o_ref.dtype)


DSA_TQ = 128
DSA_KC1 = 256
DSA_KC = 512
DSA_SUB = DSA_KC // LANES


def _dsa_kernel(q_ref, iq_ref, iw_ref, k2_ref, v2_ref, ik2_ref, bias_ref, tri_ref, o_ref,
                key_ref, wb_ref, iqm_ref, qm_ref, t_ref, m_ref, l_ref, acc_ref, *, k_top):
    i = pl.program_id(1)
    tq = DSA_TQ
    row0 = i * tq
    n_valid_cols = row0 + tq

    iw = iw_ref[...]
    for h in range(IDX_HEADS):
        col = GROUP_HEADS + h
        wb_ref[h] = jnp.broadcast_to(iw[:, col:col + 1], (tq, LANES))
        blk = iq_ref[:, (h // 2) * LANES:(h // 2 + 1) * LANES]
        iqm_ref[h] = jnp.where(_own_lane_mask(blk.shape, h), blk, jnp.zeros_like(blk))
    for h in range(GROUP_HEADS):
        blk = q_ref[:, (h // 2) * LANES:(h // 2 + 1) * LANES]
        qm_ref[h] = jnp.where(_own_lane_mask(blk.shape, h), blk, jnp.zeros_like(blk))

    qpos = row0 + lax.broadcasted_iota(I32, (tq, DSA_KC1), 0)
    n1 = (n_valid_cols + DSA_KC1 - 1) // DSA_KC1

    def score_chunk(c, carry):
        start = pl.multiple_of(c * DSA_KC1, DSA_KC1)
        ik = ik2_ref[pl.ds(start, DSA_KC1), :]
        score = jnp.zeros((tq, DSA_KC1), F32)
        for h in range(IDX_HEADS):
            r = _nt_dot(iqm_ref[h], ik)
            score = score + jnp.maximum(r, 0.0) * _lane_tile(wb_ref[h], DSA_KC1 // LANES)
        bits = lax.bitcast_convert_type(score, I32)
        key = bits ^ ((bits >> 31) & 0x7FFFFFFF)
        key = jnp.where(score == 0.0, 0, key)
        kpos = start + lax.broadcasted_iota(I32, (tq, DSA_KC1), 1)
        key = jnp.where(kpos <= qpos, key, INT_MIN)
        for u in range(DSA_KC1 // LANES):
            key_ref[c * (DSA_KC1 // LANES) + u] = key[:, u * LANES:(u + 1) * LANES]
        return carry

    lax.fori_loop(0, n1, score_chunk, 0)
    nsel = (n_valid_cols + DSA_KC - 1) // DSA_KC
    n1_blocks = n1 * (DSA_KC1 // LANES)

    @pl.when(n1_blocks < nsel * DSA_SUB)
    def _():
        for u in range(DSA_KC1 // LANES):
            key_ref[n1_blocks + u] = jnp.full((tq, LANES), INT_MIN, I32)

    def load_keys(c):
        blk = key_ref[pl.ds(c * DSA_SUB, DSA_SUB)]
        return jnp.concatenate([blk[u] for u in range(DSA_SUB)], axis=1)

    def count_ge(thr, strict):
        def body(c, acc):
            keys = load_keys(c)
            hit = (keys > _lane_tile(thr, DSA_SUB)) if strict else (keys >= _lane_tile(thr, DSA_SUB))
            ones = jnp.where(hit, 1.0, 0.0)
            for u in range(DSA_SUB):
                acc = acc + ones[:, u * LANES:(u + 1) * LANES]
            return acc
        acc = lax.fori_loop(0, nsel, body, jnp.zeros((tq, LANES), F32))
        return jnp.broadcast_to(jnp.sum(acc, axis=1, keepdims=True), (tq, LANES))

    t_ref[...] = jnp.full((tq, LANES), INT_MIN, I32)
    kf = float(k_top)

    @pl.when(n_valid_cols > k_top)
    def _():
        def bit_step(it, carry):
            thr, cnt_thr = carry
            cand = thr + jnp.left_shift(jnp.int32(1), 31 - it)
            cnt = count_ge(cand, False)
            take = cnt >= kf
            return jnp.where(take, cand, thr), jnp.where(take, cnt, cnt_thr)

        thr0 = jnp.full((tq, LANES), INT_MIN, I32)
        cnt0 = jnp.full((tq, LANES), float(2 ** 24), F32)
        thr, cnt_thr = lax.fori_loop(0, 32, bit_step, (thr0, cnt0))
        t_ref[...] = thr

        excess = jnp.where(thr > INT_MIN, cnt_thr - kf, 0.0)

        @pl.when(jnp.max(excess) > 0.0)
        def _():
            need = kf - count_ge(thr, True)

            def tie_chunk(c, seen):
                keys = load_keys(c)
                eq = keys == _lane_tile(thr, DSA_SUB)
                eqf = jnp.where(eq, 1.0, 0.0)
                rank = _dot(eqf.astype(BF16), tri_ref[...]) + _lane_tile(seen, DSA_SUB)
                dropped = jnp.where(rank >= _lane_tile(need, DSA_SUB), INT_MIN, keys)
                keys = jnp.where(eq, dropped, keys)
                for u in range(DSA_SUB):
                    key_ref[c * DSA_SUB + u] = keys[:, u * LANES:(u + 1) * LANES]
                return seen + jnp.broadcast_to(jnp.sum(eqf, axis=1, keepdims=True), (tq, LANES))

            lax.fori_loop(0, nsel, tie_chunk, jnp.zeros((tq, LANES), F32))

    thr = jnp.maximum(t_ref[...], INT_MIN + 1)

    m_ref[...] = jnp.full(m_ref.shape, NEG, F32)
    l_ref[...] = jnp.zeros(l_ref.shape, F32)
    acc_ref[...] = jnp.zeros(acc_ref.shape, F32)

    s_l = lax.broadcasted_iota(I32, (tq, LANES), 1)
    prev_blk = jnp.maximum(i - 1, 0)
    prev_lim = jnp.where(i > 0, LANES, 0)
    for which, blk_idx in ((0, prev_blk), (1, i)):
        start = pl.multiple_of(blk_idx * LANES, LANES)
        kb = k2_ref[pl.ds(start, LANES), :]
        vb = v2_ref[pl.ds(start, LANES), :]
        keyb = key_ref[blk_idx]
        if which == 0:
            keyb = jnp.where(s_l < prev_lim, keyb, INT_MIN)
        sel = keyb >= thr
        for h in range(GROUP_HEADS):
            s = _nt_dot(qm_ref[h], kb) + bias_ref[h, which]
            _softmax_step(jnp.where(sel, s, NEG), vb, m_ref, l_ref, acc_ref, h)

    key_ref[prev_blk] = jnp.full((tq, LANES), INT_MIN, I32)
    key_ref[i] = jnp.full((tq, LANES), INT_MIN, I32)
    n_far = (prev_blk * LANES + DSA_KC - 1) // DSA_KC

    def far_chunk(c, carry):
        start = pl.multiple_of(c * DSA_KC, DSA_KC)
        kc = k2_ref[pl.ds(start, DSA_KC), :]
        vc = v2_ref[pl.ds(start, DSA_KC), :]
        madd = jnp.where(load_keys(c) >= _lane_tile(thr, DSA_SUB), 0.0, NEG)
        for h in range(GROUP_HEADS):
            _softmax_step(_nt_dot(qm_ref[h], kc) + madd, vc, m_ref, l_ref, acc_ref, h)
        return carry

    lax.fori_loop(0, n_far, far_chunk, 0)
    _store_pairs(o_ref, [acc_ref[h] / l_ref[h] for h in range(GROUP_HEADS)])


def _mixer_dsa(p, side, bias, tri, batch, seq):
    t = p.shape[0]
    tq = DSA_TQ
    nq = seq // tq
    k_top = min(TOPK_MAX, seq // 4)
    cw = PT // LANES
    return pl.pallas_call(
        functools.partial(_dsa_kernel, k_top=k_top),
        grid=(batch, nq),
        in_specs=[
            pl.BlockSpec((tq, PT), lambda b, i: (b * nq + i, T_AQ)),
            pl.BlockSpec((tq, 2 * PT), lambda b, i: (b * nq + i, T_IQ // 2)),
            pl.BlockSpec((tq, LANES), lambda b, i: (b * nq + i, 0)),
            pl.BlockSpec((seq, LANES), lambda b, i: (b, T_K2 * cw)),
            pl.BlockSpec((seq, LANES), lambda b, i: (b, T_MISC * cw)),
            pl.BlockSpec((seq, LANES), lambda b, i: (b, T_MISC * cw + 2)),
            pl.BlockSpec((GROUP_HEADS, 2, tq, LANES), lambda b, i: (0, 0, 0, 0)),
            pl.BlockSpec((DSA_KC, DSA_KC), lambda b, i: (0, 0)),
        ],
        out_specs=pl.BlockSpec((tq, GROUP_WIDTH), lambda b, i: (b * nq + i, 0)),
        out_shape=jax.ShapeDtypeStruct((t, GROUP_WIDTH), F32),
        scratch_shapes=[
            pltpu.VMEM((max(seq // LANES, DSA_SUB), tq, LANES), I32),
            pltpu.VMEM((IDX_HEADS, tq, LANES), F32),
            pltpu.VMEM((IDX_HEADS, tq, LANES), BF16),
            pltpu.VMEM((GROUP_HEADS, tq, LANES), BF16),
            pltpu.VMEM((tq, LANES), I32),
            pltpu.VMEM((GROUP_HEADS, tq, LANES), F32),
            pltpu.VMEM((GROUP_HEADS, tq, LANES), F32),
            pltpu.VMEM((GROUP_HEADS, tq, LANES), F32),
        ],
        compiler_params=_cparams(("arbitrary", "arbitrary")),
        name="mixer_dsa",
    )(p, p, side, p, p, p, bias, tri)


SWA_TQ = 128


def _swa_kernel(sink_ref, q_ref, kp_ref, kc_ref, vp_ref, vc_ref, bias_ref, o_ref):
    i = pl.program_id(1)
    tq = SWA_TQ
    t_l = lax.broadcasted_iota(I32, (tq, LANES), 0)
    s_l = lax.broadcasted_iota(I32, (tq, LANES), 1)
    ok_prev = s_l > t_l + jnp.where(i > 0, 0, LANES)
    ok_cur = s_l <= t_l
    kp, kc, vp, vc = kp_ref[...], kc_ref[...], vp_ref[...], vc_ref[...]
    outs = []
    for h in range(GROUP_HEADS):
        blk = q_ref[:, (h // 2) * LANES:(h // 2 + 1) * LANES]
        qm = jnp.where(_own_lane_mask(blk.shape, h), blk, jnp.zeros_like(blk))
        s_p = jnp.where(ok_prev, _nt_dot(qm, kp) + bias_ref[h, 0], NEG)
        s_c = jnp.where(ok_cur, _nt_dot(qm, kc) + bias_ref[h, 1], NEG)
        sink = sink_ref[h]
        m = jnp.maximum(jnp.max(jnp.maximum(s_p, s_c), axis=1, keepdims=True), sink)
        p_p = jnp.exp(s_p - m)
        p_c = jnp.exp(s_c - m)
        l = jnp.sum(p_p + p_c, axis=1, keepdims=True) + jnp.exp(sink - m)
        outs.append((_dot(p_p.astype(BF16), vp) + _dot(p_c.astype(BF16), vc)) / l)
    _store_pairs(o_ref, outs)


def _mixer_swa(p, sinks, bias, batch, seq):
    t = p.shape[0]
    tq = SWA_TQ
    nq = seq // tq
    cw = PT // LANES
    cur = lambda col: (lambda b, i: (b * nq + i, col))
    prev = lambda col: (lambda b, i: (b * nq + jnp.maximum(i - 1, 0), col))
    return pl.pallas_call(
        _swa_kernel,
        grid=(batch, nq),
        in_specs=[
            pl.BlockSpec(memory_space=pltpu.SMEM),
            pl.BlockSpec((tq, PT), lambda b, i: (b * nq + i, T_BQ)),
            pl.BlockSpec((tq, LANES), prev(T_K2 * cw + 1)),
            pl.BlockSpec((tq, LANES), cur(T_K2 * cw + 1)),
            pl.BlockSpec((tq, LANES), prev(T_MISC * cw + 1)),
            pl.BlockSpec((tq, LANES), cur(T_MISC * cw + 1)),
            pl.BlockSpec((GROUP_HEADS, 2, tq, LANES), lambda b, i: (1, 0, 0, 0)),
        ],
        out_specs=pl.BlockSpec((tq, GROUP_WIDTH), lambda b, i: (b * nq + i, 0)),
        out_shape=jax.ShapeDtypeStruct((t, GROUP_WIDTH), F32),
        compiler_params=_cparams(("arbitrary", "arbitrary")),
        name="mixer_swa",
    )(sinks, p, p, p, p, p, bias)


def _logsig(x):
    return jnp.minimum(x, 0.0) - jnp.log(1.0 + jnp.exp(-jnp.abs(x)))


def _foxcum_kernel(fb_ref, f_ref, tri_ref, o_ref, *, nchunk):
    def body(c, carry):
        start = pl.multiple_of(c * LANES, LANES)
        lf = _logsig(f_ref[pl.ds(start, LANES), :] + fb_ref[...]).T
        p1 = lf.astype(BF16)
        r1 = lf - p1.astype(F32)
        p2 = r1.astype(BF16)
        p3 = (r1 - p2.astype(F32)).astype(BF16)
        tri = tri_ref[...]
        cum = _dot(p1, tri) + _dot(p2, tri) + _dot(p3, tri) + carry
        o_ref[0, c] = cum[:GROUP_HEADS]
        return jnp.broadcast_to(cum[:, LANES - 1:LANES], cum.shape)

    lax.fori_loop(0, nchunk, body, jnp.zeros((LANES, LANES), F32))


def _fox_cum(side, fb_row, tri_incl, batch, seq):
    nchunk = seq // LANES
    return pl.pallas_call(
        functools.partial(_foxcum_kernel, nchunk=nchunk),
        grid=(batch,),
        in_specs=[
            pl.BlockSpec((1, LANES), lambda b: (0, 0)),
            pl.BlockSpec((seq, LANES), lambda b: (b, 0)),
            pl.BlockSpec((LANES, LANES), lambda b: (0, 0)),
        ],
        out_specs=pl.BlockSpec((1, nchunk, GROUP_HEADS, LANES), lambda b: (b, 0, 0, 0)),
        out_shape=jax.ShapeDtypeStruct((batch, nchunk, GROUP_HEADS, LANES), F32),
        compiler_params=_cparams(("arbitrary",)),
        name="fox_cumsum",
    )(fb_row, side, tri_incl)


FOX_TQ = 256
FOX_KC = 256
FOX_SUB = FOX_KC // LANES


def _fox_kernel(q_ref, g_ref, k_ref, v_ref, cum_ref, o_ref, qm_ref, m_ref, l_ref, acc_ref):
    i = pl.program_id(1)
    tq = FOX_TQ
    for h in range(GROUP_HEADS):
        blk = q_ref[:, (h // 2) * LANES:(h // 2 + 1) * LANES]
        qm_ref[h] = jnp.where(_own_lane_mask(blk.shape, h), blk, jnp.zeros_like(blk))
    m_ref[...] = jnp.full(m_ref.shape, NEG, F32)
    l_ref[...] = jnp.zeros(l_ref.shape, F32)
    acc_ref[...] = jnp.zeros(acc_ref.shape, F32)
    f_ref0 = cum_ref[0, i * (tq // LANES)][:, 0:1]

    def chunk(c, masked):
        start = pl.multiple_of(c * FOX_KC, FOX_KC)
        cum = cum_ref[0, pl.ds(c * FOX_SUB, FOX_SUB)]
        cum = jnp.concatenate([cum[u] for u in range(FOX_SUB)], axis=1)
        fbias = f_ref0 - cum
        if masked:
            qpos = i * tq + lax.broadcasted_iota(I32, (tq, FOX_KC), 0)
            kpos = start + lax.broadcasted_iota(I32, (tq, FOX_KC), 1)
            ok = kpos <= qpos
        for h in range(GROUP_HEADS):
            pair = slice((h // 2) * LANES, (h // 2 + 1) * LANES)
            s = _nt_dot(qm_ref[h], k_ref[pl.ds(start, FOX_KC), pair]) + fbias[h:h + 1, :]
            if masked:
                s = jnp.where(ok, s, NEG)
            _softmax_step(s, v_ref[pl.ds(start, FOX_KC), pair], m_ref, l_ref, acc_ref, h)

    def body(c, carry):
        chunk(c, False)
        return carry

    n_full = (i * tq) // FOX_KC
    lax.fori_loop(0, n_full, body, 0)
    for c_off in range(tq // FOX_KC):
        chunk(n_full + c_off, True)
    outs = []
    for h in range(GROUP_HEADS):
        gate = _sigmoid(g_ref[:, (h // 2) * LANES:(h // 2 + 1) * LANES].astype(F32))
        outs.append(acc_ref[h] / l_ref[h] * gate)
    _store_pairs(o_ref, outs)


def _mixer_fox(p, cum, batch, seq):
    t = p.shape[0]
    tq = FOX_TQ
    nq = seq // tq
    return pl.pallas_call(
        _fox_kernel,
        grid=(batch, nq),
        in_specs=[
            pl.BlockSpec((tq, PT), lambda b, i: (b * nq + i, T_CQ)),
            pl.BlockSpec((tq, PT), lambda b, i: (b * nq + i, T_CG)),
            pl.BlockSpec((seq, PT), lambda b, i: (b, T_CK)),
            pl.BlockSpec((seq, PT), lambda b, i: (b, T_CV)),
            pl.BlockSpec((1, seq // LANES, GROUP_HEADS, LANES), lambda b, i: (b, 0, 0, 0)),
        ],
        out_specs=pl.BlockSpec((tq, GROUP_WIDTH), lambda b, i: (b * nq + i, 0)),
        out_shape=jax.ShapeDtypeStruct((t, GROUP_WIDTH), F32),
        scratch_shapes=[
            pltpu.VMEM((GROUP_HEADS, tq, LANES), BF16),
            pltpu.VMEM((GROUP_HEADS, tq, LANES), F32),
            pltpu.VMEM((GROUP_HEADS, tq, LANES), F32),
            pltpu.VMEM((GROUP_HEADS, tq, LANES), F32),
        ],
        compiler_params=_cparams(("arbitrary", "arbitrary")),
        name="mixer_fox",
    )(p, p, p, p, cum)


STK_TQ = 128


def _stick_kernel(q_ref, k_ref, v_ref, tri_ref, o_ref, acc_ref, r_ref):
    i = pl.program_id(1)
    tq = STK_TQ
    t_l = lax.broadcasted_iota(I32, (tq, LANES), 0)
    s_l = lax.broadcasted_iota(I32, (tq, LANES), 1)
    before = s_l < t_l
    tri = tri_ref[...]
    outs = []
    for h in range(GROUP_HEADS):
        pair = slice((h // 2) * LANES, (h // 2 + 1) * LANES)
        blk = q_ref[:, pair]
        qm = jnp.where(_own_lane_mask(blk.shape, h), blk, jnp.zeros_like(blk))
        acc_ref[...] = jnp.zeros(acc_ref.shape, F32)
        r_ref[...] = jnp.zeros(r_ref.shape, F32)

        def block(j, masked):
            start = pl.multiple_of(j * LANES, LANES)
            z = _nt_dot(qm, k_ref[pl.ds(start, LANES), pair])
            lsz = _logsig(z)
            u = lsz - z
            if masked:
                u = jnp.where(before, u, 0.0)
            u_hi = u.astype(BF16)
            u_lo = (u - u_hi.astype(F32)).astype(BF16)
            run = r_ref[...]
            w = jnp.exp(lsz + _dot(u_hi, tri) + _dot(u_lo, tri) + run)
            if masked:
                w = jnp.where(before, w, 0.0)
            acc_ref[...] += _dot(w.astype(BF16), v_ref[pl.ds(start, LANES), pair])
            run = run + jnp.sum(u, axis=1, keepdims=True)
            r_ref[...] = run
            return jnp.max(run)

        rmax = block(i, True)

        def cond(carry):
            j, rmax = carry
            return jnp.logical_and(j >= 0, rmax >= STICK_EXIT)

        def body(carry):
            j, _ = carry
            return j - 1, block(j, False)

        lax.while_loop(cond, body, (i - 1, rmax))
        outs.append(acc_ref[...])
    _store_pairs(o_ref, outs)


def _mixer_stick(p, tri_excl, batch, seq):
    t = p.shape[0]
    tq = STK_TQ
    nq = seq // tq
    return pl.pallas_call(
        _stick_kernel,
        grid=(batch, nq),
        in_specs=[
            pl.BlockSpec((tq, PT), lambda b, i: (b * nq + i, T_DQ)),
            pl.BlockSpec((seq, PT), lambda b, i: (b, T_DK)),
            pl.BlockSpec((seq, PT), lambda b, i: (b, T_DV)),
            pl.BlockSpec((LANES, LANES), lambda b, i: (0, 0)),
        ],
        out_specs=pl.BlockSpec((tq, GROUP_WIDTH), lambda b, i: (b * nq + i, 0)),
        out_shape=jax.ShapeDtypeStruct((t, GROUP_WIDTH), F32),
        scratch_shapes=[pltpu.VMEM((tq, LANES), F32), pltpu.VMEM((tq, LANES), F32)],
        compiler_params=_cparams(("arbitrary", "arbitrary")),
        name="mixer_stick",
    )(p, p, p, tri_excl)


def _outproj_kernel(x_ref, gt_ref, oa_ref, ob_ref, oc_ref, od_ref, gg_ref, w_ref, o_ref):
    acc = None
    for m, ref in enumerate((oa_ref, ob_ref, oc_ref, od_ref)):
        o = ref[...]
        y = o * lax.rsqrt(jnp.mean(o * o, axis=-1, keepdims=True) + EPS) * gg_ref[m:m + 1, :]
        part = _dot(y.astype(BF16), w_ref[m * GROUP_WIDTH:(m + 1) * GROUP_WIDTH, :])
        acc = part if acc is None else acc + part
    o_ref[...] = x_ref[...] + gt_ref[0] * acc


def _outproj(x, gt, outs, gg, w, l, seq):
    t, d = x.shape
    tm = 512
    per_b = seq // tm
    mix = pl.BlockSpec((tm, GROUP_WIDTH), lambda i: (i, 0))
    return pl.pallas_call(
        _outproj_kernel,
        grid=(t // tm,),
        in_specs=[
            pl.BlockSpec((tm, d), lambda i: (i, 0)),
            pl.BlockSpec((1, 1, d), lambda i: (i // per_b, 0, 0)),
            mix, mix, mix, mix,
            pl.BlockSpec((None, N_MIXERS, GROUP_WIDTH), lambda i: (l, 0, 0)),
            pl.BlockSpec((None, N_MIXERS * GROUP_WIDTH, d), lambda i: (l, 0, 0)),
        ],
        out_specs=pl.BlockSpec((tm, d), lambda i: (i, 0)),
        out_shape=jax.ShapeDtypeStruct((t, d), F32),
        compiler_params=_cparams(("arbitrary",)),
        name="outproj",
    )(x, gt, *outs, gg, w)


def _prep_in_weights(w_in, qk_g):
    depth, d, _ = w_in.shape
    sizes = (GROUP_WIDTH, HEAD_DIM, HEAD_DIM, IDX_HEADS * IDX_DIM, IDX_DIM, IDX_HEADS,
             GROUP_WIDTH, HEAD_DIM, HEAD_DIM,
             GROUP_WIDTH, GROUP_WIDTH, GROUP_WIDTH, GROUP_HEADS, GROUP_WIDTH,
             GROUP_WIDTH, GROUP_WIDTH, GROUP_WIDTH)
    pts = np.cumsum(sizes)[:-1].tolist()
    (a_q, a_k, a_v, a_iq, a_ik, a_iw, b_q, b_k, b_v,
     c_q, c_k, c_v, c_f, c_g, d_q, d_k, d_v) = jnp.split(w_in, pts, axis=-1)
    z = lambda n: jnp.zeros((depth, d, n), w_in.dtype)
    qscale = HEAD_DIM ** -0.5
    main = jnp.concatenate([
        a_q, b_q, c_q, c_k,
        a_k, a_k, b_k, b_k, z(2 * LANES),
        c_v, c_g, d_q * qscale, d_k, d_v, a_iq * (IDX_DIM ** -0.5),
        a_v, a_v, b_v, b_v, a_ik, a_ik, z(LANES),
    ], axis=-1).astype(BF16)
    side = jnp.concatenate([c_f, a_iw * (IDX_HEADS ** -0.5), z(LANES - GROUP_HEADS - IDX_HEADS)],
                           axis=-1).astype(BF16)
    rep = lambda g, n: jnp.tile(g, (1, n))
    gain = jnp.concatenate([
        rep(qk_g[:, 0], 8) * qscale, rep(qk_g[:, 2], 8) * qscale, rep(qk_g[:, 4], 8) * qscale,
        rep(qk_g[:, 5], 8), rep(qk_g[:, 1], 2), rep(qk_g[:, 3], 2),
        jnp.zeros((depth, P_COLS - 4 * PT - 2 * LANES), F32),
    ], axis=-1).reshape(depth, 1, P_COLS)
    return main, side, gain


def _np_const(shape_fn):
    return jnp.asarray(shape_fn())


def kernel(x, c, w_ada, b_ada, norm_g, w_in, qk_g, forget_b, sinks, rel_table, group_g, w_out,
           w_ffn_gate, w_ffn_up, w_ffn_down):
    batch, seq, d = x.shape
    depth = w_ada.shape[0]
    t = batch * seq

    mod = _modulation(c, w_ada, b_ada)
    wg = w_ffn_gate.astype(BF16)
    wu = w_ffn_up.astype(BF16)
    wd = w_ffn_down.astype(BF16)
    wo = w_out.astype(BF16)
    w_main, w_side, gain = _prep_in_weights(w_in, qk_g)
    gain_side = jnp.zeros((depth, 1, LANES), F32)

    r = np.arange(PT)
    bd = jnp.asarray((r[:, None] // HEAD_DIM == r[None, :] // HEAD_DIM).astype(np.float32) / HEAD_DIM, BF16)
    bd_side = jnp.zeros((LANES, LANES), BF16)
    r = np.arange(LANES)
    tri_incl = jnp.asarray(r[:, None] <= r[None, :], BF16)
    tri_after = jnp.asarray(r[:, None] > r[None, :], BF16)
    r = np.arange(DSA_KC)
    tri_before = jnp.asarray(r[:, None] < r[None, :], BF16)

    bias = _bias_tiles(rel_table, DSA_TQ)
    fb_rows = jnp.zeros((depth, 1, LANES), F32).at[:, 0, :GROUP_HEADS].set(forget_b)
    gg = group_g.reshape(depth, N_MIXERS, GROUP_WIDTH)

    xt = x.reshape(t, d)
    for l in range(depth):
        parts = [m.reshape(batch, 1, d) for m in jnp.split(mod[l], 9, axis=-1)]
        sh1, sc1, g1, sh2, sc2, g2, sh3, sc3, g3 = parts
        xt = _ffn(xt, norm_g[l, 0:1], sh1, sc1, g1, wg, wu, wd, l, 0, seq)
        ng2 = norm_g[l, 1:2]
        p = _inproj(xt, ng2, sh2, sc2, w_main, gain, bd, l, seq, N_NORM_TILES, BF16)
        side = _inproj(xt, ng2, sh2, sc2, w_side, gain_side, bd_side, l, seq, 0, F32)
        o_a = _mixer_dsa(p, side, bias, tri_before, batch, seq)
        o_b = _mixer_swa(p, sinks[l], bias, batch, seq)
        cum = _fox_cum(side, fb_rows[l], tri_incl, batch, seq)
        o_c = _mixer_fox(p, cum, batch, seq)
        o_d = _mixer_stick(p, tri_after, batch, seq)
        xt = _outproj(xt, g2, (o_a, o_b, o_c, o_d), gg, wo, l, seq)
        xt = _ffn(xt, norm_g[l, 2:3], sh3, sc3, g3, wg, wu, wd, l, 1, seq)
    return xt.reshape(batch, seq, d)
```

```python
import functools
import math

import numpy as np
import jax
import jax.numpy as jnp
from jax import lax
from jax.experimental import pallas as pl
from jax.experimental.pallas import tpu as pltpu

F32 = jnp.float32
BF16 = jnp.bfloat16
I32 = jnp.int32

HEAD_DIM = 64
N_MIXERS = 4
GROUP_HEADS = 8
GROUP_WIDTH = GROUP_HEADS * HEAD_DIM
IDX_HEADS = 16
IDX_DIM = 64
TOPK_MAX = 256
WINDOW = 128
REL_BUCKETS = 32
REL_MAX_DIST = 128
EPS = 1e-6

LANES = 128
PAIRS = GROUP_HEADS // 2
NEG = -1e30
INT_MIN = -2 ** 31
STICK_EXIT = -110.0
VMEM_LIMIT = 56 * 1024 * 1024

PT = 512
T_AQ, T_BQ, T_CQ, T_CK, T_K2 = 0, 1, 2, 3, 4
N_NORM_TILES = 5
T_CV, T_CG, T_DQ, T_DK, T_DV, T_IQ, T_MISC = 5, 6, 7, 8, 9, 10, 12
N_TILES = 13
P_COLS = N_TILES * PT


def _cparams(sem, vmem=VMEM_LIMIT):
    return pltpu.CompilerParams(dimension_semantics=sem, vmem_limit_bytes=vmem)


def _nt_dot(a, b):
    return lax.dot_general(a, b, (((1,), (1,)), ((), ())), preferred_element_type=F32)


def _dot(a, b):
    return jnp.dot(a, b, preferred_element_type=F32)


def _sigmoid(x):
    return 1.0 / (1.0 + jnp.exp(-x))


def _lane_tile(x, n):
    return x if n == 1 else jnp.concatenate([x] * n, axis=1)


def _own_lane_mask(shape, h):
    lane = lax.broadcasted_iota(I32, shape, 1)
    return (lane < HEAD_DIM) if h % 2 == 0 else (lane >= HEAD_DIM)


def _mod_kernel(c_ref, w_ref, b_ref, o_ref):
    c = c_ref[...]
    cond = (c * _sigmoid(c)).astype(BF16)
    o_ref[0] = _dot(cond, w_ref[0].astype(BF16)) + b_ref[0]


def _modulation(c, w_ada, b_ada):
    depth, d, n = w_ada.shape
    b = c.shape[0]
    rows = 8
    tn = 1024
    c_pad = jnp.zeros((rows, d), F32).at[:b].set(c)
    out = pl.pallas_call(
        _mod_kernel,
        grid=(depth, n // tn),
        in_specs=[
            pl.BlockSpec((rows, d), lambda l, j: (0, 0)),
            pl.BlockSpec((1, d, tn), lambda l, j: (l, 0, j)),
            pl.BlockSpec((1, 1, tn), lambda l, j: (l, 0, j)),
        ],
        out_specs=pl.BlockSpec((1, rows, tn), lambda l, j: (l, 0, j)),
        out_shape=jax.ShapeDtypeStruct((depth, rows, n), F32),
        compiler_params=_cparams(("arbitrary", "arbitrary")),
        name="adaln_mod",
    )(c_pad, w_ada, b_ada.reshape(depth, 1, n))
    return out[:, :b]


def _norm_modulate(x, ng, sh, sc):
    ms = jnp.mean(x * x, axis=-1, keepdims=True)
    y = x * lax.rsqrt(ms + EPS) * ng
    return y * (1.0 + sc) + sh


def _ffn_kernel(x_ref, ng_ref, sh_ref, sc_ref, gt_ref, wg_ref, wu_ref, wd_ref, o_ref, h_ref, acc_ref):
    f = pl.program_id(1)

    @pl.when(f == 0)
    def _():
        h = _norm_modulate(x_ref[...], ng_ref[...], sh_ref[0], sc_ref[0])
        h_ref[...] = h.astype(BF16)
        acc_ref[...] = jnp.zeros_like(acc_ref)

    h = h_ref[...]
    g = _dot(h, wg_ref[...])
    u = _dot(h, wu_ref[...])
    a = (g * _sigmoid(g) * u).astype(BF16)
    acc_ref[...] += _dot(a, wd_ref[...])

    @pl.when(f == pl.num_programs(1) - 1)
    def _():
        o_ref[...] = x_ref[...] + 0.5 * gt_ref[0] * acc_ref[...]


def _ffn(x, ng, sh, sc, gt, wg, wu, wd, l, s, seq):
    t, d = x.shape
    dff = wg.shape[-1]
    tm, tf = 512, 512
    per_b = seq // tm
    vec = pl.BlockSpec((1, 1, d), lambda i, f: (i // per_b, 0, 0))
    return pl.pallas_call(
        _ffn_kernel,
        grid=(t // tm, dff // tf),
        in_specs=[
            pl.BlockSpec((tm, d), lambda i, f: (i, 0)),
            pl.BlockSpec((1, d), lambda i, f: (0, 0)),
            vec, vec, vec,
            pl.BlockSpec((None, None, d, tf), lambda i, f: (l, s, 0, f)),
            pl.BlockSpec((None, None, d, tf), lambda i, f: (l, s, 0, f)),
            pl.BlockSpec((None, None, tf, d), lambda i, f: (l, s, f, 0)),
        ],
        out_specs=pl.BlockSpec((tm, d), lambda i, f: (i, 0)),
        out_shape=jax.ShapeDtypeStruct((t, d), F32),
        scratch_shapes=[pltpu.VMEM((tm, d), BF16), pltpu.VMEM((tm, d), F32)],
        compiler_params=_cparams(("arbitrary", "arbitrary")),
        name="ffn",
    )(x, ng, sh, sc, gt, wg, wu, wd)


def _inproj_kernel(x_ref, ng_ref, sh_ref, sc_ref, w_ref, gain_ref, bd_ref, o_ref, h_ref, *, n_norm):
    j = pl.program_id(1)

    @pl.when(j == 0)
    def _():
        h = _norm_modulate(x_ref[...], ng_ref[...], sh_ref[0], sc_ref[0])
        h_ref[...] = h.astype(BF16)

    y = _dot(h_ref[...], w_ref[...])

    @pl.when(j < n_norm)
    def _():
        sq = y * y
        hi = sq.astype(BF16)
        lo = (sq - hi.astype(F32)).astype(BF16)
        ms = _dot(hi, bd_ref[...]) + _dot(lo, bd_ref[...])
        o_ref[...] = (y * lax.rsqrt(ms + EPS) * gain_ref[...]).astype(o_ref.dtype)

    @pl.when(j >= n_norm)
    def _():
        o_ref[...] = y.astype(o_ref.dtype)


def _inproj(x, ng, sh, sc, w, gain, bd, l, seq, n_norm, out_dtype):
    t, d = x.shape
    n = w.shape[-1]
    tm = 512
    tn = bd.shape[0]
    per_b = seq // tm
    vec = pl.BlockSpec((1, 1, d), lambda i, j: (i // per_b, 0, 0))
    return pl.pallas_call(
        functools.partial(_inproj_kernel, n_norm=n_norm),
        grid=(t // tm, n // tn),
        in_specs=[
            pl.BlockSpec((tm, d), lambda i, j: (i, 0)),
            pl.BlockSpec((1, d), lambda i, j: (0, 0)),
            vec, vec,
            pl.BlockSpec((None, d, tn), lambda i, j: (l, 0, j)),
            pl.BlockSpec((None, 1, tn), lambda i, j: (l, 0, j)),
            pl.BlockSpec((tn, tn), lambda i, j: (0, 0)),
        ],
        out_specs=pl.BlockSpec((tm, tn), lambda i, j: (i, j)),
        out_shape=jax.ShapeDtypeStruct((t, n), out_dtype),
        scratch_shapes=[pltpu.VMEM((tm, d), BF16)],
        compiler_params=_cparams(("arbitrary", "arbitrary")),
        name="inproj",
    )(x, ng, sh, sc, w, gain, bd)


def _rel_bucket_np(dist):
    n = np.maximum(dist, 0)
    max_exact = REL_BUCKETS // 2
    nf = np.maximum(n, 1).astype(np.float32)
    large = max_exact + (np.log(nf / np.float32(max_exact)) / np.float32(math.log(REL_MAX_DIST / max_exact))
                         * np.float32(REL_BUCKETS - max_exact)).astype(np.int32)
    large = np.minimum(large, REL_BUCKETS - 1)
    return np.where(n < max_exact, n, large).astype(np.int32)


def _bias_kernel(tab_ref, bkt_ref, o_ref, *, sub_far):
    h = pl.program_id(0)
    far = tab_ref[REL_BUCKETS - 1, h]
    for which in range(2):
        bkt = bkt_ref[which]
        acc = jnp.zeros(bkt.shape, F32)
        for b in range(REL_BUCKETS):
            acc = jnp.where(bkt == b, tab_ref[b, h], acc)
        o_ref[0, which] = acc - jnp.where(h < sub_far, far, 0.0)


def _bias_tiles(rel_table, tq):
    t_l = np.arange(tq)[:, None]
    s_l = np.arange(LANES)[None, :]
    bkt = np.stack([_rel_bucket_np(t_l + LANES - s_l), _rel_bucket_np(t_l - s_l)]).astype(np.int32)
    nh = rel_table.shape[1]
    return pl.pallas_call(
        functools.partial(_bias_kernel, sub_far=GROUP_HEADS),
        grid=(nh,),
        in_specs=[
            pl.BlockSpec(memory_space=pltpu.SMEM),
            pl.BlockSpec((2, tq, LANES), lambda h: (0, 0, 0)),
        ],
        out_specs=pl.BlockSpec((1, 2, tq, LANES), lambda h: (h, 0, 0, 0)),
        out_shape=jax.ShapeDtypeStruct((nh, 2, tq, LANES), F32),
        compiler_params=_cparams(("arbitrary",)),
        name="rel_bias_tiles",
    )(rel_table, jnp.asarray(bkt))


def _dsa_window_bias(bias):
    prev, cur = bias[:GROUP_HEADS, 0], bias[:GROUP_HEADS, 1]
    return jnp.stack([jnp.concatenate([cur, cur], axis=-1), jnp.concatenate([prev, cur], axis=-1)])


def _softmax_step(s, v, m_ref, l_ref, acc_ref, idx):
    reps = s.shape[1] // LANES
    m_prev = m_ref[idx]
    l_prev = l_ref[idx]
    m_cur = jnp.max(s, axis=1, keepdims=True)
    m_new = jnp.maximum(m_prev, m_cur)
    p = jnp.exp(s - _lane_tile(m_new, reps))
    alpha = jnp.exp(m_prev - m_new)
    l_ref[idx] = alpha * l_prev + jnp.sum(p, axis=1, keepdims=True)
    m_ref[idx] = m_new
    acc_ref[idx] = alpha * acc_ref[idx] + _dot(p.astype(BF16), v)


def _softmax_step_stacked(s, v, m_ref, l_ref, acc_ref):
    nh, rows, width = s.shape
    reps = width // LANES
    m_prev = m_ref[...]
    m_new = jnp.maximum(m_prev, jnp.max(s, axis=2, keepdims=True))
    p = jnp.exp(s - jnp.concatenate([m_new] * reps, axis=2))
    alpha = jnp.exp(m_prev - m_new)
    l_ref[...] = alpha * l_ref[...] + jnp.sum(p, axis=2, keepdims=True)
    m_ref[...] = m_new
    pv = _dot(p.astype(BF16).reshape(nh * rows, width), v).reshape(nh, rows, LANES)
    acc_ref[...] = alpha * acc_ref[...] + pv


def _store_pairs(o_ref, per_head):
    for j in range(PAIRS):
        lane = lax.broadcasted_iota(I32, per_head[0].shape, 1)
        o_ref[:, j * LANES:(j + 1) * LANES] = jnp.where(
            lane < HEAD_DIM, per_head[2 * j], per_head[2 * j + 1]).astype(o_ref.dtype)


DSA_TQ = 128
DSA_KC1 = 256
DSA_KC = 512
DSA_SUB = DSA_KC // LANES


def _dsa_kernel(q_ref, iq_ref, iw_ref, k2_ref, v2_ref, ik2_ref, bias_ref, tri_ref, o_ref,
                key_ref, wb_ref, iqm_ref, qm_ref, t_ref, m_ref, l_ref, acc_ref, *, k_top):
    i = pl.program_id(1)
    tq = DSA_TQ
    row0 = i * tq
    n_valid_cols = row0 + tq

    iw = iw_ref[...]
    for h in range(IDX_HEADS):
        col = GROUP_HEADS + h
        wb_ref[h] = jnp.broadcast_to(iw[:, col:col + 1], (tq, LANES))
        blk = iq_ref[:, (h // 2) * LANES:(h // 2 + 1) * LANES]
        iqm_ref[h] = jnp.where(_own_lane_mask(blk.shape, h), blk, jnp.zeros_like(blk))
    for h in range(GROUP_HEADS):
        blk = q_ref[:, (h // 2) * LANES:(h // 2 + 1) * LANES]
        qm_ref[h] = jnp.where(_own_lane_mask(blk.shape, h), blk, jnp.zeros_like(blk))

    qpos = row0 + lax.broadcasted_iota(I32, (tq, DSA_KC1), 0)
    iq_all = iqm_ref[...].reshape(IDX_HEADS * tq, LANES)
    wb_all = jnp.concatenate([wb_ref[...]] * (DSA_KC1 // LANES), axis=2)
    n1 =(n_valid_cols + DSA_KC1 - 1) // DSA_KC1

    def score_chunk(c, carry):
        start = pl.multiple_of(c * DSA_KC1, DSA_KC1)
        ik = ik2_ref[pl.ds(start, DSA_KC1), :]
        r = _nt_dot(iq_all, ik).reshape(IDX_HEADS, tq, DSA_KC1)
        score = jnp.sum(jnp.maximum(r, 0.0) * wb_all, axis=0)
        bits = lax.bitcast_convert_type(score, I32)
        key = bits ^ ((bits >> 31) & 0x7FFFFFFF)
        key = jnp.where(score == 0.0, 0, key)
        kpos = start + lax.broadcasted_iota(I32, (tq, DSA_KC1), 1)
        key = jnp.where(kpos <= qpos, key, INT_MIN)
        for u in range(DSA_KC1 // LANES):
            key_ref[c * (DSA_KC1 // LANES) + u] = key[:, u * LANES:(u + 1) * LANES]
        return carry

    lax.fori_loop(0, n1, score_chunk, 0)
    nsel = (n_valid_cols + DSA_KC - 1) // DSA_KC
    n1_blocks = n1 * (DSA_KC1 // LANES)

    @pl.when(n1_blocks < nsel * DSA_SUB)
    def _():
        for u in range(DSA_KC1 // LANES):
            key_ref[n1_blocks + u] = jnp.full((tq, LANES), INT_MIN, I32)

    def load_keys(c):
        blk = key_ref[pl.ds(c * DSA_SUB, DSA_SUB)]
        return jnp.concatenate([blk[u] for u in range(DSA_SUB)], axis=1)

    def count_ge(thr, strict):
        def body(c, acc):
            keys = load_keys(c)
            hit = (keys > _lane_tile(thr, DSA_SUB)) if strict else (keys >= _lane_tile(thr, DSA_SUB))
            ones = jnp.where(hit, 1.0, 0.0)
            for u in range(DSA_SUB):
                acc = acc + ones[:, u * LANES:(u + 1) * LANES]
            return acc
        acc = lax.fori_loop(0, nsel, body, jnp.zeros((tq, LANES), F32))
        return jnp.broadcast_to(jnp.sum(acc, axis=1, keepdims=True), (tq, LANES))

    t_ref[...] = jnp.full((tq, LANES), INT_MIN, I32)
    kf = float(k_top)

    @pl.when(n_valid_cols > k_top)
    def _():
        def bit_step(it, carry):
            thr, cnt_thr = carry
            cand = thr + jnp.left_shift(jnp.int32(1), 31 - it)
            cnt = count_ge(cand, False)
            take = cnt >= kf
            return jnp.where(take, cand, thr), jnp.where(take, cnt, cnt_thr)

        thr0 = jnp.full((tq, LANES), INT_MIN, I32)
        cnt0 = jnp.full((tq, LANES), float(2 ** 24), F32)
        thr, cnt_thr = lax.fori_loop(0, 32, bit_step, (thr0, cnt0))
        t_ref[...] = thr

        excess = jnp.where(thr > INT_MIN, cnt_thr - kf, 0.0)

        @pl.when(jnp.max(excess) > 0.0)
        def _():
            need = kf - count_ge(thr, True)

            def tie_chunk(c, seen):
                keys = load_keys(c)
                eq = keys == _lane_tile(thr, DSA_SUB)
                eqf = jnp.where(eq, 1.0, 0.0)
                rank = _dot(eqf.astype(BF16), tri_ref[...]) + _lane_tile(seen, DSA_SUB)
                dropped = jnp.where(rank >= _lane_tile(need, DSA_SUB), INT_MIN, keys)
                keys = jnp.where(eq, dropped, keys)
                for u in range(DSA_SUB):
                    key_ref[c * DSA_SUB + u] = keys[:, u * LANES:(u + 1) * LANES]
                return seen + jnp.broadcast_to(jnp.sum(eqf, axis=1, keepdims=True), (tq, LANES))

            lax.fori_loop(0, nsel, tie_chunk, jnp.zeros((tq, LANES), F32))

    thr = jnp.maximum(t_ref[...], INT_MIN + 1)

    m_ref[...] = jnp.full(m_ref.shape, NEG, F32)
    l_ref[...] = jnp.zeros(l_ref.shape, F32)
    acc_ref[...] = jnp.zeros(acc_ref.shape, F32)
    q_all = qm_ref[...].reshape(GROUP_HEADS * tq, LANES)

    def attend(k_blk, v_blk, madd, bias):
        width = k_blk.shape[0]
        s = _nt_dot(q_all, k_blk).reshape(GROUP_HEADS, tq, width) + madd[None]
        if bias is not None:
            s = s + bias
        _softmax_step_stacked(s, v_blk, m_ref, l_ref, acc_ref)

    win_blk = jnp.maximum(i - 1, 0)
    win_start = pl.multiple_of(win_blk * LANES, LANES)
    near_keys = key_ref[pl.ds(win_blk, 2)]
    near_keys = jnp.concatenate([near_keys[0], near_keys[1]], axis=1)
    near_madd = jnp.where(near_keys >= _lane_tile(thr, 2), 0.0, NEG)
    attend(k2_ref[pl.ds(win_start, 2 * LANES), :], v2_ref[pl.ds(win_start, 2 * LANES), :],
           near_madd, bias_ref[jnp.minimum(i, 1)])
    for u in range(2):
        key_ref[win_blk + u] = jnp.full((tq, LANES), INT_MIN, I32)

    n_far = (win_blk * LANES + DSA_KC - 1) // DSA_KC

    def far_chunk(c, carry):
        start = pl.multiple_of(c * DSA_KC, DSA_KC)
        madd = jnp.where(load_keys(c) >= _lane_tile(thr, DSA_SUB), 0.0, NEG)
        attend(k2_ref[pl.ds(start, DSA_KC), :], v2_ref[pl.ds(start, DSA_KC), :], madd, None)
        return carry

    lax.fori_loop(0, n_far, far_chunk, 0)
    _store_pairs(o_ref, [acc_ref[h] / l_ref[h] for h in range(GROUP_HEADS)])


def _mixer_dsa(p, side, bias, tri, batch, seq):
    t = p.shape[0]
    tq = DSA_TQ
    nq = seq // tq
    k_top = min(TOPK_MAX, seq // 4)
    cw = PT // LANES
    return pl.pallas_call(
        functools.partial(_dsa_kernel, k_top=k_top),
        grid=(batch, nq),
        in_specs=[
            pl.BlockSpec((tq, PT), lambda b, i: (b * nq + i, T_AQ)),
            pl.BlockSpec((tq, 2 * PT), lambda b, i: (b * nq + i, T_IQ // 2)),
            pl.BlockSpec((tq, LANES), lambda b, i: (b * nq + i, 0)),
            pl.BlockSpec((seq, LANES), lambda b, i: (b, T_K2 * cw)),
            pl.BlockSpec((seq, LANES), lambda b, i: (b, T_MISC * cw)),
            pl.BlockSpec((seq, LANES), lambda b, i: (b, T_MISC * cw + 2)),
            pl.BlockSpec((2, GROUP_HEADS, tq, 2 * LANES), lambda b, i: (0, 0, 0, 0)),
            pl.BlockSpec((DSA_KC, DSA_KC), lambda b, i: (0, 0)),
        ],
        out_specs=pl.BlockSpec((tq, GROUP_WIDTH), lambda b, i: (b * nq + i, 0)),
        out_shape=jax.ShapeDtypeStruct((t, GROUP_WIDTH), F32),
        scratch_shapes=[
            pltpu.VMEM((max(seq // LANES, DSA_SUB), tq, LANES), I32),
            pltpu.VMEM((IDX_HEADS, tq, LANES), F32),
            pltpu.VMEM((IDX_HEADS, tq, LANES), BF16),
            pltpu.VMEM((GROUP_HEADS, tq, LANES), BF16),
            pltpu.VMEM((tq, LANES), I32),
            pltpu.VMEM((GROUP_HEADS, tq, LANES), F32),
            pltpu.VMEM((GROUP_HEADS, tq, LANES), F32),
            pltpu.VMEM((GROUP_HEADS, tq, LANES), F32),
        ],
        compiler_params=_cparams(("arbitrary", "arbitrary")),
        name="mixer_dsa",
    )(p, p, side, p, p, p, bias, tri)


SWA_TQ = 128


def _swa_kernel(sink_ref, q_ref, kp_ref, kc_ref, vp_ref, vc_ref, bias_ref, o_ref):
    i = pl.program_id(1)
    tq = SWA_TQ
    t_l = lax.broadcasted_iota(I32, (tq, LANES), 0)
    s_l = lax.broadcasted_iota(I32, (tq, LANES), 1)
    ok_prev = s_l > t_l + jnp.where(i > 0, 0, LANES)
    ok_cur = s_l <= t_l
    kp, kc, vp, vc = kp_ref[...], kc_ref[...], vp_ref[...], vc_ref[...]
    outs = []
    for h in range(GROUP_HEADS):
        blk = q_ref[:, (h // 2) * LANES:(h // 2 + 1) * LANES]
        qm = jnp.where(_own_lane_mask(blk.shape, h), blk, jnp.zeros_like(blk))
        s_p = jnp.where(ok_prev, _nt_dot(qm, kp) + bias_ref[h, 0], NEG)
        s_c = jnp.where(ok_cur, _nt_dot(qm, kc) + bias_ref[h, 1], NEG)
        sink = sink_ref[h]
        m = jnp.maximum(jnp.max(jnp.maximum(s_p, s_c), axis=1, keepdims=True), sink)
        p_p = jnp.exp(s_p - m)
        p_c = jnp.exp(s_c - m)
        l = jnp.sum(p_p + p_c, axis=1, keepdims=True) + jnp.exp(sink - m)
        outs.append((_dot(p_p.astype(BF16), vp) + _dot(p_c.astype(BF16), vc)) / l)
    _store_pairs(o_ref, outs)


def _mixer_swa(p, sinks, bias, batch, seq):
    t = p.shape[0]
    tq = SWA_TQ
    nq = seq // tq
    cw = PT // LANES
    cur = lambda col: (lambda b, i: (b * nq + i, col))
    prev = lambda col: (lambda b, i: (b * nq + jnp.maximum(i - 1, 0), col))
    return pl.pallas_call(
        _swa_kernel,
        grid=(batch, nq),
        in_specs=[
            pl.BlockSpec(memory_space=pltpu.SMEM),
            pl.BlockSpec((tq, PT), lambda b, i: (b * nq + i, T_BQ)),
            pl.BlockSpec((tq, LANES), prev(T_K2 * cw + 1)),
            pl.BlockSpec((tq, LANES), cur(T_K2 * cw + 1)),
            pl.BlockSpec((tq, LANES), prev(T_MISC * cw + 1)),
            pl.BlockSpec((tq, LANES), cur(T_MISC * cw + 1)),
            pl.BlockSpec((GROUP_HEADS, 2, tq, LANES), lambda b, i: (1, 0, 0, 0)),
        ],
        out_specs=pl.BlockSpec((tq, GROUP_WIDTH), lambda b, i: (b * nq + i, 0)),
        out_shape=jax.ShapeDtypeStruct((t, GROUP_WIDTH), F32),
        compiler_params=_cparams(("arbitrary", "arbitrary")),
        name="mixer_swa",
    )(sinks, p, p, p, p, p, bias)


def _logsig(x):
    return jnp.minimum(x, 0.0) - jnp.log(1.0 + jnp.exp(-jnp.abs(x)))


def _foxcum_kernel(fb_ref, f_ref, tri_ref, o_ref, *, nchunk):
    def body(c, carry):
        start = pl.multiple_of(c * LANES, LANES)
        lf = _logsig(f_ref[pl.ds(start, LANES), :] + fb_ref[...]).T
        p1 = lf.astype(BF16)
        r1 = lf - p1.astype(F32)
        p2 = r1.astype(BF16)
        p3 = (r1 - p2.astype(F32)).astype(BF16)
        tri = tri_ref[...]
        cum = _dot(p1, tri) + _dot(p2, tri) + _dot(p3, tri) + carry
        o_ref[0, c] = cum[:GROUP_HEADS]
        return jnp.broadcast_to(cum[:, LANES - 1:LANES], cum.shape)

    lax.fori_loop(0, nchunk, body, jnp.zeros((LANES, LANES), F32))


def _fox_cum(side, fb_row, tri_incl, batch, seq):
    nchunk = seq // LANES
    return pl.pallas_call(
        functools.partial(_foxcum_kernel, nchunk=nchunk),
        grid=(batch,),
        in_specs=[
            pl.BlockSpec((1, LANES), lambda b: (0, 0)),
            pl.BlockSpec((seq, LANES), lambda b: (b, 0)),
            pl.BlockSpec((LANES, LANES), lambda b: (0, 0)),
        ],
        out_specs=pl.BlockSpec((1, nchunk, GROUP_HEADS, LANES), lambda b: (b, 0, 0, 0)),
        out_shape=jax.ShapeDtypeStruct((batch, nchunk, GROUP_HEADS, LANES), F32),
        compiler_params=_cparams(("arbitrary",)),
        name="fox_cumsum",
    )(fb_row, side, tri_incl)


FOX_TQ = 256
FOX_KC = 256
FOX_SUB = FOX_KC // LANES


def _fox_kernel(q_ref, g_ref, k_ref, v_ref, cum_ref, o_ref, qm_ref, m_ref, l_ref, acc_ref):
    i = pl.program_id(1)
    tq = FOX_TQ
    for h in range(GROUP_HEADS):
        blk = q_ref[:, (h // 2) * LANES:(h // 2 + 1) * LANES]
        qm_ref[h] = jnp.where(_own_lane_mask(blk.shape, h), blk, jnp.zeros_like(blk))
    m_ref[...] = jnp.full(m_ref.shape, NEG, F32)
    l_ref[...] = jnp.zeros(l_ref.shape, F32)
    acc_ref[...] = jnp.zeros(acc_ref.shape, F32)
    f_ref0 = cum_ref[0, i * (tq // LANES)][:, 0:1]

    def chunk(c, masked):
        start = pl.multiple_of(c * FOX_KC, FOX_KC)
        cum = cum_ref[0, pl.ds(c * FOX_SUB, FOX_SUB)]
        cum = jnp.concatenate([cum[u] for u in range(FOX_SUB)], axis=1)
        fbias = f_ref0 - cum
        if masked:
            qpos = i * tq + lax.broadcasted_iota(I32, (tq, FOX_KC), 0)
            kpos = start + lax.broadcasted_iota(I32, (tq, FOX_KC), 1)
            ok = kpos <= qpos
        for h in range(GROUP_HEADS):
            pair = slice((h // 2) * LANES, (h // 2 + 1) * LANES)
            s = _nt_dot(qm_ref[h], k_ref[pl.ds(start, FOX_KC), pair]) + fbias[h:h + 1, :]
            if masked:
                s = jnp.where(ok, s, NEG)
            _softmax_step(s, v_ref[pl.ds(start, FOX_KC), pair], m_ref, l_ref, acc_ref, h)

    def body(c, carry):
        chunk(c, False)
        return carry

    n_full = (i * tq) // FOX_KC
    lax.fori_loop(0, n_full, body, 0)
    for c_off in range(tq // FOX_KC):
        chunk(n_full + c_off, True)
    outs = []
    for h in range(GROUP_HEADS):
        gate = _sigmoid(g_ref[:, (h // 2) * LANES:(h // 2 + 1) * LANES].astype(F32))
        outs.append(acc_ref[h] / l_ref[h] * gate)
    _store_pairs(o_ref, outs)


def _mixer_fox(p, cum, batch, seq):
    t = p.shape[0]
    tq = FOX_TQ
    nq = seq // tq
    return pl.pallas_call(
        _fox_kernel,
        grid=(batch, nq),
        in_specs=[
            pl.BlockSpec((tq, PT), lambda b, i: (b * nq + i, T_CQ)),
            pl.BlockSpec((tq, PT), lambda b, i: (b * nq + i, T_CG)),
            pl.BlockSpec((seq, PT), lambda b, i: (b, T_CK)),
            pl.BlockSpec((seq, PT), lambda b, i: (b, T_CV)),
            pl.BlockSpec((1, seq // LANES, GROUP_HEADS, LANES), lambda b, i: (b, 0, 0, 0)),
        ],
        out_specs=pl.BlockSpec((tq, GROUP_WIDTH), lambda b, i: (b * nq + i, 0)),
        out_shape=jax.ShapeDtypeStruct((t, GROUP_WIDTH), F32),
        scratch_shapes=[
            pltpu.VMEM((GROUP_HEADS, tq, LANES), BF16),
            pltpu.VMEM((GROUP_HEADS, tq, LANES), F32),
            pltpu.VMEM((GROUP_HEADS, tq, LANES), F32),
            pltpu.VMEM((GROUP_HEADS, tq, LANES), F32),
        ],
        compiler_params=_cparams(("arbitrary", "arbitrary")),
        name="mixer_fox",
    )(p, p, p, p, cum)


STK_TQ = 128


def _stick_kernel(q_ref, k_ref, v_ref, tri_ref, o_ref, qm_ref, acc_ref, r_ref):
    i = pl.program_id(1)
    tq = STK_TQ
    nh = GROUP_HEADS
    for h in range(nh):
        blk = q_ref[:, (h // 2) * LANES:(h // 2 + 1) * LANES]
        qm_ref[h] = jnp.where(_own_lane_mask(blk.shape, h), blk, jnp.zeros_like(blk))
    acc_ref[...] = jnp.zeros(acc_ref.shape, F32)
    r_ref[...] = jnp.zeros(r_ref.shape, F32)
    before = (lax.broadcasted_iota(I32, (nh, tq, LANES), 2) < lax.broadcasted_iota(I32, (nh, tq, LANES), 1))
    tri = tri_ref[...]

    def block(j, masked):
        start = pl.multiple_of(j * LANES, LANES)
        zs = []
        for pr in range(PAIRS):
            q_pair = qm_ref[2 * pr:2 * pr + 2].reshape(2 * tq, LANES)
            zs.append(_nt_dot(q_pair, k_ref[pl.ds(start, LANES), pr * LANES:(pr + 1) * LANES]))
        z = jnp.concatenate(zs, axis=0).reshape(nh, tq, LANES)
        lsz = _logsig(z)
        u = lsz - z
        if masked:
            u = jnp.where(before, u, 0.0)
        u2 = u.reshape(nh * tq, LANES)
        u_hi = u2.astype(BF16)
        u_lo = (u2 - u_hi.astype(F32)).astype(BF16)
        nearer = (_dot(u_hi, tri) + _dot(u_lo, tri)).reshape(nh, tq, LANES)
        run = r_ref[...]
        w = jnp.exp(lsz + nearer + run)
        if masked:
            w = jnp.where(before, w, 0.0)
        wb = w.astype(BF16)
        pvs = []
        for pr in range(PAIRS):
            w_pair = wb[2 * pr:2 * pr + 2].reshape(2 * tq, LANES)
            pvs.append(_dot(w_pair, v_ref[pl.ds(start, LANES), pr * LANES:(pr + 1) * LANES]))
        acc_ref[...] += jnp.concatenate(pvs, axis=0).reshape(nh, tq, LANES)
        run = run + jnp.sum(u, axis=2, keepdims=True)
        r_ref[...] = run
        return jnp.max(run)

    rmax = block(i, True)

    def cond(carry):
        j, rmax = carry
        return jnp.logical_and(j >= 0, rmax >= STICK_EXIT)

    def body(carry):
        j, _ = carry
        return j - 1, block(j, False)

    lax.while_loop(cond, body, (i - 1, rmax))
    _store_pairs(o_ref, [acc_ref[h] for h in range(nh)])


def _mixer_stick(p, tri_excl, batch, seq):
    t = p.shape[0]
    tq = STK_TQ
    nq = seq // tq
    return pl.pallas_call(
        _stick_kernel,
        grid=(batch, nq),
        in_specs=[
            pl.BlockSpec((tq, PT), lambda b, i: (b * nq + i, T_DQ)),
            pl.BlockSpec((seq, PT), lambda b, i: (b, T_DK)),
            pl.BlockSpec((seq, PT), lambda b, i: (b, T_DV)),
            pl.BlockSpec((LANES, LANES), lambda b, i: (0, 0)),
        ],
        out_specs=pl.BlockSpec((tq, GROUP_WIDTH), lambda b, i: (b * nq + i, 0)),
        out_shape=jax.ShapeDtypeStruct((t, GROUP_WIDTH), F32),
        scratch_shapes=[pltpu.VMEM((GROUP_HEADS, tq, LANES), BF16),
                        pltpu.VMEM((GROUP_HEADS, tq, LANES), F32),
                        pltpu.VMEM((GROUP_HEADS, tq, LANES), F32)],
        compiler_params=_cparams(("arbitrary", "arbitrary")),
        name="mixer_stick",
    )(p, p, p, tri_excl)


def _outproj_kernel(x_ref, gt_ref, oa_ref, ob_ref, oc_ref, od_ref, gg_ref, w_ref, o_ref):
    acc = None
    for m, ref in enumerate((oa_ref, ob_ref, oc_ref, od_ref)):
        o = ref[...]
        y = o * lax.rsqrt(jnp.mean(o * o, axis=-1, keepdims=True) + EPS) * gg_ref[m:m + 1, :]
        part = _dot(y.astype(BF16), w_ref[m * GROUP_WIDTH:(m + 1) * GROUP_WIDTH, :])
        acc = part if acc is None else acc + part
    o_ref[...] = x_ref[...] + gt_ref[0] * acc


def _outproj(x, gt, outs, gg, w, l, seq):
    t, d = x.shape
    tm = 512
    per_b = seq // tm
    mix = pl.BlockSpec((tm, GROUP_WIDTH), lambda i: (i, 0))
    return pl.pallas_call(
        _outproj_kernel,
        grid=(t // tm,),
        in_specs=[
            pl.BlockSpec((tm, d), lambda i: (i, 0)),
            pl.BlockSpec((1, 1, d), lambda i: (i // per_b, 0, 0)),
            mix, mix, mix, mix,
            pl.BlockSpec((None, N_MIXERS, GROUP_WIDTH), lambda i: (l, 0, 0)),
            pl.BlockSpec((None, N_MIXERS * GROUP_WIDTH, d), lambda i: (l, 0, 0)),
        ],
        out_specs=pl.BlockSpec((tm, d), lambda i: (i, 0)),
        out_shape=jax.ShapeDtypeStruct((t, d), F32),
        compiler_params=_cparams(("arbitrary",)),
        name="outproj",
    )(x, gt, *outs, gg, w)


def _prep_in_weights(w_in, qk_g):
    depth, d, _ = w_in.shape
    sizes = (GROUP_WIDTH, HEAD_DIM, HEAD_DIM, IDX_HEADS * IDX_DIM, IDX_DIM, IDX_HEADS,
             GROUP_WIDTH, HEAD_DIM, HEAD_DIM,
             GROUP_WIDTH, GROUP_WIDTH, GROUP_WIDTH, GROUP_HEADS, GROUP_WIDTH,
             GROUP_WIDTH, GROUP_WIDTH, GROUP_WIDTH)
    pts = np.cumsum(sizes)[:-1].tolist()
    (a_q, a_k, a_v, a_iq, a_ik, a_iw, b_q, b_k, b_v,
     c_q, c_k, c_v, c_f, c_g, d_q, d_k, d_v) = jnp.split(w_in, pts, axis=-1)
    z = lambda n: jnp.zeros((depth, d, n), w_in.dtype)
    qscale = HEAD_DIM ** -0.5
    main = jnp.concatenate([
        a_q, b_q, c_q, c_k,
        a_k, a_k, b_k, b_k, z(2 * LANES),
        c_v, c_g, d_q * qscale, d_k, d_v, a_iq * (IDX_DIM ** -0.5),
        a_v, a_v, b_v, b_v, a_ik, a_ik, z(LANES),
    ], axis=-1).astype(BF16)
    side = jnp.concatenate([c_f, a_iw * (IDX_HEADS ** -0.5), z(LANES - GROUP_HEADS - IDX_HEADS)],
                           axis=-1).astype(BF16)
    rep = lambda g, n: jnp.tile(g, (1, n))
    gain = jnp.concatenate([
        rep(qk_g[:, 0], 8) * qscale, rep(qk_g[:, 2], 8) * qscale, rep(qk_g[:, 4], 8) * qscale,
        rep(qk_g[:, 5], 8), rep(qk_g[:, 1], 2), rep(qk_g[:, 3], 2),
        jnp.zeros((depth, P_COLS - 4 * PT - 2 * LANES), F32),
    ], axis=-1).reshape(depth, 1, P_COLS)
    return main, side, gain


def _np_const(shape_fn):
    return jnp.asarray(shape_fn())


def kernel(x, c, w_ada, b_ada, norm_g, w_in, qk_g, forget_b, sinks, rel_table, group_g, w_out,
           w_ffn_gate, w_ffn_up, w_ffn_down):
    batch, seq, d = x.shape
    depth = w_ada.shape[0]
    t = batch * seq

    mod = _modulation(c, w_ada, b_ada)
    wg = w_ffn_gate.astype(BF16)
    wu = w_ffn_up.astype(BF16)
    wd = w_ffn_down.astype(BF16)
    wo = w_out.astype(BF16)
    w_main, w_side, gain = _prep_in_weights(w_in, qk_g)
    gain_side = jnp.zeros((depth, 1, LANES), F32)

    r = np.arange(PT)
    bd = jnp.asarray((r[:, None] // HEAD_DIM == r[None, :] // HEAD_DIM).astype(np.float32) / HEAD_DIM, BF16)
    bd_side = jnp.zeros((LANES, LANES), BF16)
    r = np.arange(LANES)
    tri_incl = jnp.asarray(r[:, None] <= r[None, :], BF16)
    tri_after = jnp.asarray(r[:, None] > r[None, :], BF16)
    r = np.arange(DSA_KC)
    tri_before = jnp.asarray(r[:, None] < r[None, :], BF16)

    bias = _bias_tiles(rel_table, DSA_TQ)
    bias_dsa = _dsa_window_bias(bias)
    fb_rows = jnp.zeros((depth, 1, LANES), F32).at[:, 0, :GROUP_HEADS].set(forget_b)
    gg = group_g.reshape(depth, N_MIXERS, GROUP_WIDTH)

    xt = x.reshape(t, d)
    for l in range(depth):
        parts = [m.reshape(batch, 1, d) for m in jnp.split(mod[l], 9, axis=-1)]
        sh1, sc1, g1, sh2, sc2, g2, sh3, sc3, g3 = parts
        xt = _ffn(xt, norm_g[l, 0:1], sh1, sc1, g1, wg, wu, wd, l, 0, seq)
        ng2 = norm_g[l, 1:2]
        p = _inproj(xt, ng2, sh2, sc2, w_main, gain, bd, l, seq, N_NORM_TILES, BF16)
        side = _inproj(xt, ng2, sh2, sc2, w_side, gain_side, bd_side, l, seq, 0, F32)
        o_a = _mixer_dsa(p, side, bias_dsa, tri_before, batch, seq)
        o_b = _mixer_swa(p, sinks[l], bias, batch, seq)
        cum = _fox_cum(side, fb_rows[l], tri_incl, batch, seq)
        o_c = _mixer_fox(p, cum, batch, seq)
        o_d = _mixer_stick(p, tri_after, batch, seq)
        xt = _outproj(xt, g2, (o_a, o_b, o_c, o_d), gg, wo, l, seq)
        xt = _ffn(xt, norm_g[l, 2:3], sh3, sc3, g3, wg, wu, wd, l, 1, seq)
    return xt.reshape(batch, seq, d)
```

```python
import functools
import math

import numpy as np
import jax
import jax.numpy as jnp
from jax import lax
from jax.experimental import pallas as pl
from jax.experimental.pallas import tpu as pltpu

F32 = jnp.float32
BF16 = jnp.bfloat16
I32 = jnp.int32

HEAD_DIM = 64
N_MIXERS = 4
GROUP_HEADS = 8
GROUP_WIDTH = GROUP_HEADS * HEAD_DIM
IDX_HEADS = 16
IDX_DIM = 64
TOPK_MAX = 256
WINDOW = 128
REL_BUCKETS = 32
REL_MAX_DIST = 128
EPS = 1e-6

LANES = 128
PAIRS = GROUP_HEADS // 2
NEG = -1e30
INT_MIN = -2 ** 31
STICK_EXIT = -110.0
VMEM_LIMIT = 56 * 1024 * 1024
FFN_VMEM_LIMIT = 58 * 1024 * 1024

PT = 512
T_AQ, T_BQ, T_CQ, T_CK, T_K2 = 0, 1, 2, 3, 4
N_NORM_TILES = 5
T_CV, T_CG, T_DQ, T_DK, T_DV, T_IQ, T_MISC = 5, 6, 7, 8, 9, 10, 12
N_TILES = 13
P_COLS = N_TILES * PT


def _cparams(sem, vmem=VMEM_LIMIT):
    return pltpu.CompilerParams(dimension_semantics=sem, vmem_limit_bytes=vmem)


def _nt_dot(a, b):
    return lax.dot_general(a, b, (((1,), (1,)), ((), ())), preferred_element_type=F32)


def _dot(a, b):
    return jnp.dot(a, b, preferred_element_type=F32)


def _sigmoid(x):
    return 1.0 / (1.0 + jnp.exp(-x))


def _lane_tile(x, n):
    return x if n == 1 else jnp.concatenate([x] * n, axis=1)


def _own_lane_mask(shape, h):
    lane = lax.broadcasted_iota(I32, shape, 1)
    return (lane < HEAD_DIM) if h % 2 == 0 else (lane >= HEAD_DIM)


def _mod_kernel(c_ref, w_ref, b_ref, o_ref):
    c = c_ref[...]
    cond = (c * _sigmoid(c)).astype(BF16)
    o_ref[0] = _dot(cond, w_ref[0].astype(BF16)) + b_ref[0]


def _modulation(c, w_ada, b_ada):
    depth, d, n = w_ada.shape
    b = c.shape[0]
    rows = 8
    tn = 1024
    c_pad = jnp.zeros((rows, d), F32).at[:b].set(c)
    out = pl.pallas_call(
        _mod_kernel,
        grid=(depth, n // tn),
        in_specs=[
            pl.BlockSpec((rows, d), lambda l, j: (0, 0)),
            pl.BlockSpec((1, d, tn), lambda l, j: (l, 0, j)),
            pl.BlockSpec((1, 1, tn), lambda l, j: (l, 0, j)),
        ],
        out_specs=pl.BlockSpec((1, rows, tn), lambda l, j: (l, 0, j)),
        out_shape=jax.ShapeDtypeStruct((depth, rows, n), F32),
        compiler_params=_cparams(("arbitrary", "arbitrary")),
        name="adaln_mod",
    )(c_pad, w_ada, b_ada.reshape(depth, 1, n))
    return out[:, :b]


def _norm_modulate(x, ng, sh, sc):
    ms = jnp.mean(x * x, axis=-1, keepdims=True)
    y = x * lax.rsqrt(ms + EPS) * ng
    return y * (1.0 + sc) + sh


def _ffn_kernel(x_ref, ng_ref, sh_ref, sc_ref, gt_ref, wg_ref, wu_ref, wd_ref, o_ref, h_ref):
    f = pl.program_id(1)

    @pl.when(f == 0)
    def _():
        h = _norm_modulate(x_ref[...], ng_ref[...], sh_ref[0], sc_ref[0])
        h_ref[...] = h.astype(BF16)
        o_ref[...] = jnp.zeros_like(o_ref)

    h = h_ref[...]
    g = _dot(h, wg_ref[...])
    u = _dot(h, wu_ref[...])
    a = (g * _sigmoid(g) * u).astype(BF16)
    o_ref[...] += _dot(a, wd_ref[...])

    @pl.when(f == pl.num_programs(1) - 1)
    def _():
        o_ref[...] = x_ref[...] + 0.5 * gt_ref[0] * o_ref[...]


def _ffn(x, ng, sh, sc, gt, wg, wu, wd, l, s, seq):
    t, d = x.shape
    dff = wg.shape[-1]
    tm, tf = min(1024, seq), 512
    per_b = seq // tm
    vec = pl.BlockSpec((1, 1, d), lambda i, f: (i // per_b, 0, 0))
    return pl.pallas_call(
        _ffn_kernel,
        grid=(t // tm, dff // tf),
        in_specs=[
            pl.BlockSpec((tm, d), lambda i, f: (i, 0), pipeline_mode=pl.Buffered(1)),
            pl.BlockSpec((1, d), lambda i, f: (0, 0)),
            vec, vec, vec,
            pl.BlockSpec((None, None, d, tf), lambda i, f: (l, s, 0, f)),
            pl.BlockSpec((None, None, d, tf), lambda i, f: (l, s, 0, f)),
            pl.BlockSpec((None, None, tf, d), lambda i, f: (l, s, f, 0)),
        ],
        out_specs=pl.BlockSpec((tm, d), lambda i, f: (i, 0)),
        out_shape=jax.ShapeDtypeStruct((t, d), F32),
        scratch_shapes=[pltpu.VMEM((tm, d), BF16)],
        compiler_params=_cparams(("arbitrary", "arbitrary"), FFN_VMEM_LIMIT),
        name="ffn",
    )(x, ng, sh, sc, gt, wg, wu, wd)


def _inproj_kernel(x_ref, ng_ref, sh_ref, sc_ref, w_ref, gain_ref, bd_ref, o_ref, h_ref, *, n_norm):
    j = pl.program_id(1)

    @pl.when(j == 0)
    def _():
        h = _norm_modulate(x_ref[...], ng_ref[...], sh_ref[0], sc_ref[0])
        h_ref[...] = h.astype(BF16)

    y = _dot(h_ref[...], w_ref[...])

    @pl.when(j < n_norm)
    def _():
        sq = y * y
        hi = sq.astype(BF16)
        lo = (sq - hi.astype(F32)).astype(BF16)
        ms = _dot(hi, bd_ref[...]) + _dot(lo, bd_ref[...])
        o_ref[...] = (y * lax.rsqrt(ms + EPS) * gain_ref[...]).astype(o_ref.dtype)

    @pl.when(j >= n_norm)
    def _():
        o_ref[...] = y.astype(o_ref.dtype)


def _inproj(x, ng, sh, sc, w, gain, bd, l, seq, n_norm, out_dtype):
    t, d = x.shape
    n = w.shape[-1]
    tm = min(1024, seq)
    tn = bd.shape[0]
    per_b = seq // tm
    vec = pl.BlockSpec((1, 1, d), lambda i, j: (i // per_b, 0, 0))
    return pl.pallas_call(
        functools.partial(_inproj_kernel, n_norm=n_norm),
        grid=(t // tm, n // tn),
        in_specs=[
            pl.BlockSpec((tm, d), lambda i, j: (i, 0)),
            pl.BlockSpec((1, d), lambda i, j: (0, 0)),
            vec, vec,
            pl.BlockSpec((None, d, tn), lambda i, j: (l, 0, j)),
            pl.BlockSpec((None, 1, tn), lambda i, j: (l, 0, j)),
            pl.BlockSpec((tn, tn), lambda i, j: (0, 0)),
        ],
        out_specs=pl.BlockSpec((tm, tn), lambda i, j: (i, j)),
        out_shape=jax.ShapeDtypeStruct((t, n), out_dtype),
        scratch_shapes=[pltpu.VMEM((tm, d), BF16)],
        compiler_params=_cparams(("arbitrary", "arbitrary")),
        name="inproj",
    )(x, ng, sh, sc, w, gain, bd)


def _rel_bucket_np(dist):
    n = np.maximum(dist, 0)
    max_exact = REL_BUCKETS // 2
    nf = np.maximum(n, 1).astype(np.float32)
    large = max_exact + (np.log(nf / np.float32(max_exact)) / np.float32(math.log(REL_MAX_DIST / max_exact))
                         * np.float32(REL_BUCKETS - max_exact)).astype(np.int32)
    large = np.minimum(large, REL_BUCKETS - 1)
    return np.where(n < max_exact, n, large).astype(np.int32)


def _bias_kernel(tab_ref, bkt_ref, o_ref, *, sub_far):
    h = pl.program_id(0)
    far = tab_ref[REL_BUCKETS - 1, h]
    for which in range(2):
        bkt = bkt_ref[which]
        acc = jnp.zeros(bkt.shape, F32)
        for b in range(REL_BUCKETS):
            acc = jnp.where(bkt == b, tab_ref[b, h], acc)
        o_ref[0, which] = acc - jnp.where(h < sub_far, far, 0.0)


def _bias_tiles(rel_table, tq):
    t_l = np.arange(tq)[:, None]
    s_l = np.arange(LANES)[None, :]
    bkt = np.stack([_rel_bucket_np(t_l + LANES - s_l), _rel_bucket_np(t_l - s_l)]).astype(np.int32)
    nh = rel_table.shape[1]
    return pl.pallas_call(
        functools.partial(_bias_kernel, sub_far=GROUP_HEADS),
        grid=(nh,),
        in_specs=[
            pl.BlockSpec(memory_space=pltpu.SMEM),
            pl.BlockSpec((2, tq, LANES), lambda h: (0, 0, 0)),
        ],
        out_specs=pl.BlockSpec((1, 2, tq, LANES), lambda h: (h, 0, 0, 0)),
        out_shape=jax.ShapeDtypeStruct((nh, 2, tq, LANES), F32),
        compiler_params=_cparams(("arbitrary",)),
        name="rel_bias_tiles",
    )(rel_table, jnp.asarray(bkt))


def _dsa_window_bias(bias):
    prev, cur = bias[:GROUP_HEADS, 0], bias[:GROUP_HEADS, 1]
    return jnp.stack([jnp.concatenate([cur, cur], axis=-1), jnp.concatenate([prev, cur], axis=-1)])


def _softmax_step(s, v, m_ref, l_ref, acc_ref, idx):
    reps = s.shape[1] // LANES
    m_prev = m_ref[idx]
    l_prev = l_ref[idx]
    m_cur = jnp.max(s, axis=1, keepdims=True)
    m_new = jnp.maximum(m_prev, m_cur)
    p = jnp.exp(s - _lane_tile(m_new, reps))
    alpha = jnp.exp(m_prev - m_new)
    l_ref[idx] = alpha * l_prev + jnp.sum(p, axis=1, keepdims=True)
    m_ref[idx] = m_new
    acc_ref[idx] = alpha * acc_ref[idx] + _dot(p.astype(BF16), v)


def _softmax_step_stacked(s, v, m_ref, l_ref, acc_ref):
    nh, rows, width = s.shape
    reps = width // LANES
    m_prev = m_ref[...]
    m_new = jnp.maximum(m_prev, jnp.max(s, axis=2, keepdims=True))
    p = jnp.exp(s - jnp.concatenate([m_new] * reps, axis=2))
    alpha = jnp.exp(m_prev - m_new)
    l_ref[...] = alpha * l_ref[...] + jnp.sum(p, axis=2, keepdims=True)
    m_ref[...] = m_new
    pv = _dot(p.astype(BF16).reshape(nh * rows, width), v).reshape(nh, rows, LANES)
    acc_ref[...] = alpha * acc_ref[...] + pv


def _store_pairs(o_ref, per_head):
    for j in range(PAIRS):
        lane = lax.broadcasted_iota(I32, per_head[0].shape, 1)
        o_ref[:, j * LANES:(j + 1) * LANES] = jnp.where(
            lane < HEAD_DIM, per_head[2 * j], per_head[2 * j + 1]).astype(o_ref.dtype)


DSA_TQ = 128
DSA_KC1 = 256
DSA_KC = 512
DSA_SUB = DSA_KC // LANES


def _dsa_kernel(q_ref, iq_ref, iw_ref, k2_ref, v2_ref, ik2_ref, bias_ref, tri_ref, o_ref,
                key_ref, wb_ref, iqm_ref, qm_ref, t_ref, m_ref, l_ref, acc_ref, *, k_top):
    i = pl.program_id(1)
    tq = DSA_TQ
    row0 = i * tq
    n_valid_cols = row0 + tq

    iw = iw_ref[...]
    for h in range(IDX_HEADS):
        col = GROUP_HEADS + h
        wb_ref[h] = jnp.broadcast_to(iw[:, col:col + 1], (tq, LANES))
        blk = iq_ref[:, (h // 2) * LANES:(h // 2 + 1) * LANES]
        iqm_ref[h] = jnp.where(_own_lane_mask(blk.shape, h), blk, jnp.zeros_like(blk))
    for h in range(GROUP_HEADS):
        blk = q_ref[:, (h // 2) * LANES:(h // 2 + 1) * LANES]
        qm_ref[h] = jnp.where(_own_lane_mask(blk.shape, h), blk, jnp.zeros_like(blk))

    qpos = row0 + lax.broadcasted_iota(I32, (tq, DSA_KC1), 0)
    iq_all = iqm_ref[...].reshape(IDX_HEADS * tq, LANES)
    wb_all = jnp.concatenate([wb_ref[...]] * (DSA_KC1 // LANES), axis=2)
    n1 =(n_valid_cols + DSA_KC1 - 1) // DSA_KC1

    def score_chunk(c, carry):
        start = pl.multiple_of(c * DSA_KC1, DSA_KC1)
        ik = ik2_ref[pl.ds(start, DSA_KC1), :]
        r = _nt_dot(iq_all, ik).reshape(IDX_HEADS, tq, DSA_KC1)
        score = jnp.sum(jnp.maximum(r, 0.0) * wb_all, axis=0)
        bits = lax.bitcast_convert_type(score, I32)
        key = bits ^ ((bits >> 31) & 0x7FFFFFFF)
        key = jnp.where(score == 0.0, 0, key)
        kpos = start + lax.broadcasted_iota(I32, (tq, DSA_KC1), 1)
        key = jnp.where(kpos <= qpos, key, INT_MIN)
        for u in range(DSA_KC1 // LANES):
            key_ref[c * (DSA_KC1 // LANES) + u] = key[:, u * LANES:(u + 1) * LANES]
        return carry

    lax.fori_loop(0, n1, score_chunk, 0)
    nsel = (n_valid_cols + DSA_KC - 1) // DSA_KC
    n1_blocks = n1 * (DSA_KC1 // LANES)

    @pl.when(n1_blocks < nsel * DSA_SUB)
    def _():
        for u in range(DSA_KC1 // LANES):
            key_ref[n1_blocks + u] = jnp.full((tq, LANES), INT_MIN, I32)

    def load_keys(c):
        blk = key_ref[pl.ds(c * DSA_SUB, DSA_SUB)]
        return jnp.concatenate([blk[u] for u in range(DSA_SUB)], axis=1)

    def count_ge(thr, strict):
        def body(c, acc):
            blk = key_ref[pl.ds(c * DSA_SUB, DSA_SUB)]
            for u in range(DSA_SUB):
                hit = (blk[u] > thr) if strict else (blk[u] >= thr)
                acc = jnp.where(hit, acc + 1.0, acc)
            return acc
        acc = lax.fori_loop(0, nsel, body, jnp.zeros((tq, LANES), F32))
        return jnp.broadcast_to(jnp.sum(acc, axis=1, keepdims=True), (tq, LANES))

    t_ref[...] = jnp.full((tq, LANES), INT_MIN, I32)
    kf = float(k_top)

    @pl.when(n_valid_cols > k_top)
    def _():
        def bit_step(it, carry):
            thr, cnt_thr = carry
            cand = thr + jnp.left_shift(jnp.int32(1), 31 - it)
            cnt = count_ge(cand, False)
            take = cnt >= kf
            return jnp.where(take, cand, thr), jnp.where(take, cnt, cnt_thr)

        thr0 = jnp.full((tq, LANES), INT_MIN, I32)
        cnt0 = jnp.full((tq, LANES), float(2 ** 24), F32)
        thr, cnt_thr = lax.fori_loop(0, 32, bit_step, (thr0, cnt0))
        t_ref[...] = thr

        excess = jnp.where(thr > INT_MIN, cnt_thr - kf, 0.0)

        @pl.when(jnp.max(excess) > 0.0)
        def _():
            need = kf - count_ge(thr, True)

            def tie_chunk(c, seen):
                keys = load_keys(c)
                eq = keys == _lane_tile(thr, DSA_SUB)
                eqf = jnp.where(eq, 1.0, 0.0)
                rank = _dot(eqf.astype(BF16), tri_ref[...]) + _lane_tile(seen, DSA_SUB)
                dropped = jnp.where(rank >= _lane_tile(need, DSA_SUB), INT_MIN, keys)
                keys = jnp.where(eq, dropped, keys)
                for u in range(DSA_SUB):
                    key_ref[c * DSA_SUB + u] = keys[:, u * LANES:(u + 1) * LANES]
                return seen + jnp.broadcast_to(jnp.sum(eqf, axis=1, keepdims=True), (tq, LANES))

            lax.fori_loop(0, nsel, tie_chunk, jnp.zeros((tq, LANES), F32))

    thr = jnp.maximum(t_ref[...], INT_MIN + 1)

    m_ref[...] = jnp.full(m_ref.shape, NEG, F32)
    l_ref[...] = jnp.zeros(l_ref.shape, F32)
    acc_ref[...] = jnp.zeros(acc_ref.shape, F32)
    q_all = qm_ref[...].reshape(GROUP_HEADS * tq, LANES)

    def attend(k_blk, v_blk, madd, bias):
        width = k_blk.shape[0]
        s = _nt_dot(q_all, k_blk).reshape(GROUP_HEADS, tq, width) + madd[None]
        if bias is not None:
            s = s + bias
        _softmax_step_stacked(s, v_blk, m_ref, l_ref, acc_ref)

    win_blk = jnp.maximum(i - 1, 0)
    win_start = pl.multiple_of(win_blk * LANES, LANES)
    near_keys = key_ref[pl.ds(win_blk, 2)]
    near_keys = jnp.concatenate([near_keys[0], near_keys[1]], axis=1)
    near_madd = jnp.where(near_keys >= _lane_tile(thr, 2), 0.0, NEG)
    attend(k2_ref[pl.ds(win_start, 2 * LANES), :], v2_ref[pl.ds(win_start, 2 * LANES), :],
           near_madd, bias_ref[jnp.minimum(i, 1)])
    for u in range(2):
        key_ref[win_blk + u] = jnp.full((tq, LANES), INT_MIN, I32)

    n_far = (win_blk * LANES + DSA_KC - 1) // DSA_KC

    def far_chunk(c, carry):
        start = pl.multiple_of(c * DSA_KC, DSA_KC)
        madd = jnp.where(load_keys(c) >= _lane_tile(thr, DSA_SUB), 0.0, NEG)
        attend(k2_ref[pl.ds(start, DSA_KC), :], v2_ref[pl.ds(start, DSA_KC), :], madd, None)
        return carry

    lax.fori_loop(0, n_far, far_chunk, 0)
    _store_pairs(o_ref, [acc_ref[h] / l_ref[h] for h in range(GROUP_HEADS)])


def _mixer_dsa(p, side, bias, tri, batch, seq):
    t = p.shape[0]
    tq = DSA_TQ
    nq = seq // tq
    k_top = min(TOPK_MAX, seq // 4)
    cw = PT // LANES
    return pl.pallas_call(
        functools.partial(_dsa_kernel, k_top=k_top),
        grid=(batch, nq),
        in_specs=[
            pl.BlockSpec((tq, PT), lambda b, i: (b * nq + i, T_AQ)),
            pl.BlockSpec((tq, 2 * PT), lambda b, i: (b * nq + i, T_IQ // 2)),
            pl.BlockSpec((tq, LANES), lambda b, i: (b * nq + i, 0)),
            pl.BlockSpec((seq, LANES), lambda b, i: (b, T_K2 * cw)),
            pl.BlockSpec((seq, LANES), lambda b, i: (b, T_MISC * cw)),
            pl.BlockSpec((seq, LANES), lambda b, i: (b, T_MISC * cw + 2)),
            pl.BlockSpec((2, GROUP_HEADS, tq, 2 * LANES), lambda b, i: (0, 0, 0, 0)),
            pl.BlockSpec((DSA_KC, DSA_KC), lambda b, i: (0, 0)),
        ],
        out_specs=pl.BlockSpec((tq, GROUP_WIDTH), lambda b, i: (b * nq + i, 0)),
        out_shape=jax.ShapeDtypeStruct((t, GROUP_WIDTH), F32),
        scratch_shapes=[
            pltpu.VMEM((max(seq // LANES, DSA_SUB), tq, LANES), I32),
            pltpu.VMEM((IDX_HEADS, tq, LANES), F32),
            pltpu.VMEM((IDX_HEADS, tq, LANES), BF16),
            pltpu.VMEM((GROUP_HEADS, tq, LANES), BF16),
            pltpu.VMEM((tq, LANES), I32),
            pltpu.VMEM((GROUP_HEADS, tq, LANES), F32),
            pltpu.VMEM((GROUP_HEADS, tq, LANES), F32),
            pltpu.VMEM((GROUP_HEADS, tq, LANES), F32),
        ],
        compiler_params=_cparams(("arbitrary", "arbitrary")),
        name="mixer_dsa",
    )(p, p, side, p, p, p, bias, tri)


SWA_TQ = 128


def _swa_kernel(sink_ref, q_ref, kp_ref, kc_ref, vp_ref, vc_ref, bias_ref, o_ref):
    i = pl.program_id(1)
    tq = SWA_TQ
    t_l = lax.broadcasted_iota(I32, (tq, LANES), 0)
    s_l = lax.broadcasted_iota(I32, (tq, LANES), 1)
    ok_prev = s_l > t_l + jnp.where(i > 0, 0, LANES)
    ok_cur = s_l <= t_l
    kp, kc, vp, vc = kp_ref[...], kc_ref[...], vp_ref[...], vc_ref[...]
    outs = []
    for h in range(GROUP_HEADS):
        blk = q_ref[:, (h // 2) * LANES:(h // 2 + 1) * LANES]
        qm = jnp.where(_own_lane_mask(blk.shape, h), blk, jnp.zeros_like(blk))
        s_p = jnp.where(ok_prev, _nt_dot(qm, kp) + bias_ref[h, 0], NEG)
        s_c = jnp.where(ok_cur, _nt_dot(qm, kc) + bias_ref[h, 1], NEG)
        sink = sink_ref[h]
        m = jnp.maximum(jnp.max(jnp.maximum(s_p, s_c), axis=1, keepdims=True), sink)
        p_p = jnp.exp(s_p - m)
        p_c = jnp.exp(s_c - m)
        l = jnp.sum(p_p + p_c, axis=1, keepdims=True) + jnp.exp(sink - m)
        outs.append((_dot(p_p.astype(BF16), vp) + _dot(p_c.astype(BF16), vc)) / l)
    _store_pairs(o_ref, outs)


def _mixer_swa(p, sinks, bias, batch, seq):
    t = p.shape[0]
    tq = SWA_TQ
    nq = seq // tq
    cw = PT // LANES
    cur = lambda col: (lambda b, i: (b * nq + i, col))
    prev = lambda col: (lambda b, i: (b * nq + jnp.maximum(i - 1, 0), col))
    return pl.pallas_call(
        _swa_kernel,
        grid=(batch, nq),
        in_specs=[
            pl.BlockSpec(memory_space=pltpu.SMEM),
            pl.BlockSpec((tq, PT), lambda b, i: (b * nq + i, T_BQ)),
            pl.BlockSpec((tq, LANES), prev(T_K2 * cw + 1)),
            pl.BlockSpec((tq, LANES), cur(T_K2 * cw + 1)),
            pl.BlockSpec((tq, LANES), prev(T_MISC * cw + 1)),
            pl.BlockSpec((tq, LANES), cur(T_MISC * cw + 1)),
            pl.BlockSpec((GROUP_HEADS, 2, tq, LANES), lambda b, i: (1, 0, 0, 0)),
        ],
        out_specs=pl.BlockSpec((tq, GROUP_WIDTH), lambda b, i: (b * nq + i, 0)),
        out_shape=jax.ShapeDtypeStruct((t, GROUP_WIDTH), F32),
        compiler_params=_cparams(("arbitrary", "arbitrary")),
        name="mixer_swa",
    )(sinks, p, p, p, p, p, bias)


def _logsig(x):
    return jnp.minimum(x, 0.0) - jnp.log(1.0 + jnp.exp(-jnp.abs(x)))


def _foxcum_kernel(fb_ref, f_ref, tri_ref, o_ref, *, nchunk):
    def body(c, carry):
        start = pl.multiple_of(c * LANES, LANES)
        lf = _logsig(f_ref[pl.ds(start, LANES), :] + fb_ref[...]).T
        p1 = lf.astype(BF16)
        r1 = lf - p1.astype(F32)
        p2 = r1.astype(BF16)
        p3 = (r1 - p2.astype(F32)).astype(BF16)
        tri = tri_ref[...]
        cum = _dot(p1, tri) + _dot(p2, tri) + _dot(p3, tri) + carry
        o_ref[0, c] = cum[:GROUP_HEADS]
        return jnp.broadcast_to(cum[:, LANES - 1:LANES], cum.shape)

    lax.fori_loop(0, nchunk, body, jnp.zeros((LANES, LANES), F32))


def _fox_cum(side, fb_row, tri_incl, batch, seq):
    nchunk = seq // LANES
    return pl.pallas_call(
        functools.partial(_foxcum_kernel, nchunk=nchunk),
        grid=(batch,),
        in_specs=[
            pl.BlockSpec((1, LANES), lambda b: (0, 0)),
            pl.BlockSpec((seq, LANES), lambda b: (b, 0)),
            pl.BlockSpec((LANES, LANES), lambda b: (0, 0)),
        ],
        out_specs=pl.BlockSpec((1, nchunk, GROUP_HEADS, LANES), lambda b: (b, 0, 0, 0)),
        out_shape=jax.ShapeDtypeStruct((batch, nchunk, GROUP_HEADS, LANES), F32),
        compiler_params=_cparams(("arbitrary",)),
        name="fox_cumsum",
    )(fb_row, side, tri_incl)


FOX_TQ = 256
FOX_KC = 512
FOX_SUB = FOX_KC // LANES


def _fox_kernel(q_ref, g_ref, k_ref, v_ref, cum_ref, o_ref, qm_ref, m_ref, l_ref, acc_ref):
    i = pl.program_id(1)
    tq = FOX_TQ
    for h in range(GROUP_HEADS):
        blk = q_ref[:, (h // 2) * LANES:(h // 2 + 1) * LANES]
        qm_ref[h] = jnp.where(_own_lane_mask(blk.shape, h), blk, jnp.zeros_like(blk))
    m_ref[...] = jnp.full(m_ref.shape, NEG, F32)
    l_ref[...] = jnp.zeros(l_ref.shape, F32)
    acc_ref[...] = jnp.zeros(acc_ref.shape, F32)
    f_ref0 = cum_ref[0, i * (tq // LANES)][:, 0:1]

    def chunk(c, masked):
        start = pl.multiple_of(c * FOX_KC, FOX_KC)
        cum = cum_ref[0, pl.ds(c * FOX_SUB, FOX_SUB)]
        cum = jnp.concatenate([cum[u] for u in range(FOX_SUB)], axis=1)
        fbias = f_ref0 - cum
        per_head = []
        for pr in range(PAIRS):
            pair = slice(pr * LANES, (pr + 1) * LANES)
            q_pair = qm_ref[2 * pr:2 * pr + 2].reshape(2 * tq, LANES)
            s_pair = _nt_dot(q_pair, k_ref[pl.ds(start, FOX_KC), pair])
            for e in range(2):
                h = 2 * pr + e
                per_head.append(s_pair[e * tq:(e + 1) * tq] + fbias[h:h + 1, :])
        s = jnp.stack(per_head)
        if masked:
            shape = (GROUP_HEADS, tq, FOX_KC)
            ok = start + lax.broadcasted_iota(I32, shape, 2) <= i * tq + lax.broadcasted_iota(I32, shape, 1)
            s = jnp.where(ok, s, NEG)
        reps = FOX_KC // LANES
        m_prev = m_ref[...]
        m_new = jnp.maximum(m_prev, jnp.max(s, axis=2, keepdims=True))
        p = jnp.exp(s - jnp.concatenate([m_new] * reps, axis=2))
        alpha = jnp.exp(m_prev - m_new)
        l_ref[...] = alpha * l_ref[...] + jnp.sum(p, axis=2, keepdims=True)
        m_ref[...] = m_new
        p = p.astype(BF16)
        pvs = []
        for pr in range(PAIRS):
            p_pair = p[2 * pr:2 * pr + 2].reshape(2 * tq, FOX_KC)
            pvs.append(_dot(p_pair, v_ref[pl.ds(start, FOX_KC), pr * LANES:(pr + 1) * LANES]))
        acc_ref[...] = alpha * acc_ref[...] + jnp.concatenate(pvs, axis=0).reshape(GROUP_HEADS, tq, LANES)

    def body(c, carry):
        chunk(c, False)
        return carry

    n_full = (i * tq) // FOX_KC
    lax.fori_loop(0, n_full, body, 0)
    chunk(n_full, True)
    outs = []
    for h in range(GROUP_HEADS):
        gate = _sigmoid(g_ref[:, (h // 2) * LANES:(h // 2 + 1) * LANES].astype(F32))
        outs.append(acc_ref[h] / l_ref[h] * gate)
    _store_pairs(o_ref, outs)


def _mixer_fox(p, cum, batch, seq):
    t = p.shape[0]
    tq = FOX_TQ
    nq = seq // tq
    return pl.pallas_call(
        _fox_kernel,
        grid=(batch, nq),
        in_specs=[
            pl.BlockSpec((tq, PT), lambda b, i: (b * nq + i, T_CQ)),
            pl.BlockSpec((tq, PT), lambda b, i: (b * nq + i, T_CG)),
            pl.BlockSpec((seq, PT), lambda b, i: (b, T_CK)),
            pl.BlockSpec((seq, PT), lambda b, i: (b, T_CV)),
            pl.BlockSpec((1, seq // LANES, GROUP_HEADS, LANES), lambda b, i: (b, 0, 0, 0)),
        ],
        out_specs=pl.BlockSpec((tq, GROUP_WIDTH), lambda b, i: (b * nq + i, 0)),
        out_shape=jax.ShapeDtypeStruct((t, GROUP_WIDTH), F32),
        scratch_shapes=[
            pltpu.VMEM((GROUP_HEADS, tq, LANES), BF16),
            pltpu.VMEM((GROUP_HEADS, tq, LANES), F32),
            pltpu.VMEM((GROUP_HEADS, tq, LANES), F32),
            pltpu.VMEM((GROUP_HEADS, tq, LANES), F32),
        ],
        compiler_params=_cparams(("arbitrary", "arbitrary")),
        name="mixer_fox",
    )(p, p, p, p, cum)


STK_TQ = 128


def _stick_kernel(q_ref, k_ref, v_ref, tri_ref, o_ref, qm_ref, acc_ref, r_ref):
    i = pl.program_id(1)
    tq = STK_TQ
    nh = GROUP_HEADS
    for h in range(nh):
        blk = q_ref[:, (h // 2) * LANES:(h // 2 + 1) * LANES]
        qm_ref[h] = jnp.where(_own_lane_mask(blk.shape, h), blk, jnp.zeros_like(blk))
    acc_ref[...] = jnp.zeros(acc_ref.shape, F32)
    r_ref[...] = jnp.zeros(r_ref.shape, F32)
    before = (lax.broadcasted_iota(I32, (nh, tq, LANES), 2) < lax.broadcasted_iota(I32, (nh, tq, LANES), 1))
    tri = tri_ref[...]

    def block(j, masked):
        start = pl.multiple_of(j * LANES, LANES)
        zs = []
        for pr in range(PAIRS):
            q_pair = qm_ref[2 * pr:2 * pr + 2].reshape(2 * tq, LANES)
            zs.append(_nt_dot(q_pair, k_ref[pl.ds(start, LANES), pr * LANES:(pr + 1) * LANES]))
        z = jnp.concatenate(zs, axis=0).reshape(nh, tq, LANES)
        lsz = _logsig(z)
        u = lsz - z
        if masked:
            u = jnp.where(before, u, 0.0)
        u2 = u.reshape(nh * tq, LANES)
        u_hi = u2.astype(BF16)
        u_lo = (u2 - u_hi.astype(F32)).astype(BF16)
        nearer = (_dot(u_hi, tri) + _dot(u_lo, tri)).reshape(nh, tq, LANES)
        run = r_ref[...]
        w = jnp.exp(lsz + nearer + run)
        if masked:
            w = jnp.where(before, w, 0.0)
        wb = w.astype(BF16)
        pvs = []
        for pr in range(PAIRS):
            w_pair = wb[2 * pr:2 * pr + 2].reshape(2 * tq, LANES)
            pvs.append(_dot(w_pair, v_ref[pl.ds(start, LANES), pr * LANES:(pr + 1) * LANES]))
        acc_ref[...] += jnp.concatenate(pvs, axis=0).reshape(nh, tq, LANES)
        run = run + jnp.sum(u, axis=2, keepdims=True)
        r_ref[...] = run
        return jnp.max(run)

    rmax = block(i, True)

    def cond(carry):
        j, rmax = carry
        return jnp.logical_and(j >= 0, rmax >= STICK_EXIT)

    def body(carry):
        j, _ = carry
        return j - 1, block(j, False)

    lax.while_loop(cond, body, (i - 1, rmax))
    _store_pairs(o_ref, [acc_ref[h] for h in range(nh)])


def _mixer_stick(p, tri_excl, batch, seq):
    t = p.shape[0]
    tq = STK_TQ
    nq = seq // tq
    return pl.pallas_call(
        _stick_kernel,
        grid=(batch, nq),
        in_specs=[
            pl.BlockSpec((tq, PT), lambda b, i: (b * nq + i, T_DQ)),
            pl.BlockSpec((seq, PT), lambda b, i: (b, T_DK)),
            pl.BlockSpec((seq, PT), lambda b, i: (b, T_DV)),
            pl.BlockSpec((LANES, LANES), lambda b, i: (0, 0)),
        ],
        out_specs=pl.BlockSpec((tq, GROUP_WIDTH), lambda b, i: (b * nq + i, 0)),
        out_shape=jax.ShapeDtypeStruct((t, GROUP_WIDTH), F32),
        scratch_shapes=[pltpu.VMEM((GROUP_HEADS, tq, LANES), BF16),
                        pltpu.VMEM((GROUP_HEADS, tq, LANES), F32),
                        pltpu.VMEM((GROUP_HEADS, tq, LANES), F32)],
        compiler_params=_cparams(("arbitrary", "arbitrary")),
        name="mixer_stick",
    )(p, p, p, tri_excl)


def _outproj_kernel(x_ref, gt_ref, oa_ref, ob_ref, oc_ref, od_ref, gg_ref, w_ref, o_ref):
    acc = None
    for m, ref in enumerate((oa_ref, ob_ref, oc_ref, od_ref)):
        o = ref[...]
        y = o * lax.rsqrt(jnp.mean(o * o, axis=-1, keepdims=True) + EPS) * gg_ref[m:m + 1, :]
        part = _dot(y.astype(BF16), w_ref[m * GROUP_WIDTH:(m + 1) * GROUP_WIDTH, :])
        acc = part if acc is None else acc + part
    o_ref[...] = x_ref[...] + gt_ref[0] * acc


def _outproj(x, gt, outs, gg, w, l, seq):
    t, d = x.shape
    tm = 512
    per_b = seq // tm
    mix = pl.BlockSpec((tm, GROUP_WIDTH), lambda i: (i, 0))
    return pl.pallas_call(
        _outproj_kernel,
        grid=(t // tm,),
        in_specs=[
            pl.BlockSpec((tm, d), lambda i: (i, 0)),
            pl.BlockSpec((1, 1, d), lambda i: (i // per_b, 0, 0)),
            mix, mix, mix, mix,
            pl.BlockSpec((None, N_MIXERS, GROUP_WIDTH), lambda i: (l, 0, 0)),
            pl.BlockSpec((None, N_MIXERS * GROUP_WIDTH, d), lambda i: (l, 0, 0)),
        ],
        out_specs=pl.BlockSpec((tm, d), lambda i: (i, 0)),
        out_shape=jax.ShapeDtypeStruct((t, d), F32),
        compiler_params=_cparams(("arbitrary",)),
        name="outproj",
    )(x, gt, *outs, gg, w)


def _prep_in_weights(w_in, qk_g):
    depth, d, _ = w_in.shape
    sizes = (GROUP_WIDTH, HEAD_DIM, HEAD_DIM, IDX_HEADS * IDX_DIM, IDX_DIM, IDX_HEADS,
             GROUP_WIDTH, HEAD_DIM, HEAD_DIM,
             GROUP_WIDTH, GROUP_WIDTH, GROUP_WIDTH, GROUP_HEADS, GROUP_WIDTH,
             GROUP_WIDTH, GROUP_WIDTH, GROUP_WIDTH)
    pts = np.cumsum(sizes)[:-1].tolist()
    (a_q, a_k, a_v, a_iq, a_ik, a_iw, b_q, b_k, b_v,
     c_q, c_k, c_v, c_f, c_g, d_q, d_k, d_v) = jnp.split(w_in, pts, axis=-1)
    z = lambda n: jnp.zeros((depth, d, n), w_in.dtype)
    qscale = HEAD_DIM ** -0.5
    main = jnp.concatenate([
        a_q, b_q, c_q, c_k,
        a_k, a_k, b_k, b_k, z(2 * LANES),
        c_v, c_g, d_q * qscale, d_k, d_v, a_iq * (IDX_DIM ** -0.5),
        a_v, a_v, b_v, b_v, a_ik, a_ik, z(LANES),
    ], axis=-1).astype(BF16)
    side = jnp.concatenate([c_f, a_iw * (IDX_HEADS ** -0.5), z(LANES - GROUP_HEADS - IDX_HEADS)],
                           axis=-1).astype(BF16)
    rep = lambda g, n: jnp.tile(g, (1, n))
    gain = jnp.concatenate([
        rep(qk_g[:, 0], 8) * qscale, rep(qk_g[:, 2], 8) * qscale, rep(qk_g[:, 4], 8) * qscale,
        rep(qk_g[:, 5], 8), rep(qk_g[:, 1], 2), rep(qk_g[:, 3], 2),
        jnp.zeros((depth, P_COLS - 4 * PT - 2 * LANES), F32),
    ], axis=-1).reshape(depth, 1, P_COLS)
    return main, side, gain


def _np_const(shape_fn):
    return jnp.asarray(shape_fn())


def kernel(x, c, w_ada, b_ada, norm_g, w_in, qk_g, forget_b, sinks, rel_table, group_g, w_out,
           w_ffn_gate, w_ffn_up, w_ffn_down):
    batch, seq, d = x.shape
    depth = w_ada.shape[0]
    t = batch * seq

    mod = _modulation(c, w_ada, b_ada)
    wg = w_ffn_gate.astype(BF16)
    wu = w_ffn_up.astype(BF16)
    wd = w_ffn_down.astype(BF16)
    wo = w_out.astype(BF16)
    w_main, w_side, gain = _prep_in_weights(w_in, qk_g)
    gain_side = jnp.zeros((depth, 1, LANES), F32)

    r = np.arange(PT)
    bd = jnp.asarray((r[:, None] // HEAD_DIM == r[None, :] // HEAD_DIM).astype(np.float32) / HEAD_DIM, BF16)
    bd_side = jnp.zeros((LANES, LANES), BF16)
    r = np.arange(LANES)
    tri_incl = jnp.asarray(r[:, None] <= r[None, :], BF16)
    tri_after = jnp.asarray(r[:, None] > r[None, :], BF16)
    r = np.arange(DSA_KC)
    tri_before = jnp.asarray(r[:, None] < r[None, :], BF16)

    bias = _bias_tiles(rel_table, DSA_TQ)
    bias_dsa = _dsa_window_bias(bias)
    fb_rows = jnp.zeros((depth, 1, LANES), F32).at[:, 0, :GROUP_HEADS].set(forget_b)
    gg = group_g.reshape(depth, N_MIXERS, GROUP_WIDTH)

    xt = x.reshape(t, d)
    for l in range(depth):
        parts = [m.reshape(batch, 1, d) for m in jnp.split(mod[l], 9, axis=-1)]
        sh1, sc1, g1, sh2, sc2, g2, sh3, sc3, g3 = parts
        xt = _ffn(xt, norm_g[l, 0:1], sh1, sc1, g1, wg, wu, wd, l, 0, seq)
        ng2 = norm_g[l, 1:2]
        p = _inproj(xt, ng2, sh2, sc2, w_main, gain, bd, l, seq, N_NORM_TILES, BF16)
        side = _inproj(xt, ng2, sh2, sc2, w_side, gain_side, bd_side, l, seq, 0, F32)
        o_a = _mixer_dsa(p, side, bias_dsa, tri_before, batch, seq)
        o_b = _mixer_swa(p, sinks[l], bias, batch, seq)
        cum = _fox_cum(side, fb_rows[l], tri_incl, batch, seq)
        o_c = _mixer_fox(p, cum, batch, seq)
        o_d = _mixer_stick(p, tri_after, batch, seq)
        xt = _outproj(xt, g2, (o_a, o_b, o_c, o_d), gg, wo, l, seq)
        xt = _ffn(xt, norm_g[l, 2:3], sh3, sc3, g3, wg, wu, wd, l, 1, seq)
    return xt.reshape(batch, seq, d)
```

```python
import functools
import math

import numpy as np
import jax
import jax.numpy as jnp
from jax import lax
from jax.experimental import pallas as pl
from jax.experimental.pallas import tpu as pltpu

F32 = jnp.float32
BF16 = jnp.bfloat16
I32 = jnp.int32

HEAD_DIM = 64
N_MIXERS = 4
GROUP_HEADS = 8
GROUP_WIDTH = GROUP_HEADS * HEAD_DIM
IDX_HEADS = 16
IDX_DIM = 64
TOPK_MAX = 256
WINDOW = 128
REL_BUCKETS = 32
REL_MAX_DIST = 128
EPS = 1e-6

LANES = 128
PAIRS = GROUP_HEADS // 2
NEG = -1e30
INT_MIN = -2 ** 31
STICK_EXIT = -110.0
VMEM_LIMIT = 56 * 1024 * 1024

PT = 512
T_AQ, T_BQ, T_CQ, T_CK, T_K2 = 0, 1, 2, 3, 4
N_NORM_TILES = 5
T_CV, T_CG, T_DQ, T_DK, T_DV, T_IQ, T_MISC = 5, 6, 7, 8, 9, 10, 12
N_TILES = 13
P_COLS = N_TILES * PT


def _cparams(sem, vmem=VMEM_LIMIT):
    return pltpu.CompilerParams(dimension_semantics=sem, vmem_limit_bytes=vmem)


def _nt_dot(a, b):
    return lax.dot_general(a, b, (((1,), (1,)), ((), ())), preferred_element_type=F32)


def _dot(a, b):
    return jnp.dot(a, b, preferred_element_type=F32)


def _sigmoid(x):
    return 1.0 / (1.0 + jnp.exp(-x))


def _lane_tile(x, n):
    return x if n == 1 else jnp.concatenate([x] * n, axis=1)


def _own_lane_mask(shape, h):
    lane = lax.broadcasted_iota(I32, shape, 1)
    return (lane < HEAD_DIM) if h % 2 == 0 else (lane >= HEAD_DIM)


def _mod_kernel(c_ref, w_ref, b_ref, o_ref):
    c = c_ref[...]
    cond = (c * _sigmoid(c)).astype(BF16)
    o_ref[0] = _dot(cond, w_ref[0].astype(BF16)) + b_ref[0]


def _modulation(c, w_ada, b_ada):
    depth, d, n = w_ada.shape
    b = c.shape[0]
    rows = 8
    tn = 1024
    c_pad = jnp.zeros((rows, d), F32).at[:b].set(c)
    out = pl.pallas_call(
        _mod_kernel,
        grid=(depth, n // tn),
        in_specs=[
            pl.BlockSpec((rows, d), lambda l, j: (0, 0)),
            pl.BlockSpec((1, d, tn), lambda l, j: (l, 0, j)),
            pl.BlockSpec((1, 1, tn), lambda l, j: (l, 0, j)),
        ],
        out_specs=pl.BlockSpec((1, rows, tn), lambda l, j: (l, 0, j)),
        out_shape=jax.ShapeDtypeStruct((depth, rows, n), F32),
        compiler_params=_cparams(("arbitrary", "arbitrary")),
        name="adaln_mod",
    )(c_pad, w_ada, b_ada.reshape(depth, 1, n))
    return out[:, :b]


def _norm_modulate(x, ng, sh, sc):
    ms = jnp.mean(x * x, axis=-1, keepdims=True)
    y = x * lax.rsqrt(ms + EPS) * ng
    return y * (1.0 + sc) + sh


def _ffn_kernel(x_ref, ng_ref, sh_ref, sc_ref, gt_ref, wg_ref, wu_ref, wd_ref, o_ref, h_ref, a_ref):
    f = pl.program_id(1)
    last = pl.num_programs(1) - 1

    def activation():
        h = h_ref[...]
        g = _dot(h, wg_ref[...])
        u = _dot(h, wu_ref[...])
        return (g * _sigmoid(g) * u).astype(BF16)

    @pl.when(f == 0)
    def _():
        h = _norm_modulate(x_ref[...], ng_ref[...], sh_ref[0], sc_ref[0])
        h_ref[...] = h.astype(BF16)
        o_ref[...] = jnp.zeros_like(o_ref)
        a_ref[...] = activation()

    @pl.when(jnp.logical_and(f > 0, f < last))
    def _():
        o_ref[...] += _dot(a_ref[...], wd_ref[...])
        a_ref[...] = activation()

    @pl.when(f == last)
    def _():
        acc = o_ref[...] + _dot(a_ref[...], wd_ref[...])
        o_ref[...] = x_ref[...] + 0.5 * gt_ref[0] * acc


def _ffn(x, ng, sh, sc, gt, wg, wu, wd, l, s, seq):
    t, d = x.shape
    dff = wg.shape[-1]
    tm, tf = 512, 512
    nf = dff // tf
    per_b = seq // tm
    vec = pl.BlockSpec((1, 1, d), lambda i, f: (i // per_b, 0, 0))
    up = lambda i, f: (l, s, 0, jnp.minimum(f, nf - 1))
    return pl.pallas_call(
        _ffn_kernel,
        grid=(t // tm, nf + 1),
        in_specs=[
            pl.BlockSpec((tm, d), lambda i, f: (i, 0)),
            pl.BlockSpec((1, d), lambda i, f: (0, 0)),
            vec, vec, vec,
            pl.BlockSpec((None, None, d, tf), up),
            pl.BlockSpec((None, None, d, tf), up),
            pl.BlockSpec((None, None, tf, d), lambda i, f: (l, s, jnp.maximum(f - 1, 0), 0)),
        ],
        out_specs=pl.BlockSpec((tm, d), lambda i, f: (i, 0)),
        out_shape=jax.ShapeDtypeStruct((t, d), F32),
        scratch_shapes=[pltpu.VMEM((tm, d), BF16), pltpu.VMEM((tm, tf), BF16)],
        compiler_params=_cparams(("arbitrary", "arbitrary")),
        name="ffn",
    )(x, ng, sh, sc, gt, wg, wu, wd)


def _inproj_kernel(x_ref, ng_ref, sh_ref, sc_ref, w_ref, gain_ref, bd_ref, o_ref, h_ref, *, n_norm):
    j = pl.program_id(1)

    @pl.when(j == 0)
    def _():
        h = _norm_modulate(x_ref[...], ng_ref[...], sh_ref[0], sc_ref[0])
        h_ref[...] = h.astype(BF16)

    y = _dot(h_ref[...], w_ref[...])

    @pl.when(j < n_norm)
    def _():
        sq = y * y
        hi = sq.astype(BF16)
        lo = (sq - hi.astype(F32)).astype(BF16)
        ms = _dot(hi, bd_ref[...]) + _dot(lo, bd_ref[...])
        o_ref[...] = (y * lax.rsqrt(ms + EPS) * gain_ref[...]).astype(o_ref.dtype)

    @pl.when(j >= n_norm)
    def _():
        o_ref[...] = y.astype(o_ref.dtype)


def _inproj(x, ng, sh, sc, w, gain, bd, l, seq, n_norm, out_dtype):
    t, d = x.shape
    n = w.shape[-1]
    tm = min(1024, seq)
    tn = bd.shape[0]
    per_b = seq // tm
    vec = pl.BlockSpec((1, 1, d), lambda i, j: (i // per_b, 0, 0))
    return pl.pallas_call(
        functools.partial(_inproj_kernel, n_norm=n_norm),
        grid=(t // tm, n // tn),
        in_specs=[
            pl.BlockSpec((tm, d), lambda i, j: (i, 0)),
            pl.BlockSpec((1, d), lambda i, j: (0, 0)),
            vec, vec,
            pl.BlockSpec((None, d, tn), lambda i, j: (l, 0, j)),
            pl.BlockSpec((None, 1, tn), lambda i, j: (l, 0, j)),
            pl.BlockSpec((tn, tn), lambda i, j: (0, 0)),
        ],
        out_specs=pl.BlockSpec((tm, tn), lambda i, j: (i, j)),
        out_shape=jax.ShapeDtypeStruct((t, n), out_dtype),
        scratch_shapes=[pltpu.VMEM((tm, d), BF16)],
        compiler_params=_cparams(("arbitrary", "arbitrary")),
        name="inproj",
    )(x, ng, sh, sc, w, gain, bd)


def _rel_bucket_np(dist):
    n = np.maximum(dist, 0)
    max_exact = REL_BUCKETS // 2
    nf = np.maximum(n, 1).astype(np.float32)
    large = max_exact + (np.log(nf / np.float32(max_exact)) / np.float32(math.log(REL_MAX_DIST / max_exact))
                         * np.float32(REL_BUCKETS - max_exact)).astype(np.int32)
    large = np.minimum(large, REL_BUCKETS - 1)
    return np.where(n < max_exact, n, large).astype(np.int32)


def _bias_kernel(tab_ref, bkt_ref, o_ref, *, sub_far):
    h = pl.program_id(0)
    far = tab_ref[REL_BUCKETS - 1, h]
    for which in range(2):
        bkt = bkt_ref[which]
        acc = jnp.zeros(bkt.shape, F32)
        for b in range(REL_BUCKETS):
            acc = jnp.where(bkt == b, tab_ref[b, h], acc)
        o_ref[0, which] = acc - jnp.where(h < sub_far, far, 0.0)


def _bias_tiles(rel_table, tq):
    t_l = np.arange(tq)[:, None]
    s_l = np.arange(LANES)[None, :]
    bkt = np.stack([_rel_bucket_np(t_l + LANES - s_l), _rel_bucket_np(t_l - s_l)]).astype(np.int32)
    nh = rel_table.shape[1]
    return pl.pallas_call(
        functools.partial(_bias_kernel, sub_far=GROUP_HEADS),
        grid=(nh,),
        in_specs=[
            pl.BlockSpec(memory_space=pltpu.SMEM),
            pl.BlockSpec((2, tq, LANES), lambda h: (0, 0, 0)),
        ],
        out_specs=pl.BlockSpec((1, 2, tq, LANES), lambda h: (h, 0, 0, 0)),
        out_shape=jax.ShapeDtypeStruct((nh, 2, tq, LANES), F32),
        compiler_params=_cparams(("arbitrary",)),
        name="rel_bias_tiles",
    )(rel_table, jnp.asarray(bkt))


def _dsa_window_bias(bias):
    prev, cur = bias[:GROUP_HEADS, 0], bias[:GROUP_HEADS, 1]
    return jnp.stack([jnp.concatenate([cur, cur], axis=-1), jnp.concatenate([prev, cur], axis=-1)])


def _softmax_step(s, v, m_ref, l_ref, acc_ref, idx):
    reps = s.shape[1] // LANES
    m_prev = m_ref[idx]
    l_prev = l_ref[idx]
    m_cur = jnp.max(s, axis=1, keepdims=True)
    m_new = jnp.maximum(m_prev, m_cur)
    p = jnp.exp(s - _lane_tile(m_new, reps))
    alpha = jnp.exp(m_prev - m_new)
    l_ref[idx] = alpha * l_prev + jnp.sum(p, axis=1, keepdims=True)
    m_ref[idx] = m_new
    acc_ref[idx] = alpha * acc_ref[idx] + _dot(p.astype(BF16), v)


def _softmax_step_stacked(s, v, m_ref, l_ref, acc_ref):
    nh, rows, width = s.shape
    reps = width // LANES
    m_prev = m_ref[...]
    m_new = jnp.maximum(m_prev, jnp.max(s, axis=2, keepdims=True))
    p = jnp.exp(s - jnp.concatenate([m_new] * reps, axis=2))
    alpha = jnp.exp(m_prev - m_new)
    l_ref[...] = alpha * l_ref[...] + jnp.sum(p, axis=2, keepdims=True)
    m_ref[...] = m_new
    pv = _dot(p.astype(BF16).reshape(nh * rows, width), v).reshape(nh, rows, LANES)
    acc_ref[...] = alpha * acc_ref[...] + pv


def _store_pairs(o_ref, per_head):
    for j in range(PAIRS):
        lane = lax.broadcasted_iota(I32, per_head[0].shape, 1)
        o_ref[:, j * LANES:(j + 1) * LANES] = jnp.where(
            lane < HEAD_DIM, per_head[2 * j], per_head[2 * j + 1]).astype(o_ref.dtype)


DSA_TQ = 128
DSA_KC1 = 512
DSA_KC = 512
DSA_SUB = DSA_KC // LANES


def _dsa_kernel(q_ref, iq_ref, iw_ref, k2_ref, v2_ref, ik2_ref, bias_ref, tri_ref, o_ref,
                key_ref, wb_ref, iqm_ref, qm_ref, t_ref, m_ref, l_ref, acc_ref, *, k_top):
    i = pl.program_id(1)
    tq = DSA_TQ
    row0 = i * tq
    n_valid_cols = row0 + tq

    iw = iw_ref[...]
    for h in range(IDX_HEADS):
        col = GROUP_HEADS + h
        wb_ref[h] = jnp.broadcast_to(iw[:, col:col + 1], (tq, LANES))
        blk = iq_ref[:, (h // 2) * LANES:(h // 2 + 1) * LANES]
        iqm_ref[h] = jnp.where(_own_lane_mask(blk.shape, h), blk, jnp.zeros_like(blk))
    for h in range(GROUP_HEADS):
        blk = q_ref[:, (h // 2) * LANES:(h // 2 + 1) * LANES]
        qm_ref[h] = jnp.where(_own_lane_mask(blk.shape, h), blk, jnp.zeros_like(blk))

    qpos = row0 + lax.broadcasted_iota(I32, (tq, DSA_KC1), 0)
    iq_all = iqm_ref[...].reshape(IDX_HEADS * tq, LANES)
    wb_all = jnp.concatenate([wb_ref[...]] * (DSA_KC1 // LANES), axis=2)
    n1 =(n_valid_cols + DSA_KC1 - 1) // DSA_KC1

    def score_chunk(c, carry):
        start = pl.multiple_of(c * DSA_KC1, DSA_KC1)
        ik = ik2_ref[pl.ds(start, DSA_KC1), :]
        r = _nt_dot(iq_all, ik).reshape(IDX_HEADS, tq, DSA_KC1)
        score = jnp.sum(jnp.maximum(r, 0.0) * wb_all, axis=0)
        bits = lax.bitcast_convert_type(score, I32)
        key = bits ^ ((bits >> 31) & 0x7FFFFFFF)
        key = jnp.where(score == 0.0, 0, key)
        kpos = start + lax.broadcasted_iota(I32, (tq, DSA_KC1), 1)
        key = jnp.where(kpos <= qpos, key, INT_MIN)
        for u in range(DSA_KC1 // LANES):
            key_ref[c * (DSA_KC1 // LANES) + u] = key[:, u * LANES:(u + 1) * LANES]
        return carry

    lax.fori_loop(0, n1, score_chunk, 0)
    nsel = (n_valid_cols + DSA_KC - 1) // DSA_KC
    n1_blocks = n1 * (DSA_KC1 // LANES)

    @pl.when(n1_blocks < nsel * DSA_SUB)
    def _():
        for u in range(DSA_KC1 // LANES):
            key_ref[n1_blocks + u] = jnp.full((tq, LANES), INT_MIN, I32)

    def load_keys(c):
        blk = key_ref[pl.ds(c * DSA_SUB, DSA_SUB)]
        return jnp.concatenate([blk[u] for u in range(DSA_SUB)], axis=1)

    def count_ge(thr, strict):
        def body(c, acc):
            blk = key_ref[pl.ds(c * DSA_SUB, DSA_SUB)]
            for u in range(DSA_SUB):
                hit = (blk[u] > thr) if strict else (blk[u] >= thr)
                acc = jnp.where(hit, acc + 1.0, acc)
            return acc
        acc = lax.fori_loop(0, nsel, body, jnp.zeros((tq, LANES), F32))
        return jnp.broadcast_to(jnp.sum(acc, axis=1, keepdims=True), (tq, LANES))

    t_ref[...] = jnp.full((tq, LANES), INT_MIN, I32)
    kf = float(k_top)

    @pl.when(n_valid_cols > k_top)
    def _():
        def bit_step(it, carry):
            thr, cnt_thr = carry
            cand = thr + jnp.left_shift(jnp.int32(1), 31 - it)
            cnt = count_ge(cand, False)
            take = cnt >= kf
            return jnp.where(take, cand, thr), jnp.where(take, cnt, cnt_thr)

        thr0 = jnp.full((tq, LANES), INT_MIN, I32)
        cnt0 = jnp.full((tq, LANES), float(2 ** 24), F32)
        thr, cnt_thr = lax.fori_loop(0, 32, bit_step, (thr0, cnt0))
        t_ref[...] = thr

        excess = jnp.where(thr > INT_MIN, cnt_thr - kf, 0.0)

        @pl.when(jnp.max(excess) > 0.0)
        def _():
            need = kf - count_ge(thr, True)

            def tie_chunk(c, seen):
                keys = load_keys(c)
                eq = keys == _lane_tile(thr, DSA_SUB)
                eqf = jnp.where(eq, 1.0, 0.0)
                rank = _dot(eqf.astype(BF16), tri_ref[...]) + _lane_tile(seen, DSA_SUB)
                dropped = jnp.where(rank >= _lane_tile(need, DSA_SUB), INT_MIN, keys)
                keys = jnp.where(eq, dropped, keys)
                for u in range(DSA_SUB):
                    key_ref[c * DSA_SUB + u] = keys[:, u * LANES:(u + 1) * LANES]
                return seen + jnp.broadcast_to(jnp.sum(eqf, axis=1, keepdims=True), (tq, LANES))

            lax.fori_loop(0, nsel, tie_chunk, jnp.zeros((tq, LANES), F32))

    thr = jnp.maximum(t_ref[...], INT_MIN + 1)

    m_ref[...] = jnp.full(m_ref.shape, NEG, F32)
    l_ref[...] = jnp.zeros(l_ref.shape, F32)
    acc_ref[...] = jnp.zeros(acc_ref.shape, F32)
    q_all = qm_ref[...].reshape(GROUP_HEADS * tq, LANES)

    def attend(k_blk, v_blk, madd, bias):
        width = k_blk.shape[0]
        s = _nt_dot(q_all, k_blk).reshape(GROUP_HEADS, tq, width) + madd[None]
        if bias is not None:
            s = s + bias
        _softmax_step_stacked(s, v_blk, m_ref, l_ref, acc_ref)

    win_blk = jnp.maximum(i - 1, 0)
    win_start = pl.multiple_of(win_blk * LANES, LANES)
    near_keys = key_ref[pl.ds(win_blk, 2)]
    near_keys = jnp.concatenate([near_keys[0], near_keys[1]], axis=1)
    near_madd = jnp.where(near_keys >= _lane_tile(thr, 2), 0.0, NEG)
    attend(k2_ref[pl.ds(win_start, 2 * LANES), :], v2_ref[pl.ds(win_start, 2 * LANES), :],
           near_madd, bias_ref[jnp.minimum(i, 1)])
    for u in range(2):
        key_ref[win_blk + u] = jnp.full((tq, LANES), INT_MIN, I32)

    n_far = (win_blk * LANES + DSA_KC - 1) // DSA_KC

    def far_chunk(c, carry):
        start = pl.multiple_of(c * DSA_KC, DSA_KC)
        madd = jnp.where(load_keys(c) >= _lane_tile(thr, DSA_SUB), 0.0, NEG)
        attend(k2_ref[pl.ds(start, DSA_KC), :], v2_ref[pl.ds(start, DSA_KC), :], madd, None)
        return carry

    lax.fori_loop(0, n_far, far_chunk, 0)
    _store_pairs(o_ref, [acc_ref[h] / l_ref[h] for h in range(GROUP_HEADS)])


def _mixer_dsa(p, side, bias, tri, batch, seq):
    t = p.shape[0]
    tq = DSA_TQ
    nq = seq // tq
    k_top = min(TOPK_MAX, seq // 4)
    cw = PT // LANES
    return pl.pallas_call(
        functools.partial(_dsa_kernel, k_top=k_top),
        grid=(batch, nq),
        in_specs=[
            pl.BlockSpec((tq, PT), lambda b, i: (b * nq + i, T_AQ)),
            pl.BlockSpec((tq, 2 * PT), lambda b, i: (b * nq + i, T_IQ // 2)),
            pl.BlockSpec((tq, LANES), lambda b, i: (b * nq + i, 0)),
            pl.BlockSpec((seq, LANES), lambda b, i: (b, T_K2 * cw)),
            pl.BlockSpec((seq, LANES), lambda b, i: (b, T_MISC * cw)),
            pl.BlockSpec((seq, LANES), lambda b, i: (b, T_MISC * cw + 2)),
            pl.BlockSpec((2, GROUP_HEADS, tq, 2 * LANES), lambda b, i: (0, 0, 0, 0)),
            pl.BlockSpec((DSA_KC, DSA_KC), lambda b, i: (0, 0)),
        ],
        out_specs=pl.BlockSpec((tq, GROUP_WIDTH), lambda b, i: (b * nq + i, 0)),
        out_shape=jax.ShapeDtypeStruct((t, GROUP_WIDTH), F32),
        scratch_shapes=[
            pltpu.VMEM((max(seq // LANES, DSA_SUB), tq, LANES), I32),
            pltpu.VMEM((IDX_HEADS, tq, LANES), F32),
            pltpu.VMEM((IDX_HEADS, tq, LANES), BF16),
            pltpu.VMEM((GROUP_HEADS, tq, LANES), BF16),
            pltpu.VMEM((tq, LANES), I32),
            pltpu.VMEM((GROUP_HEADS, tq, LANES), F32),
            pltpu.VMEM((GROUP_HEADS, tq, LANES), F32),
            pltpu.VMEM((GROUP_HEADS, tq, LANES), F32),
        ],
        compiler_params=_cparams(("arbitrary", "arbitrary")),
        name="mixer_dsa",
    )(p, p, side, p, p, p, bias, tri)


SWA_TQ = 128


def _swa_kernel(sink_ref, q_ref, kp_ref, kc_ref, vp_ref, vc_ref, bias_ref, o_ref):
    i = pl.program_id(1)
    tq = SWA_TQ
    t_l = lax.broadcasted_iota(I32, (tq, LANES), 0)
    s_l = lax.broadcasted_iota(I32, (tq, LANES), 1)
    ok_prev = s_l > t_l + jnp.where(i > 0, 0, LANES)
    ok_cur = s_l <= t_l
    kp, kc, vp, vc = kp_ref[...], kc_ref[...], vp_ref[...], vc_ref[...]
    outs = []
    for h in range(GROUP_HEADS):
        blk = q_ref[:, (h // 2) * LANES:(h // 2 + 1) * LANES]
        qm = jnp.where(_own_lane_mask(blk.shape, h), blk, jnp.zeros_like(blk))
        s_p = jnp.where(ok_prev, _nt_dot(qm, kp) + bias_ref[h, 0], NEG)
        s_c = jnp.where(ok_cur, _nt_dot(qm, kc) + bias_ref[h, 1], NEG)
        sink = sink_ref[h]
        m = jnp.maximum(jnp.max(jnp.maximum(s_p, s_c), axis=1, keepdims=True), sink)
        p_p = jnp.exp(s_p - m)
        p_c = jnp.exp(s_c - m)
        l = jnp.sum(p_p + p_c, axis=1, keepdims=True) + jnp.exp(sink - m)
        outs.append((_dot(p_p.astype(BF16), vp) + _dot(p_c.astype(BF16), vc)) / l)
    _store_pairs(o_ref, outs)


def _mixer_swa(p, sinks, bias, batch, seq):
    t = p.shape[0]
    tq = SWA_TQ
    nq = seq // tq
    cw = PT // LANES
    cur = lambda col: (lambda b, i: (b * nq + i, col))
    prev = lambda col: (lambda b, i: (b * nq + jnp.maximum(i - 1, 0), col))
    return pl.pallas_call(
        _swa_kernel,
        grid=(batch, nq),
        in_specs=[
            pl.BlockSpec(memory_space=pltpu.SMEM),
            pl.BlockSpec((tq, PT), lambda b, i: (b * nq + i, T_BQ)),
            pl.BlockSpec((tq, LANES), prev(T_K2 * cw + 1)),
            pl.BlockSpec((tq, LANES), cur(T_K2 * cw + 1)),
            pl.BlockSpec((tq, LANES), prev(T_MISC * cw + 1)),
            pl.BlockSpec((tq, LANES), cur(T_MISC * cw + 1)),
            pl.BlockSpec((GROUP_HEADS, 2, tq, LANES), lambda b, i: (1, 0, 0, 0)),
        ],
        out_specs=pl.BlockSpec((tq, GROUP_WIDTH), lambda b, i: (b * nq + i, 0)),
        out_shape=jax.ShapeDtypeStruct((t, GROUP_WIDTH), F32),
        compiler_params=_cparams(("arbitrary", "arbitrary")),
        name="mixer_swa",
    )(sinks, p, p, p, p, p, bias)


def _logsig(x):
    return jnp.minimum(x, 0.0) - jnp.log(1.0 + jnp.exp(-jnp.abs(x)))


def _foxcum_kernel(fb_ref, f_ref, tri_ref, o_ref, *, nchunk):
    def body(c, carry):
        start = pl.multiple_of(c * LANES, LANES)
        lf = _logsig(f_ref[pl.ds(start, LANES), :] + fb_ref[...]).T
        p1 = lf.astype(BF16)
        r1 = lf - p1.astype(F32)
        p2 = r1.astype(BF16)
        p3 = (r1 - p2.astype(F32)).astype(BF16)
        tri = tri_ref[...]
        cum = _dot(p1, tri) + _dot(p2, tri) + _dot(p3, tri) + carry
        o_ref[0, c] = cum[:GROUP_HEADS]
        return jnp.broadcast_to(cum[:, LANES - 1:LANES], cum.shape)

    lax.fori_loop(0, nchunk, body, jnp.zeros((LANES, LANES), F32))


def _fox_cum(side, fb_row, tri_incl, batch, seq):
    nchunk = seq // LANES
    return pl.pallas_call(
        functools.partial(_foxcum_kernel, nchunk=nchunk),
        grid=(batch,),
        in_specs=[
            pl.BlockSpec((1, LANES), lambda b: (0, 0)),
            pl.BlockSpec((seq, LANES), lambda b: (b, 0)),
            pl.BlockSpec((LANES, LANES), lambda b: (0, 0)),
        ],
        out_specs=pl.BlockSpec((1, nchunk, GROUP_HEADS, LANES), lambda b: (b, 0, 0, 0)),
        out_shape=jax.ShapeDtypeStruct((batch, nchunk, GROUP_HEADS, LANES), F32),
        compiler_params=_cparams(("arbitrary",)),
        name="fox_cumsum",
    )(fb_row, side, tri_incl)


FOX_TQ = 256
FOX_KC = 512
FOX_SUB = FOX_KC // LANES


def _fox_kernel(q_ref, g_ref, k_ref, v_ref, cum_ref, o_ref, qm_ref, m_ref, l_ref, acc_ref):
    i = pl.program_id(1)
    tq = FOX_TQ
    for h in range(GROUP_HEADS):
        blk = q_ref[:, (h // 2) * LANES:(h // 2 + 1) * LANES]
        qm_ref[h] = jnp.where(_own_lane_mask(blk.shape, h), blk, jnp.zeros_like(blk))
    m_ref[...] = jnp.full(m_ref.shape, NEG, F32)
    l_ref[...] = jnp.zeros(l_ref.shape, F32)
    acc_ref[...] = jnp.zeros(acc_ref.shape, F32)
    f_ref0 = cum_ref[0, i * (tq // LANES)][:, 0:1]

    def chunk(c, masked):
        start = pl.multiple_of(c * FOX_KC, FOX_KC)
        cum = cum_ref[0, pl.ds(c * FOX_SUB, FOX_SUB)]
        cum = jnp.concatenate([cum[u] for u in range(FOX_SUB)], axis=1)
        fbias = f_ref0 - cum
        per_head = []
        for pr in range(PAIRS):
            pair = slice(pr * LANES, (pr + 1) * LANES)
            q_pair = qm_ref[2 * pr:2 * pr + 2].reshape(2 * tq, LANES)
            s_pair = _nt_dot(q_pair, k_ref[pl.ds(start, FOX_KC), pair])
            for e in range(2):
                h = 2 * pr + e
                per_head.append(s_pair[e * tq:(e + 1) * tq] + fbias[h:h + 1, :])
        s = jnp.stack(per_head)
        if masked:
            shape = (GROUP_HEADS, tq, FOX_KC)
            ok = start + lax.broadcasted_iota(I32, shape, 2) <= i * tq + lax.broadcasted_iota(I32, shape, 1)
            s = jnp.where(ok, s, NEG)
        reps = FOX_KC // LANES
        m_prev = m_ref[...]
        m_new = jnp.maximum(m_prev, jnp.max(s, axis=2, keepdims=True))
        p = jnp.exp(s - jnp.concatenate([m_new] * reps, axis=2))
        alpha = jnp.exp(m_prev - m_new)
        l_ref[...] = alpha * l_ref[...] + jnp.sum(p, axis=2, keepdims=True)
        m_ref[...] = m_new
        p = p.astype(BF16)
        pvs = []
        for pr in range(PAIRS):
            p_pair = p[2 * pr:2 * pr + 2].reshape(2 * tq, FOX_KC)
            pvs.append(_dot(p_pair, v_ref[pl.ds(start, FOX_KC), pr * LANES:(pr + 1) * LANES]))
        acc_ref[...] = alpha * acc_ref[...] + jnp.concatenate(pvs, axis=0).reshape(GROUP_HEADS, tq, LANES)

    def body(c, carry):
        chunk(c, False)
        return carry

    n_full = (i * tq) // FOX_KC
    lax.fori_loop(0, n_full, body, 0)
    chunk(n_full, True)
    outs = []
    for h in range(GROUP_HEADS):
        gate = _sigmoid(g_ref[:, (h // 2) * LANES:(h // 2 + 1) * LANES].astype(F32))
        outs.append(acc_ref[h] / l_ref[h] * gate)
    _store_pairs(o_ref, outs)


def _mixer_fox(p, cum, batch, seq):
    t = p.shape[0]
    tq = FOX_TQ
    nq = seq // tq
    return pl.pallas_call(
        _fox_kernel,
        grid=(batch, nq),
        in_specs=[
            pl.BlockSpec((tq, PT), lambda b, i: (b * nq + i, T_CQ)),
            pl.BlockSpec((tq, PT), lambda b, i: (b * nq + i, T_CG)),
            pl.BlockSpec((seq, PT), lambda b, i: (b, T_CK)),
            pl.BlockSpec((seq, PT), lambda b, i: (b, T_CV)),
            pl.BlockSpec((1, seq // LANES, GROUP_HEADS, LANES), lambda b, i: (b, 0, 0, 0)),
        ],
        out_specs=pl.BlockSpec((tq, GROUP_WIDTH), lambda b, i: (b * nq + i, 0)),
        out_shape=jax.ShapeDtypeStruct((t, GROUP_WIDTH), F32),
        scratch_shapes=[
            pltpu.VMEM((GROUP_HEADS, tq, LANES), BF16),
            pltpu.VMEM((GROUP_HEADS, tq, LANES), F32),
            pltpu.VMEM((GROUP_HEADS, tq, LANES), F32),
            pltpu.VMEM((GROUP_HEADS, tq, LANES), F32),
        ],
        compiler_params=_cparams(("arbitrary", "arbitrary")),
        name="mixer_fox",
    )(p, p, p, p, cum)


STK_TQ = 128


def _stick_kernel(q_ref, k_ref, v_ref, tri_ref, o_ref, qm_ref, acc_ref, r_ref):
    i = pl.program_id(1)
    tq = STK_TQ
    nh = GROUP_HEADS
    for h in range(nh):
        blk = q_ref[:, (h // 2) * LANES:(h // 2 + 1) * LANES]
        qm_ref[h] = jnp.where(_own_lane_mask(blk.shape, h), blk, jnp.zeros_like(blk))
    acc_ref[...] = jnp.zeros(acc_ref.shape, F32)
    r_ref[...] = jnp.zeros(r_ref.shape, F32)
    before = (lax.broadcasted_iota(I32, (nh, tq, LANES), 2) < lax.broadcasted_iota(I32, (nh, tq, LANES), 1))
    tri = tri_ref[...]

    def block(j, masked):
        start = pl.multiple_of(j * LANES, LANES)
        zs = []
        for pr in range(PAIRS):
            q_pair = qm_ref[2 * pr:2 * pr + 2].reshape(2 * tq, LANES)
            zs.append(_nt_dot(q_pair, k_ref[pl.ds(start, LANES), pr * LANES:(pr + 1) * LANES]))
        z = jnp.concatenate(zs, axis=0).reshape(nh, tq, LANES)
        lsz = _logsig(z)
        u = lsz - z
        if masked:
            u = jnp.where(before, u, 0.0)
        u2 = u.reshape(nh * tq, LANES)
        u_hi = u2.astype(BF16)
        u_lo = (u2 - u_hi.astype(F32)).astype(BF16)
        nearer = (_dot(u_hi, tri) + _dot(u_lo, tri)).reshape(nh, tq, LANES)
        run = r_ref[...]
        w = jnp.exp(lsz + nearer + run)
        if masked:
            w = jnp.where(before, w, 0.0)
        wb = w.astype(BF16)
        pvs = []
        for pr in range(PAIRS):
            w_pair = wb[2 * pr:2 * pr + 2].reshape(2 * tq, LANES)
            pvs.append(_dot(w_pair, v_ref[pl.ds(start, LANES), pr * LANES:(pr + 1) * LANES]))
        acc_ref[...] += jnp.concatenate(pvs, axis=0).reshape(nh, tq, LANES)
        run = run + jnp.sum(u, axis=2, keepdims=True)
        r_ref[...] = run
        return jnp.max(run)

    rmax = block(i, True)

    def cond(carry):
        j, rmax = carry
        return jnp.logical_and(j >= 0, rmax >= STICK_EXIT)

    def body(carry):
        j, _ = carry
        return j - 1, block(j, False)

    lax.while_loop(cond, body, (i - 1, rmax))
    _store_pairs(o_ref, [acc_ref[h] for h in range(nh)])


def _mixer_stick(p, tri_excl, batch, seq):
    t = p.shape[0]
    tq = STK_TQ
    nq = seq // tq
    return pl.pallas_call(
        _stick_kernel,
        grid=(batch, nq),
        in_specs=[
            pl.BlockSpec((tq, PT), lambda b, i: (b * nq + i, T_DQ)),
            pl.BlockSpec((seq, PT), lambda b, i: (b, T_DK)),
            pl.BlockSpec((seq, PT), lambda b, i: (b, T_DV)),
            pl.BlockSpec((LANES, LANES), lambda b, i: (0, 0)),
        ],
        out_specs=pl.BlockSpec((tq, GROUP_WIDTH), lambda b, i: (b * nq + i, 0)),
        out_shape=jax.ShapeDtypeStruct((t, GROUP_WIDTH), F32),
        scratch_shapes=[pltpu.VMEM((GROUP_HEADS, tq, LANES), BF16),
                        pltpu.VMEM((GROUP_HEADS, tq, LANES), F32),
                        pltpu.VMEM((GROUP_HEADS, tq, LANES), F32)],
        compiler_params=_cparams(("arbitrary", "arbitrary")),
        name="mixer_stick",
    )(p, p, p, tri_excl)


def _outproj_kernel(x_ref, gt_ref, oa_ref, ob_ref, oc_ref, od_ref, gg_ref, w_ref, o_ref):
    acc = None
    for m, ref in enumerate((oa_ref, ob_ref, oc_ref, od_ref)):
        o = ref[...]
        y = o * lax.rsqrt(jnp.mean(o * o, axis=-1, keepdims=True) + EPS) * gg_ref[m:m + 1, :]
        part = _dot(y.astype(BF16), w_ref[m * GROUP_WIDTH:(m + 1) * GROUP_WIDTH, :])
        acc = part if acc is None else acc + part
    o_ref[...] = x_ref[...] + gt_ref[0] * acc


def _outproj(x, gt, outs, gg, w, l, seq):
    t, d = x.shape
    tm = 512
    per_b = seq // tm
    mix = pl.BlockSpec((tm, GROUP_WIDTH), lambda i: (i, 0))
    return pl.pallas_call(
        _outproj_kernel,
        grid=(t // tm,),
        in_specs=[
            pl.BlockSpec((tm, d), lambda i: (i, 0)),
            pl.BlockSpec((1, 1, d), lambda i: (i // per_b, 0, 0)),
            mix, mix, mix, mix,
            pl.BlockSpec((None, N_MIXERS, GROUP_WIDTH), lambda i: (l, 0, 0)),
            pl.BlockSpec((None, N_MIXERS * GROUP_WIDTH, d), lambda i: (l, 0, 0)),
        ],
        out_specs=pl.BlockSpec((tm, d), lambda i: (i, 0)),
        out_shape=jax.ShapeDtypeStruct((t, d), F32),
        compiler_params=_cparams(("arbitrary",)),
        name="outproj",
    )(x, gt, *outs, gg, w)


def _prep_in_weights(w_in, qk_g):
    depth, d, _ = w_in.shape
    sizes = (GROUP_WIDTH, HEAD_DIM, HEAD_DIM, IDX_HEADS * IDX_DIM, IDX_DIM, IDX_HEADS,
             GROUP_WIDTH, HEAD_DIM, HEAD_DIM,
             GROUP_WIDTH, GROUP_WIDTH, GROUP_WIDTH, GROUP_HEADS, GROUP_WIDTH,
             GROUP_WIDTH, GROUP_WIDTH, GROUP_WIDTH)
    pts = np.cumsum(sizes)[:-1].tolist()
    (a_q, a_k, a_v, a_iq, a_ik, a_iw, b_q, b_k, b_v,
     c_q, c_k, c_v, c_f, c_g, d_q, d_k, d_v) = jnp.split(w_in, pts, axis=-1)
    z = lambda n: jnp.zeros((depth, d, n), w_in.dtype)
    qscale = HEAD_DIM ** -0.5
    main = jnp.concatenate([
        a_q, b_q, c_q, c_k,
        a_k, a_k, b_k, b_k, z(2 * LANES),
        c_v, c_g, d_q * qscale, d_k, d_v, a_iq * (IDX_DIM ** -0.5),
        a_v, a_v, b_v, b_v, a_ik, a_ik, z(LANES),
    ], axis=-1).astype(BF16)
    side = jnp.concatenate([c_f, a_iw * (IDX_HEADS ** -0.5), z(LANES - GROUP_HEADS - IDX_HEADS)],
                           axis=-1).astype(BF16)
    rep = lambda g, n: jnp.tile(g, (1, n))
    gain = jnp.concatenate([
        rep(qk_g[:, 0], 8) * qscale, rep(qk_g[:, 2], 8) * qscale, rep(qk_g[:, 4], 8) * qscale,
        rep(qk_g[:, 5], 8), rep(qk_g[:, 1], 2), rep(qk_g[:, 3], 2),
        jnp.zeros((depth, P_COLS - 4 * PT - 2 * LANES), F32),
    ], axis=-1).reshape(depth, 1, P_COLS)
    return main, side, gain


def _np_const(shape_fn):
    return jnp.asarray(shape_fn())


def kernel(x, c, w_ada, b_ada, norm_g, w_in, qk_g, forget_b, sinks, rel_table, group_g, w_out,
           w_ffn_gate, w_ffn_up, w_ffn_down):
    batch, seq, d = x.shape
    depth = w_ada.shape[0]
    t = batch * seq

    mod = _modulation(c, w_ada, b_ada)
    wg = w_ffn_gate.astype(BF16)
    wu = w_ffn_up.astype(BF16)
    wd = w_ffn_down.astype(BF16)
    wo = w_out.astype(BF16)
    w_main, w_side, gain = _prep_in_weights(w_in, qk_g)
    gain_side = jnp.zeros((depth, 1, LANES), F32)

    r = np.arange(PT)
    bd = jnp.asarray((r[:, None] // HEAD_DIM == r[None, :] // HEAD_DIM).astype(np.float32) / HEAD_DIM, BF16)
    bd_side = jnp.zeros((LANES, LANES), BF16)
    r = np.arange(LANES)
    tri_incl = jnp.asarray(r[:, None] <= r[None, :], BF16)
    tri_after = jnp.asarray(r[:, None] > r[None, :], BF16)
    r = np.arange(DSA_KC)
    tri_before = jnp.asarray(r[:, None] < r[None, :], BF16)

    bias = _bias_tiles(rel_table, DSA_TQ)
    bias_dsa = _dsa_window_bias(bias)
    fb_rows = jnp.zeros((depth, 1, LANES), F32).at[:, 0, :GROUP_HEADS].set(forget_b)
    gg = group_g.reshape(depth, N_MIXERS, GROUP_WIDTH)

    xt = x.reshape(t, d)
    for l in range(depth):
        parts = [m.reshape(batch, 1, d) for m in jnp.split(mod[l], 9, axis=-1)]
        sh1, sc1, g1, sh2, sc2, g2, sh3, sc3, g3 = parts
        xt = _ffn(xt, norm_g[l, 0:1], sh1, sc1, g1, wg, wu, wd, l, 0, seq)
        ng2 = norm_g[l, 1:2]
        p = _inproj(xt, ng2, sh2, sc2, w_main, gain, bd, l, seq, N_NORM_TILES, BF16)
        side = _inproj(xt, ng2, sh2, sc2, w_side, gain_side, bd_side, l, seq, 0, F32)
        o_a = _mixer_dsa(p, side, bias_dsa, tri_before, batch, seq)
        o_b = _mixer_swa(p, sinks[l], bias, batch, seq)
        cum = _fox_cum(side, fb_rows[l], tri_incl, batch, seq)
        o_c = _mixer_fox(p, cum, batch, seq)
        o_d = _mixer_stick(p, tri_after, batch, seq)
        xt = _outproj(xt, g2, (o_a, o_b, o_c, o_d), gg, wo, l, seq)
        xt = _ffn(xt, norm_g[l, 2:3], sh3, sc3, g3, wg, wu, wd, l, 1, seq)
    return xt.reshape(batch, seq, d)
```

```python
import functools
import math

import numpy as np
import jax
import jax.numpy as jnp
from jax import lax
from jax.experimental import pallas as pl
from jax.experimental.pallas import tpu as pltpu

F32 = jnp.float32
BF16 = jnp.bfloat16
I32 = jnp.int32

HEAD_DIM = 64
N_MIXERS = 4
GROUP_HEADS = 8
GROUP_WIDTH = GROUP_HEADS * HEAD_DIM
IDX_HEADS = 16
IDX_DIM = 64
TOPK_MAX = 256
WINDOW = 128
REL_BUCKETS = 32
REL_MAX_DIST = 128
EPS = 1e-6

LANES = 128
PAIRS = GROUP_HEADS // 2
NEG = -1e30
INT_MIN = -2 ** 31
STICK_EXIT = -110.0
VMEM_LIMIT = 56 * 1024 * 1024

PT = 512
T_AQ, T_BQ, T_CQ, T_CK, T_K2 = 0, 1, 2, 3, 4
N_NORM_TILES = 5
T_CV, T_CG, T_DQ, T_DK, T_DV, T_IQ, T_MISC = 5, 6, 7, 8, 9, 10, 12
N_TILES = 13
P_COLS = N_TILES * PT


def _cparams(sem, vmem=VMEM_LIMIT):
    return pltpu.CompilerParams(dimension_semantics=sem, vmem_limit_bytes=vmem)


def _nt_dot(a, b):
    return lax.dot_general(a, b, (((1,), (1,)), ((), ())), preferred_element_type=F32)


def _dot(a, b):
    return jnp.dot(a, b, preferred_element_type=F32)


def _sigmoid(x):
    return 1.0 / (1.0 + jnp.exp(-x))


def _lane_tile(x, n):
    return x if n == 1 else jnp.concatenate([x] * n, axis=1)


def _own_lane_mask(shape, h):
    lane = lax.broadcasted_iota(I32, shape, 1)
    return (lane < HEAD_DIM) if h % 2 == 0 else (lane >= HEAD_DIM)


def _mod_kernel(c_ref, w_ref, b_ref, o_ref):
    c = c_ref[...]
    cond = (c * _sigmoid(c)).astype(BF16)
    o_ref[0] = _dot(cond, w_ref[0].astype(BF16)) + b_ref[0]


def _modulation(c, w_ada, b_ada):
    depth, d, n = w_ada.shape
    b = c.shape[0]
    rows = 8
    tn = 1024
    c_pad = jnp.zeros((rows, d), F32).at[:b].set(c)
    out = pl.pallas_call(
        _mod_kernel,
        grid=(depth, n // tn),
        in_specs=[
            pl.BlockSpec((rows, d), lambda l, j: (0, 0)),
            pl.BlockSpec((1, d, tn), lambda l, j: (l, 0, j)),
            pl.BlockSpec((1, 1, tn), lambda l, j: (l, 0, j)),
        ],
        out_specs=pl.BlockSpec((1, rows, tn), lambda l, j: (l, 0, j)),
        out_shape=jax.ShapeDtypeStruct((depth, rows, n), F32),
        compiler_params=_cparams(("arbitrary", "arbitrary")),
        name="adaln_mod",
    )(c_pad, w_ada, b_ada.reshape(depth, 1, n))
    return out[:, :b]


def _norm_modulate(x, ng, sh, sc):
    ms = jnp.mean(x * x, axis=-1, keepdims=True)
    y = x * lax.rsqrt(ms + EPS) * ng
    return y * (1.0 + sc) + sh


def _ffn_kernel(x_ref, ng_ref, sh_ref, sc_ref, gt_ref, wg_ref, wu_ref, wd_ref, o_ref, h_ref, a_ref):
    f = pl.program_id(1)
    last = pl.num_programs(1) - 1

    def activation():
        h = h_ref[...]
        g = _dot(h, wg_ref[...])
        u = _dot(h, wu_ref[...])
        return (g * _sigmoid(g) * u).astype(BF16)

    @pl.when(f == 0)
    def _():
        h = _norm_modulate(x_ref[...], ng_ref[...], sh_ref[0], sc_ref[0])
        h_ref[...] = h.astype(BF16)
        o_ref[...] = jnp.zeros_like(o_ref)
        a_ref[...] = activation()

    @pl.when(jnp.logical_and(f > 0, f < last))
    def _():
        o_ref[...] += _dot(a_ref[...], wd_ref[...])
        a_ref[...] = activation()

    @pl.when(f == last)
    def _():
        acc = o_ref[...] + _dot(a_ref[...], wd_ref[...])
        o_ref[...] = x_ref[...] + 0.5 * gt_ref[0] * acc


def _ffn(x, ng, sh, sc, gt, wg, wu, wd, l, s, seq):
    t, d = x.shape
    tm = 512
    nf, tf = wg.shape[2], wg.shape[-1]
    per_b = seq // tm
    vec = pl.BlockSpec((1, 1, d), lambda i, f: (i // per_b, 0, 0))
    up = lambda i, f: (l, s, jnp.minimum(f, nf - 1), 0, 0)
    return pl.pallas_call(
        _ffn_kernel,
        grid=(t // tm, nf + 1),
        in_specs=[
            pl.BlockSpec((tm, d), lambda i, f: (i, 0)),
            pl.BlockSpec((1, d), lambda i, f: (0, 0)),
            vec, vec, vec,
            pl.BlockSpec((None, None, None, d, tf), up),
            pl.BlockSpec((None, None, None, d, tf), up),
            pl.BlockSpec((None, None, tf, d), lambda i, f: (l, s, jnp.maximum(f - 1, 0), 0)),
        ],
        out_specs=pl.BlockSpec((tm, d), lambda i, f: (i, 0)),
        out_shape=jax.ShapeDtypeStruct((t, d), F32),
        scratch_shapes=[pltpu.VMEM((tm, d), BF16), pltpu.VMEM((tm, tf), BF16)],
        compiler_params=_cparams(("arbitrary", "arbitrary")),
        name="ffn",
    )(x, ng, sh, sc, gt, wg, wu, wd)


def _inproj_kernel(x_ref, ng_ref, sh_ref, sc_ref, w_ref, gain_ref, bd_ref, o_ref, h_ref, *, n_norm):
    j = pl.program_id(1)

    @pl.when(j == 0)
    def _():
        h = _norm_modulate(x_ref[...], ng_ref[...], sh_ref[0], sc_ref[0])
        h_ref[...] = h.astype(BF16)

    y = _dot(h_ref[...], w_ref[...])

    @pl.when(j < n_norm)
    def _():
        sq = y * y
        hi = sq.astype(BF16)
        lo = (sq - hi.astype(F32)).astype(BF16)
        ms = _dot(hi, bd_ref[...]) + _dot(lo, bd_ref[...])
        o_ref[...] = (y * lax.rsqrt(ms + EPS) * gain_ref[...]).astype(o_ref.dtype)

    @pl.when(j >= n_norm)
    def _():
        o_ref[...] = y.astype(o_ref.dtype)


def _inproj(x, ng, sh, sc, w, gain, bd, l, seq, n_norm, out_dtype):
    t, d = x.shape
    tm = min(1024, seq)
    nt, tn = w.shape[1], w.shape[-1]
    n = nt * tn
    per_b = seq // tm
    vec = pl.BlockSpec((1, 1, d), lambda i, j: (i // per_b, 0, 0))
    return pl.pallas_call(
        functools.partial(_inproj_kernel, n_norm=n_norm),
        grid=(t // tm, n // tn),
        in_specs=[
            pl.BlockSpec((tm, d), lambda i, j: (i, 0)),
            pl.BlockSpec((1, d), lambda i, j: (0, 0)),
            vec, vec,
            pl.BlockSpec((None, None, d, tn), lambda i, j: (l, j, 0, 0)),
            pl.BlockSpec((None, 1, tn), lambda i, j: (l, 0, j)),
            pl.BlockSpec((tn, tn), lambda i, j: (0, 0)),
        ],
        out_specs=pl.BlockSpec((tm, tn), lambda i, j: (i, j)),
        out_shape=jax.ShapeDtypeStruct((t, n), out_dtype),
        scratch_shapes=[pltpu.VMEM((tm, d), BF16)],
        compiler_params=_cparams(("arbitrary", "arbitrary")),
        name="inproj",
    )(x, ng, sh, sc, w, gain, bd)


def _rel_bucket_np(dist):
    n = np.maximum(dist, 0)
    max_exact = REL_BUCKETS // 2
    nf = np.maximum(n, 1).astype(np.float32)
    large = max_exact + (np.log(nf / np.float32(max_exact)) / np.float32(math.log(REL_MAX_DIST / max_exact))
                         * np.float32(REL_BUCKETS - max_exact)).astype(np.int32)
    large = np.minimum(large, REL_BUCKETS - 1)
    return np.where(n < max_exact, n, large).astype(np.int32)


def _bias_kernel(tab_ref, bkt_ref, o_ref, *, sub_far):
    h = pl.program_id(0)
    far = tab_ref[REL_BUCKETS - 1, h]
    for which in range(2):
        bkt = bkt_ref[which]
        acc = jnp.zeros(bkt.shape, F32)
        for b in range(REL_BUCKETS):
            acc = jnp.where(bkt == b, tab_ref[b, h], acc)
        o_ref[0, which] = acc - jnp.where(h < sub_far, far, 0.0)


def _bias_tiles(rel_table, tq):
    t_l = np.arange(tq)[:, None]
    s_l = np.arange(LANES)[None, :]
    bkt = np.stack([_rel_bucket_np(t_l + LANES - s_l), _rel_bucket_np(t_l - s_l)]).astype(np.int32)
    nh = rel_table.shape[1]
    return pl.pallas_call(
        functools.partial(_bias_kernel, sub_far=GROUP_HEADS),
        grid=(nh,),
        in_specs=[
            pl.BlockSpec(memory_space=pltpu.SMEM),
            pl.BlockSpec((2, tq, LANES), lambda h: (0, 0, 0)),
        ],
        out_specs=pl.BlockSpec((1, 2, tq, LANES), lambda h: (h, 0, 0, 0)),
        out_shape=jax.ShapeDtypeStruct((nh, 2, tq, LANES), F32),
        compiler_params=_cparams(("arbitrary",)),
        name="rel_bias_tiles",
    )(rel_table, jnp.asarray(bkt))


def _dsa_window_bias(bias):
    prev, cur = bias[:GROUP_HEADS, 0], bias[:GROUP_HEADS, 1]
    return jnp.stack([jnp.concatenate([cur, cur], axis=-1), jnp.concatenate([prev, cur], axis=-1)])


def _softmax_step(s, v, m_ref, l_ref, acc_ref, idx):
    reps = s.shape[1] // LANES
    m_prev = m_ref[idx]
    l_prev = l_ref[idx]
    m_cur = jnp.max(s, axis=1, keepdims=True)
    m_new = jnp.maximum(m_prev, m_cur)
    p = jnp.exp(s - _lane_tile(m_new, reps))
    alpha = jnp.exp(m_prev - m_new)
    l_ref[idx] = alpha * l_prev + jnp.sum(p, axis=1, keepdims=True)
    m_ref[idx] = m_new
    acc_ref[idx] = alpha * acc_ref[idx] + _dot(p.astype(BF16), v)


def _softmax_step_stacked(s, v, m_ref, l_ref, acc_ref):
    nh, rows, width = s.shape
    reps = width // LANES
    m_prev = m_ref[...]
    m_new = jnp.maximum(m_prev, jnp.max(s, axis=2, keepdims=True))
    p = jnp.exp(s - jnp.concatenate([m_new] * reps, axis=2))
    alpha = jnp.exp(m_prev - m_new)
    l_ref[...] = alpha * l_ref[...] + jnp.sum(p, axis=2, keepdims=True)
    m_ref[...] = m_new
    pv = _dot(p.astype(BF16).reshape(nh * rows, width), v).reshape(nh, rows, LANES)
    acc_ref[...] = alpha * acc_ref[...] + pv


def _store_pairs(o_ref, per_head):
    for j in range(PAIRS):
        lane = lax.broadcasted_iota(I32, per_head[0].shape, 1)
        o_ref[:, j * LANES:(j + 1) * LANES] = jnp.where(
            lane < HEAD_DIM, per_head[2 * j], per_head[2 * j + 1]).astype(o_ref.dtype)


DSA_TQ = 128
DSA_KC1 = 512
DSA_KC = 512
DSA_SUB = DSA_KC // LANES


def _dsa_kernel(q_ref, iq_ref, iw_ref, k2_ref, v2_ref, ik2_ref, bias_ref, tri_ref, o_ref,
                key_ref, wb_ref, iqm_ref, qm_ref, t_ref, m_ref, l_ref, acc_ref, *, k_top):
    i = pl.program_id(1)
    tq = DSA_TQ
    row0 = i * tq
    n_valid_cols = row0 + tq

    iw = iw_ref[...]
    for h in range(IDX_HEADS):
        col = GROUP_HEADS + h
        wb_ref[h] = jnp.broadcast_to(iw[:, col:col + 1], (tq, LANES))
        blk = iq_ref[:, (h // 2) * LANES:(h // 2 + 1) * LANES]
        iqm_ref[h] = jnp.where(_own_lane_mask(blk.shape, h), blk, jnp.zeros_like(blk))
    for h in range(GROUP_HEADS):
        blk = q_ref[:, (h // 2) * LANES:(h // 2 + 1) * LANES]
        qm_ref[h] = jnp.where(_own_lane_mask(blk.shape, h), blk, jnp.zeros_like(blk))

    qpos = row0 + lax.broadcasted_iota(I32, (tq, DSA_KC1), 0)
    iq_all = iqm_ref[...].reshape(IDX_HEADS * tq, LANES)
    wb_all = jnp.concatenate([wb_ref[...]] * (DSA_KC1 // LANES), axis=2)
    n1 =(n_valid_cols + DSA_KC1 - 1) // DSA_KC1

    def score_chunk(c, carry):
        start = pl.multiple_of(c * DSA_KC1, DSA_KC1)
        ik = ik2_ref[pl.ds(start, DSA_KC1), :]
        r = _nt_dot(iq_all, ik).reshape(IDX_HEADS, tq, DSA_KC1)
        score = jnp.sum(jnp.maximum(r, 0.0) * wb_all, axis=0)
        bits = lax.bitcast_convert_type(score, I32)
        key = bits ^ ((bits >> 31) & 0x7FFFFFFF)
        key = jnp.where(score == 0.0, 0, key)
        kpos = start + lax.broadcasted_iota(I32, (tq, DSA_KC1), 1)
        key = jnp.where(kpos <= qpos, key, INT_MIN)
        for u in range(DSA_KC1 // LANES):
            key_ref[c * (DSA_KC1 // LANES) + u] = key[:, u * LANES:(u + 1) * LANES]
        return carry

    lax.fori_loop(0, n1, score_chunk, 0)
    nsel = (n_valid_cols + DSA_KC - 1) // DSA_KC
    n1_blocks = n1 * (DSA_KC1 // LANES)

    @pl.when(n1_blocks < nsel * DSA_SUB)
    def _():
        for u in range(DSA_KC1 // LANES):
            key_ref[n1_blocks + u] = jnp.full((tq, LANES), INT_MIN, I32)

    def load_keys(c):
        blk = key_ref[pl.ds(c * DSA_SUB, DSA_SUB)]
        return jnp.concatenate([blk[u] for u in range(DSA_SUB)], axis=1)

    def count_ge(thr, strict):
        def body(c, acc):
            blk = key_ref[pl.ds(c * DSA_SUB, DSA_SUB)]
            for u in range(DSA_SUB):
                hit = (blk[u] > thr) if strict else (blk[u] >= thr)
                acc = jnp.where(hit, acc + 1.0, acc)
            return acc
        acc = lax.fori_loop(0, nsel, body, jnp.zeros((tq, LANES), F32))
        return jnp.broadcast_to(jnp.sum(acc, axis=1, keepdims=True), (tq, LANES))

    t_ref[...] = jnp.full((tq, LANES), INT_MIN, I32)
    kf = float(k_top)

    @pl.when(n_valid_cols > k_top)
    def _():
        def bit_step(it, carry):
            thr, cnt_thr = carry
            cand = thr + jnp.left_shift(jnp.int32(1), 31 - it)
            cnt = count_ge(cand, False)
            take = cnt >= kf
            return jnp.where(take, cand, thr), jnp.where(take, cnt, cnt_thr)

        thr0 = jnp.full((tq, LANES), INT_MIN, I32)
        cnt0 = jnp.full((tq, LANES), float(2 ** 24), F32)
        thr, cnt_thr = lax.fori_loop(0, 32, bit_step, (thr0, cnt0))
        t_ref[...] = thr

        excess = jnp.where(thr > INT_MIN, cnt_thr - kf, 0.0)

        @pl.when(jnp.max(excess) > 0.0)
        def _():
            need = kf - count_ge(thr, True)

            def tie_chunk(c, seen):
                keys = load_keys(c)
                eq = keys == _lane_tile(thr, DSA_SUB)
                eqf = jnp.where(eq, 1.0, 0.0)
                rank = _dot(eqf.astype(BF16), tri_ref[...]) + _lane_tile(seen, DSA_SUB)
                dropped = jnp.where(rank >= _lane_tile(need, DSA_SUB), INT_MIN, keys)
                keys = jnp.where(eq, dropped, keys)
                for u in range(DSA_SUB):
                    key_ref[c * DSA_SUB + u] = keys[:, u * LANES:(u + 1) * LANES]
                return seen + jnp.broadcast_to(jnp.sum(eqf, axis=1, keepdims=True), (tq, LANES))

            lax.fori_loop(0, nsel, tie_chunk, jnp.zeros((tq, LANES), F32))

    thr = jnp.maximum(t_ref[...], INT_MIN + 1)

    m_ref[...] = jnp.full(m_ref.shape, NEG, F32)
    l_ref[...] = jnp.zeros(l_ref.shape, F32)
    acc_ref[...] = jnp.zeros(acc_ref.shape, F32)
    q_all = qm_ref[...].reshape(GROUP_HEADS * tq, LANES)

    def attend(k_blk, v_blk, madd, bias):
        width = k_blk.shape[0]
        s = _nt_dot(q_all, k_blk).reshape(GROUP_HEADS, tq, width) + madd[None]
        if bias is not None:
            s = s + bias
        _softmax_step_stacked(s, v_blk, m_ref, l_ref, acc_ref)

    win_blk = jnp.maximum(i - 1, 0)
    win_start = pl.multiple_of(win_blk * LANES, LANES)
    near_keys = key_ref[pl.ds(win_blk, 2)]
    near_keys = jnp.concatenate([near_keys[0], near_keys[1]], axis=1)
    near_madd = jnp.where(near_keys >= _lane_tile(thr, 2), 0.0, NEG)
    attend(k2_ref[pl.ds(win_start, 2 * LANES), :], v2_ref[pl.ds(win_start, 2 * LANES), :],
           near_madd, bias_ref[jnp.minimum(i, 1)])
    for u in range(2):
        key_ref[win_blk + u] = jnp.full((tq, LANES), INT_MIN, I32)

    n_far = (win_blk * LANES + DSA_KC - 1) // DSA_KC

    def far_chunk(c, carry):
        start = pl.multiple_of(c * DSA_KC, DSA_KC)
        madd = jnp.where(load_keys(c) >= _lane_tile(thr, DSA_SUB), 0.0, NEG)
        attend(k2_ref[pl.ds(start, DSA_KC), :], v2_ref[pl.ds(start, DSA_KC), :], madd, None)
        return carry

    lax.fori_loop(0, n_far, far_chunk, 0)
    _store_pairs(o_ref, [acc_ref[h] / l_ref[h] for h in range(GROUP_HEADS)])


def _mixer_dsa(p, side, bias, tri, batch, seq):
    t = p.shape[0]
    tq = DSA_TQ
    nq = seq // tq
    k_top = min(TOPK_MAX, seq // 4)
    cw = PT // LANES
    return pl.pallas_call(
        functools.partial(_dsa_kernel, k_top=k_top),
        grid=(batch, nq),
        in_specs=[
            pl.BlockSpec((tq, PT), lambda b, i: (b * nq + i, T_AQ)),
            pl.BlockSpec((tq, 2 * PT), lambda b, i: (b * nq + i, T_IQ // 2)),
            pl.BlockSpec((tq, LANES), lambda b, i: (b * nq + i, 0)),
            pl.BlockSpec((seq, LANES), lambda b, i: (b, T_K2 * cw)),
            pl.BlockSpec((seq, LANES), lambda b, i: (b, T_MISC * cw)),
            pl.BlockSpec((seq, LANES), lambda b, i: (b, T_MISC * cw + 2)),
            pl.BlockSpec((2, GROUP_HEADS, tq, 2 * LANES), lambda b, i: (0, 0, 0, 0)),
            pl.BlockSpec((DSA_KC, DSA_KC), lambda b, i: (0, 0)),
        ],
        out_specs=pl.BlockSpec((tq, GROUP_WIDTH), lambda b, i: (b * nq + i, 0)),
        out_shape=jax.ShapeDtypeStruct((t, GROUP_WIDTH), F32),
        scratch_shapes=[
            pltpu.VMEM((max(seq // LANES, DSA_SUB), tq, LANES), I32),
            pltpu.VMEM((IDX_HEADS, tq, LANES), F32),
            pltpu.VMEM((IDX_HEADS, tq, LANES), BF16),
            pltpu.VMEM((GROUP_HEADS, tq, LANES), BF16),
            pltpu.VMEM((tq, LANES), I32),
            pltpu.VMEM((GROUP_HEADS, tq, LANES), F32),
            pltpu.VMEM((GROUP_HEADS, tq, LANES), F32),
            pltpu.VMEM((GROUP_HEADS, tq, LANES), F32),
        ],
        compiler_params=_cparams(("arbitrary", "arbitrary")),
        name="mixer_dsa",
    )(p, p, side, p, p, p, bias, tri)


SWA_TQ = 128


def _swa_kernel(sink_ref, q_ref, kp_ref, kc_ref, vp_ref, vc_ref, bias_ref, o_ref):
    i = pl.program_id(1)
    tq = SWA_TQ
    t_l = lax.broadcasted_iota(I32, (tq, LANES), 0)
    s_l = lax.broadcasted_iota(I32, (tq, LANES), 1)
    ok_prev = s_l > t_l + jnp.where(i > 0, 0, LANES)
    ok_cur = s_l <= t_l
    kp, kc, vp, vc = kp_ref[...], kc_ref[...], vp_ref[...], vc_ref[...]
    outs = []
    for h in range(GROUP_HEADS):
        blk = q_ref[:, (h // 2) * LANES:(h // 2 + 1) * LANES]
        qm = jnp.where(_own_lane_mask(blk.shape, h), blk, jnp.zeros_like(blk))
        s_p = jnp.where(ok_prev, _nt_dot(qm, kp) + bias_ref[h, 0], NEG)
        s_c = jnp.where(ok_cur, _nt_dot(qm, kc) + bias_ref[h, 1], NEG)
        sink = sink_ref[h]
        m = jnp.maximum(jnp.max(jnp.maximum(s_p, s_c), axis=1, keepdims=True), sink)
        p_p = jnp.exp(s_p - m)
        p_c = jnp.exp(s_c - m)
        l = jnp.sum(p_p + p_c, axis=1, keepdims=True) + jnp.exp(sink - m)
        outs.append((_dot(p_p.astype(BF16), vp) + _dot(p_c.astype(BF16), vc)) / l)
    _store_pairs(o_ref, outs)


def _mixer_swa(p, sinks, bias, batch, seq):
    t = p.shape[0]
    tq = SWA_TQ
    nq = seq // tq
    cw = PT // LANES
    cur = lambda col: (lambda b, i: (b * nq + i, col))
    prev = lambda col: (lambda b, i: (b * nq + jnp.maximum(i - 1, 0), col))
    return pl.pallas_call(
        _swa_kernel,
        grid=(batch, nq),
        in_specs=[
            pl.BlockSpec(memory_space=pltpu.SMEM),
            pl.BlockSpec((tq, PT), lambda b, i: (b * nq + i, T_BQ)),
            pl.BlockSpec((tq, LANES), prev(T_K2 * cw + 1)),
            pl.BlockSpec((tq, LANES), cur(T_K2 * cw + 1)),
            pl.BlockSpec((tq, LANES), prev(T_MISC * cw + 1)),
            pl.BlockSpec((tq, LANES), cur(T_MISC * cw + 1)),
            pl.BlockSpec((GROUP_HEADS, 2, tq, LANES), lambda b, i: (1, 0, 0, 0)),
        ],
        out_specs=pl.BlockSpec((tq, GROUP_WIDTH), lambda b, i: (b * nq + i, 0)),
        out_shape=jax.ShapeDtypeStruct((t, GROUP_WIDTH), F32),
        compiler_params=_cparams(("arbitrary", "arbitrary")),
        name="mixer_swa",
    )(sinks, p, p, p, p, p, bias)


def _logsig(x):
    return jnp.minimum(x, 0.0) - jnp.log(1.0 + jnp.exp(-jnp.abs(x)))


def _foxcum_kernel(fb_ref, f_ref, tri_ref, o_ref, *, nchunk):
    def body(c, carry):
        start = pl.multiple_of(c * LANES, LANES)
        lf = _logsig(f_ref[pl.ds(start, LANES), :] + fb_ref[...]).T
        p1 = lf.astype(BF16)
        r1 = lf - p1.astype(F32)
        p2 = r1.astype(BF16)
        p3 = (r1 - p2.astype(F32)).astype(BF16)
        tri = tri_ref[...]
        cum = _dot(p1, tri) + _dot(p2, tri) + _dot(p3, tri) + carry
        o_ref[0, c] = cum[:GROUP_HEADS]
        return jnp.broadcast_to(cum[:, LANES - 1:LANES], cum.shape)

    lax.fori_loop(0, nchunk, body, jnp.zeros((LANES, LANES), F32))


def _fox_cum(side, fb_row, tri_incl, batch, seq):
    nchunk = seq // LANES
    return pl.pallas_call(
        functools.partial(_foxcum_kernel, nchunk=nchunk),
        grid=(batch,),
        in_specs=[
            pl.BlockSpec((1, LANES), lambda b: (0, 0)),
            pl.BlockSpec((seq, LANES), lambda b: (b, 0)),
            pl.BlockSpec((LANES, LANES), lambda b: (0, 0)),
        ],
        out_specs=pl.BlockSpec((1, nchunk, GROUP_HEADS, LANES), lambda b: (b, 0, 0, 0)),
        out_shape=jax.ShapeDtypeStruct((batch, nchunk, GROUP_HEADS, LANES), F32),
        compiler_params=_cparams(("arbitrary",)),
        name="fox_cumsum",
    )(fb_row, side, tri_incl)


FOX_TQ = 256
FOX_KC = 512
FOX_SUB = FOX_KC // LANES


def _fox_kernel(q_ref, g_ref, k_ref, v_ref, cum_ref, o_ref, qm_ref, m_ref, l_ref, acc_ref):
    i = pl.program_id(1)
    tq = FOX_TQ
    for h in range(GROUP_HEADS):
        blk = q_ref[:, (h // 2) * LANES:(h // 2 + 1) * LANES]
        qm_ref[h] = jnp.where(_own_lane_mask(blk.shape, h), blk, jnp.zeros_like(blk))
    m_ref[...] = jnp.full(m_ref.shape, NEG, F32)
    l_ref[...] = jnp.zeros(l_ref.shape, F32)
    acc_ref[...] = jnp.zeros(acc_ref.shape, F32)
    f_ref0 = cum_ref[0, i * (tq // LANES)][:, 0:1]

    def chunk(c, masked):
        start = pl.multiple_of(c * FOX_KC, FOX_KC)
        cum = cum_ref[0, pl.ds(c * FOX_SUB, FOX_SUB)]
        cum = jnp.concatenate([cum[u] for u in range(FOX_SUB)], axis=1)
        fbias = f_ref0 - cum
        per_head = []
        for pr in range(PAIRS):
            pair = slice(pr * LANES, (pr + 1) * LANES)
            q_pair = qm_ref[2 * pr:2 * pr + 2].reshape(2 * tq, LANES)
            s_pair = _nt_dot(q_pair, k_ref[pl.ds(start, FOX_KC), pair])
            for e in range(2):
                h = 2 * pr + e
                per_head.append(s_pair[e * tq:(e + 1) * tq] + fbias[h:h + 1, :])
        s = jnp.stack(per_head)
        if masked:
            shape = (GROUP_HEADS, tq, FOX_KC)
            ok = start + lax.broadcasted_iota(I32, shape, 2) <= i * tq + lax.broadcasted_iota(I32, shape, 1)
            s = jnp.where(ok, s, NEG)
        reps = FOX_KC // LANES
        m_prev = m_ref[...]
        m_new = jnp.maximum(m_prev, jnp.max(s, axis=2, keepdims=True))
        p = jnp.exp(s - jnp.concatenate([m_new] * reps, axis=2))
        alpha = jnp.exp(m_prev - m_new)
        l_ref[...] = alpha * l_ref[...] + jnp.sum(p, axis=2, keepdims=True)
        m_ref[...] = m_new
        p = p.astype(BF16)
        pvs = []
        for pr in range(PAIRS):
            p_pair = p[2 * pr:2 * pr + 2].reshape(2 * tq, FOX_KC)
            pvs.append(_dot(p_pair, v_ref[pl.ds(start, FOX_KC), pr * LANES:(pr + 1) * LANES]))
        acc_ref[...] = alpha * acc_ref[...] + jnp.concatenate(pvs, axis=0).reshape(GROUP_HEADS, tq, LANES)

    def body(c, carry):
        chunk(c, False)
        return carry

    n_full = (i * tq) // FOX_KC
    lax.fori_loop(0, n_full, body, 0)
    chunk(n_full, True)
    outs = []
    for h in range(GROUP_HEADS):
        gate = _sigmoid(g_ref[:, (h // 2) * LANES:(h // 2 + 1) * LANES].astype(F32))
        outs.append(acc_ref[h] / l_ref[h] * gate)
    _store_pairs(o_ref, outs)


def _mixer_fox(p, cum, batch, seq):
    t = p.shape[0]
    tq = FOX_TQ
    nq = seq // tq
    return pl.pallas_call(
        _fox_kernel,
        grid=(batch, nq),
        in_specs=[
            pl.BlockSpec((tq, PT), lambda b, i: (b * nq + i, T_CQ)),
            pl.BlockSpec((tq, PT), lambda b, i: (b * nq + i, T_CG)),
            pl.BlockSpec((seq, PT), lambda b, i: (b, T_CK)),
            pl.BlockSpec((seq, PT), lambda b, i: (b, T_CV)),
            pl.BlockSpec((1, seq // LANES, GROUP_HEADS, LANES), lambda b, i: (b, 0, 0, 0)),
        ],
        out_specs=pl.BlockSpec((tq, GROUP_WIDTH), lambda b, i: (b * nq + i, 0)),
        out_shape=jax.ShapeDtypeStruct((t, GROUP_WIDTH), F32),
        scratch_shapes=[
            pltpu.VMEM((GROUP_HEADS, tq, LANES), BF16),
            pltpu.VMEM((GROUP_HEADS, tq, LANES), F32),
            pltpu.VMEM((GROUP_HEADS, tq, LANES), F32),
            pltpu.VMEM((GROUP_HEADS, tq, LANES), F32),
        ],
        compiler_params=_cparams(("arbitrary", "arbitrary")),
        name="mixer_fox",
    )(p, p, p, p, cum)


STK_TQ = 128


def _stick_kernel(q_ref, k_ref, v_ref, tri_ref, o_ref, qm_ref, acc_ref, r_ref):
    i = pl.program_id(1)
    tq = STK_TQ
    nh = GROUP_HEADS
    for h in range(nh):
        blk = q_ref[:, (h // 2) * LANES:(h // 2 + 1) * LANES]
        qm_ref[h] = jnp.where(_own_lane_mask(blk.shape, h), blk, jnp.zeros_like(blk))
    acc_ref[...] = jnp.zeros(acc_ref.shape, F32)
    r_ref[...] = jnp.zeros(r_ref.shape, F32)
    before = (lax.broadcasted_iota(I32, (nh, tq, LANES), 2) < lax.broadcasted_iota(I32, (nh, tq, LANES), 1))
    tri = tri_ref[...]

    def block(j, masked):
        start = pl.multiple_of(j * LANES, LANES)
        zs = []
        for pr in range(PAIRS):
            q_pair = qm_ref[2 * pr:2 * pr + 2].reshape(2 * tq, LANES)
            zs.append(_nt_dot(q_pair, k_ref[pl.ds(start, LANES), pr * LANES:(pr + 1) * LANES]))
        z = jnp.concatenate(zs, axis=0).reshape(nh, tq, LANES)
        lsz = _logsig(z)
        u = lsz - z
        if masked:
            u = jnp.where(before, u, 0.0)
        u2 = u.reshape(nh * tq, LANES)
        u_hi = u2.astype(BF16)
        u_lo = (u2 - u_hi.astype(F32)).astype(BF16)
        nearer = (_dot(u_hi, tri) + _dot(u_lo, tri)).reshape(nh, tq, LANES)
        run = r_ref[...]
        w = jnp.exp(lsz + nearer + run)
        if masked:
            w = jnp.where(before, w, 0.0)
        wb = w.astype(BF16)
        pvs = []
        for pr in range(PAIRS):
            w_pair = wb[2 * pr:2 * pr + 2].reshape(2 * tq, LANES)
            pvs.append(_dot(w_pair, v_ref[pl.ds(start, LANES), pr * LANES:(pr + 1) * LANES]))
        acc_ref[...] += jnp.concatenate(pvs, axis=0).reshape(nh, tq, LANES)
        run = run + jnp.sum(u, axis=2, keepdims=True)
        r_ref[...] = run
        return jnp.max(run)

    rmax = block(i, True)

    def cond(carry):
        j, rmax = carry
        return jnp.logical_and(j >= 0, rmax >= STICK_EXIT)

    def body(carry):
        j, _ = carry
        return j - 1, block(j, False)

    lax.while_loop(cond, body, (i - 1, rmax))
    _store_pairs(o_ref, [acc_ref[h] for h in range(nh)])


def _mixer_stick(p, tri_excl, batch, seq):
    t = p.shape[0]
    tq = STK_TQ
    nq = seq // tq
    return pl.pallas_call(
        _stick_kernel,
        grid=(batch, nq),
        in_specs=[
            pl.BlockSpec((tq, PT), lambda b, i: (b * nq + i, T_DQ)),
            pl.BlockSpec((seq, PT), lambda b, i: (b, T_DK)),
            pl.BlockSpec((seq, PT), lambda b, i: (b, T_DV)),
            pl.BlockSpec((LANES, LANES), lambda b, i: (0, 0)),
        ],
        out_specs=pl.BlockSpec((tq, GROUP_WIDTH), lambda b, i: (b * nq + i, 0)),
        out_shape=jax.ShapeDtypeStruct((t, GROUP_WIDTH), F32),
        scratch_shapes=[pltpu.VMEM((GROUP_HEADS, tq, LANES), BF16),
                        pltpu.VMEM((GROUP_HEADS, tq, LANES), F32),
                        pltpu.VMEM((GROUP_HEADS, tq, LANES), F32)],
        compiler_params=_cparams(("arbitrary", "arbitrary")),
        name="mixer_stick",
    )(p, p, p, tri_excl)


def _outproj_kernel(x_ref, gt_ref, oa_ref, ob_ref, oc_ref, od_ref, gg_ref, w_ref, o_ref):
    acc = None
    for m, ref in enumerate((oa_ref, ob_ref, oc_ref, od_ref)):
        o = ref[...]
        y = o * lax.rsqrt(jnp.mean(o * o, axis=-1, keepdims=True) + EPS) * gg_ref[m:m + 1, :]
        part = _dot(y.astype(BF16), w_ref[m * GROUP_WIDTH:(m + 1) * GROUP_WIDTH, :])
        acc = part if acc is None else acc + part
    o_ref[...] = x_ref[...] + gt_ref[0] * acc


def _outproj(x, gt, outs, gg, w, l, seq):
    t, d = x.shape
    tm = 512
    per_b = seq // tm
    mix = pl.BlockSpec((tm, GROUP_WIDTH), lambda i: (i, 0))
    return pl.pallas_call(
        _outproj_kernel,
        grid=(t // tm,),
        in_specs=[
            pl.BlockSpec((tm, d), lambda i: (i, 0)),
            pl.BlockSpec((1, 1, d), lambda i: (i // per_b, 0, 0)),
            mix, mix, mix, mix,
            pl.BlockSpec((None, N_MIXERS, GROUP_WIDTH), lambda i: (l, 0, 0)),
            pl.BlockSpec((None, N_MIXERS * GROUP_WIDTH, d), lambda i: (l, 0, 0)),
        ],
        out_specs=pl.BlockSpec((tm, d), lambda i: (i, 0)),
        out_shape=jax.ShapeDtypeStruct((t, d), F32),
        compiler_params=_cparams(("arbitrary",)),
        name="outproj",
    )(x, gt, *outs, gg, w)


def _prep_in_weights(w_in, qk_g):
    depth, d, _ = w_in.shape
    sizes = (GROUP_WIDTH, HEAD_DIM, HEAD_DIM, IDX_HEADS * IDX_DIM, IDX_DIM, IDX_HEADS,
             GROUP_WIDTH, HEAD_DIM, HEAD_DIM,
             GROUP_WIDTH, GROUP_WIDTH, GROUP_WIDTH, GROUP_HEADS, GROUP_WIDTH,
             GROUP_WIDTH, GROUP_WIDTH, GROUP_WIDTH)
    pts = np.cumsum(sizes)[:-1].tolist()
    (a_q, a_k, a_v, a_iq, a_ik, a_iw, b_q, b_k, b_v,
     c_q, c_k, c_v, c_f, c_g, d_q, d_k, d_v) = jnp.split(w_in, pts, axis=-1)
    z = lambda n: jnp.zeros((depth, d, n), w_in.dtype)
    qscale = HEAD_DIM ** -0.5
    main = jnp.concatenate([
        a_q, b_q, c_q, c_k,
        a_k, a_k, b_k, b_k, z(2 * LANES),
        c_v, c_g, d_q * qscale, d_k, d_v, a_iq * (IDX_DIM ** -0.5),
        a_v, a_v, b_v, b_v, a_ik, a_ik, z(LANES),
    ], axis=-1).astype(BF16)
    side = jnp.concatenate([c_f, a_iw * (IDX_HEADS ** -0.5), z(LANES - GROUP_HEADS - IDX_HEADS)],
                           axis=-1).astype(BF16)
    rep = lambda g, n: jnp.tile(g, (1, n))
    gain = jnp.concatenate([
        rep(qk_g[:, 0], 8) * qscale, rep(qk_g[:, 2], 8) * qscale, rep(qk_g[:, 4], 8) * qscale,
        rep(qk_g[:, 5], 8), rep(qk_g[:, 1], 2), rep(qk_g[:, 3], 2),
        jnp.zeros((depth, P_COLS - 4 * PT - 2 * LANES), F32),
    ], axis=-1).reshape(depth, 1, P_COLS)
    return main, side, gain


FFN_TF = 512


def _tile_major(w, tn):
    *lead, k, n = w.shape
    return jnp.moveaxis(w.reshape(*lead, k, n // tn, tn), -2, -3)


def kernel(x, c, w_ada, b_ada, norm_g, w_in, qk_g, forget_b, sinks, rel_table, group_g, w_out,
           w_ffn_gate, w_ffn_up, w_ffn_down):
    batch, seq, d = x.shape
    depth = w_ada.shape[0]
    t = batch * seq

    mod = _modulation(c, w_ada, b_ada)
    wg = _tile_major(w_ffn_gate.astype(BF16), FFN_TF)
    wu = _tile_major(w_ffn_up.astype(BF16), FFN_TF)
    wd = w_ffn_down.astype(BF16)
    wo = w_out.astype(BF16)
    w_main, w_side, gain = _prep_in_weights(w_in, qk_g)
    w_main, w_side = _tile_major(w_main, PT), _tile_major(w_side, LANES)
    gain_side = jnp.zeros((depth, 1, LANES), F32)

    r = np.arange(PT)
    bd = jnp.asarray((r[:, None] // HEAD_DIM == r[None, :] // HEAD_DIM).astype(np.float32) / HEAD_DIM, BF16)
    bd_side = jnp.zeros((LANES, LANES), BF16)
    r = np.arange(LANES)
    tri_incl = jnp.asarray(r[:, None] <= r[None, :], BF16)
    tri_after = jnp.asarray(r[:, None] > r[None, :], BF16)
    r = np.arange(DSA_KC)
    tri_before = jnp.asarray(r[:, None] < r[None, :], BF16)

    bias = _bias_tiles(rel_table, DSA_TQ)
    bias_dsa = _dsa_window_bias(bias)
    fb_rows = jnp.zeros((depth, 1, LANES), F32).at[:, 0, :GROUP_HEADS].set(forget_b)
    gg = group_g.reshape(depth, N_MIXERS, GROUP_WIDTH)

    xt = x.reshape(t, d)
    for l in range(depth):
        parts = [m.reshape(batch, 1, d) for m in jnp.split(mod[l], 9, axis=-1)]
        sh1, sc1, g1, sh2, sc2, g2, sh3, sc3, g3 = parts
        xt = _ffn(xt, norm_g[l, 0:1], sh1, sc1, g1, wg, wu, wd, l, 0, seq)
        ng2 = norm_g[l, 1:2]
        p = _inproj(xt, ng2, sh2, sc2, w_main, gain, bd, l, seq, N_NORM_TILES, BF16)
        side = _inproj(xt, ng2, sh2, sc2, w_side, gain_side, bd_side, l, seq, 0, F32)
        o_a = _mixer_dsa(p, side, bias_dsa, tri_before, batch, seq)
        o_b = _mixer_swa(p, sinks[l], bias, batch, seq)
        cum = _fox_cum(side, fb_rows[l], tri_incl, batch, seq)
        o_c = _mixer_fox(p, cum, batch, seq)
        o_d = _mixer_stick(p, tri_after, batch, seq)
        xt = _outproj(xt, g2, (o_a, o_b, o_c, o_d), gg, wo, l, seq)
        xt = _ffn(xt, norm_g[l, 2:3], sh3, sc3, g3, wg, wu, wd, l, 1, seq)
    return xt.reshape(batch, seq, d)
```

```python
import functools
import math

import numpy as np
import jax
import jax.numpy as jnp
from jax import lax
from jax.experimental import pallas as pl
from jax.experimental.pallas import tpu as pltpu

F32 = jnp.float32
BF16 = jnp.bfloat16
I32 = jnp.int32

HEAD_DIM = 64
N_MIXERS = 4
GROUP_HEADS = 8
GROUP_WIDTH = GROUP_HEADS * HEAD_DIM
IDX_HEADS = 16
IDX_DIM = 64
TOPK_MAX = 256
WINDOW = 128
REL_BUCKETS = 32
REL_MAX_DIST = 128
EPS = 1e-6

LANES = 128
PAIRS = GROUP_HEADS // 2
NEG = -1e30
INT_MIN = -2 ** 31
STICK_EXIT = -110.0
VMEM_LIMIT = 56 * 1024 * 1024

PT = 512
T_AQ, T_BQ, T_CQ, T_CK, T_K2 = 0, 1, 2, 3, 4
N_NORM_TILES = 5
T_CV, T_CG, T_DQ, T_DK, T_DV, T_IQ, T_MISC = 5, 6, 7, 8, 9, 10, 12
N_TILES = 13
P_COLS = N_TILES * PT


def _cparams(sem, vmem=VMEM_LIMIT):
    return pltpu.CompilerParams(dimension_semantics=sem, vmem_limit_bytes=vmem)


def _nt_dot(a, b):
    return lax.dot_general(a, b, (((1,), (1,)), ((), ())), preferred_element_type=F32)


def _dot(a, b):
    return jnp.dot(a, b, preferred_element_type=F32)


def _sigmoid(x):
    return 1.0 / (1.0 + jnp.exp(-x))


def _lane_tile(x, n):
    return x if n == 1 else jnp.concatenate([x] * n, axis=1)


def _own_lane_mask(shape, h):
    lane = lax.broadcasted_iota(I32, shape, 1)
    return (lane < HEAD_DIM) if h % 2 == 0 else (lane >= HEAD_DIM)


def _mod_kernel(c_ref, w_ref, b_ref, o_ref):
    c = c_ref[...]
    cond = (c * _sigmoid(c)).astype(BF16)
    o_ref[0] = _dot(cond, w_ref[0].astype(BF16)) + b_ref[0]


def _modulation(c, w_ada, b_ada):
    depth, d, n = w_ada.shape
    b = c.shape[0]
    rows = 8
    tn = 1024
    c_pad = jnp.zeros((rows, d), F32).at[:b].set(c)
    out = pl.pallas_call(
        _mod_kernel,
        grid=(depth, n // tn),
        in_specs=[
            pl.BlockSpec((rows, d), lambda l, j: (0, 0)),
            pl.BlockSpec((1, d, tn), lambda l, j: (l, 0, j)),
            pl.BlockSpec((1, 1, tn), lambda l, j: (l, 0, j)),
        ],
        out_specs=pl.BlockSpec((1, rows, tn), lambda l, j: (l, 0, j)),
        out_shape=jax.ShapeDtypeStruct((depth, rows, n), F32),
        compiler_params=_cparams(("arbitrary", "arbitrary")),
        name="adaln_mod",
    )(c_pad, w_ada, b_ada.reshape(depth, 1, n))
    return out[:, :b]


def _norm_modulate(x, ng, sh, sc):
    ms = jnp.mean(x * x, axis=-1, keepdims=True)
    y = x * lax.rsqrt(ms + EPS) * ng
    return y * (1.0 + sc) + sh


def _ffn_kernel(x_ref, ng_ref, sh_ref, sc_ref, gt_ref, wg_ref, wu_ref, wd_ref, o_ref, h_ref, a_ref):
    f = pl.program_id(1)
    last = pl.num_programs(1) - 1

    def activation():
        h = h_ref[...]
        g = _dot(h, wg_ref[...])
        u = _dot(h, wu_ref[...])
        return (g * _sigmoid(g) * u).astype(BF16)

    @pl.when(f == 0)
    def _():
        h = _norm_modulate(x_ref[...], ng_ref[...], sh_ref[0], sc_ref[0])
        h_ref[...] = h.astype(BF16)
        o_ref[...] = jnp.zeros_like(o_ref)
        a_ref[...] = activation()

    @pl.when(jnp.logical_and(f > 0, f < last))
    def _():
        o_ref[...] += _dot(a_ref[...], wd_ref[...])
        a_ref[...] = activation()

    @pl.when(f == last)
    def _():
        acc = o_ref[...] + _dot(a_ref[...], wd_ref[...])
        o_ref[...] = x_ref[...] + 0.5 * gt_ref[0] * acc


def _ffn(x, ng, sh, sc, gt, wg, wu, wd, l, s, seq):
    t, d = x.shape
    tm, tf = min(1024, seq), 512
    nf = wg.shape[-1] // tf
    per_b = seq // tm
    vec = pl.BlockSpec((1, 1, d), lambda i, f: (i // per_b, 0, 0))
    up = lambda i, f: (l, s, 0, jnp.minimum(f, nf - 1))
    return pl.pallas_call(
        _ffn_kernel,
        grid=(t // tm, nf + 1),
        in_specs=[
            pl.BlockSpec((tm, d), lambda i, f: (i, 0), pipeline_mode=pl.Buffered(1)),
            pl.BlockSpec((1, d), lambda i, f: (0, 0)),
            vec, vec, vec,
            pl.BlockSpec((None, None, d, tf), up),
            pl.BlockSpec((None, None, d, tf), up),
            pl.BlockSpec((None, None, tf, d), lambda i, f: (l, s, jnp.maximum(f - 1, 0), 0)),
        ],
        out_specs=pl.BlockSpec((tm, d), lambda i, f: (i, 0)),
        out_shape=jax.ShapeDtypeStruct((t, d), F32),
        scratch_shapes=[pltpu.VMEM((tm, d), BF16), pltpu.VMEM((tm, tf), BF16)],
        compiler_params=_cparams(("arbitrary", "arbitrary")),
        name="ffn",
    )(x, ng, sh, sc, gt, wg, wu, wd)


def _inproj_kernel(x_ref, ng_ref, sh_ref, sc_ref, w_ref, gain_ref, bd_ref, o_ref, h_ref, *, n_norm):
    j = pl.program_id(1)

    @pl.when(j == 0)
    def _():
        h = _norm_modulate(x_ref[...], ng_ref[...], sh_ref[0], sc_ref[0])
        h_ref[...] = h.astype(BF16)

    y = _dot(h_ref[...], w_ref[...])

    @pl.when(j < n_norm)
    def _():
        sq = y * y
        hi = sq.astype(BF16)
        lo = (sq - hi.astype(F32)).astype(BF16)
        ms = _dot(hi, bd_ref[...]) + _dot(lo, bd_ref[...])
        o_ref[...] = (y * lax.rsqrt(ms + EPS) * gain_ref[...]).astype(o_ref.dtype)

    @pl.when(j >= n_norm)
    def _():
        o_ref[...] = y.astype(o_ref.dtype)


def _inproj(x, ng, sh, sc, w, gain, bd, l, seq, n_norm, out_dtype):
    t, d = x.shape
    n = w.shape[-1]
    tm = min(1024, seq)
    tn = bd.shape[0]
    per_b = seq // tm
    vec = pl.BlockSpec((1, 1, d), lambda i, j: (i // per_b, 0, 0))
    return pl.pallas_call(
        functools.partial(_inproj_kernel, n_norm=n_norm),
        grid=(t // tm, n // tn),
        in_specs=[
            pl.BlockSpec((tm, d), lambda i, j: (i, 0)),
            pl.BlockSpec((1, d), lambda i, j: (0, 0)),
            vec, vec,
            pl.BlockSpec((None, d, tn), lambda i, j: (l, 0, j)),
            pl.BlockSpec((None, 1, tn), lambda i, j: (l, 0, j)),
            pl.BlockSpec((tn, tn), lambda i, j: (0, 0)),
        ],
        out_specs=pl.BlockSpec((tm, tn), lambda i, j: (i, j)),
        out_shape=jax.ShapeDtypeStruct((t, n), out_dtype),
        scratch_shapes=[pltpu.VMEM((tm, d), BF16)],
        compiler_params=_cparams(("arbitrary", "arbitrary")),
        name="inproj",
    )(x, ng, sh, sc, w, gain, bd)


def _rel_bucket_np(dist):
    n = np.maximum(dist, 0)
    max_exact = REL_BUCKETS // 2
    nf = np.maximum(n, 1).astype(np.float32)
    large = max_exact + (np.log(nf / np.float32(max_exact)) / np.float32(math.log(REL_MAX_DIST / max_exact))
                         * np.float32(REL_BUCKETS - max_exact)).astype(np.int32)
    large = np.minimum(large, REL_BUCKETS - 1)
    return np.where(n < max_exact, n, large).astype(np.int32)


def _bias_kernel(tab_ref, bkt_ref, o_ref, *, sub_far):
    h = pl.program_id(0)
    far = tab_ref[REL_BUCKETS - 1, h]
    for which in range(2):
        bkt = bkt_ref[which]
        acc = jnp.zeros(bkt.shape, F32)
        for b in range(REL_BUCKETS):
            acc = jnp.where(bkt == b, tab_ref[b, h], acc)
        o_ref[0, which] = acc - jnp.where(h < sub_far, far, 0.0)


def _bias_tiles(rel_table, tq):
    t_l = np.arange(tq)[:, None]
    s_l = np.arange(LANES)[None, :]
    bkt = np.stack([_rel_bucket_np(t_l + LANES - s_l), _rel_bucket_np(t_l - s_l)]).astype(np.int32)
    nh = rel_table.shape[1]
    return pl.pallas_call(
        functools.partial(_bias_kernel, sub_far=GROUP_HEADS),
        grid=(nh,),
        in_specs=[
            pl.BlockSpec(memory_space=pltpu.SMEM),
            pl.BlockSpec((2, tq, LANES), lambda h: (0, 0, 0)),
        ],
        out_specs=pl.BlockSpec((1, 2, tq, LANES), lambda h: (h, 0, 0, 0)),
        out_shape=jax.ShapeDtypeStruct((nh, 2, tq, LANES), F32),
        compiler_params=_cparams(("arbitrary",)),
        name="rel_bias_tiles",
    )(rel_table, jnp.asarray(bkt))


def _dsa_window_bias(bias):
    prev, cur = bias[:GROUP_HEADS, 0], bias[:GROUP_HEADS, 1]
    return jnp.stack([jnp.concatenate([cur, cur], axis=-1), jnp.concatenate([prev, cur], axis=-1)])


def _softmax_step(s, v, m_ref, l_ref, acc_ref, idx):
    reps = s.shape[1] // LANES
    m_prev = m_ref[idx]
    l_prev = l_ref[idx]
    m_cur = jnp.max(s, axis=1, keepdims=True)
    m_new = jnp.maximum(m_prev, m_cur)
    p = jnp.exp(s - _lane_tile(m_new, reps))
    alpha = jnp.exp(m_prev - m_new)
    l_ref[idx] = alpha * l_prev + jnp.sum(p, axis=1, keepdims=True)
    m_ref[idx] = m_new
    acc_ref[idx] = alpha * acc_ref[idx] + _dot(p.astype(BF16), v)


def _softmax_step_stacked(s, v, m_ref, l_ref, acc_ref):
    nh, rows, width = s.shape
    reps = width // LANES
    m_prev = m_ref[...]
    m_new = jnp.maximum(m_prev, jnp.max(s, axis=2, keepdims=True))
    p = jnp.exp(s - jnp.concatenate([m_new] * reps, axis=2))
    alpha = jnp.exp(m_prev - m_new)
    l_ref[...] = alpha * l_ref[...] + jnp.sum(p, axis=2, keepdims=True)
    m_ref[...] = m_new
    pv = _dot(p.astype(BF16).reshape(nh * rows, width), v).reshape(nh, rows, LANES)
    acc_ref[...] = alpha * acc_ref[...] + pv


def _store_pairs(o_ref, per_head):
    for j in range(PAIRS):
        lane = lax.broadcasted_iota(I32, per_head[0].shape, 1)
        o_ref[:, j * LANES:(j + 1) * LANES] = jnp.where(
            lane < HEAD_DIM, per_head[2 * j], per_head[2 * j + 1]).astype(o_ref.dtype)


DSA_TQ = 128
DSA_KC1 = 512
DSA_KC = 512
DSA_SUB = DSA_KC // LANES
DSA_VALUE_STEPS = 26
DSA_VALUE_BLIND_STEPS = 10


def _dsa_kernel(q_ref, iq_ref, iw_ref, k2_ref, v2_ref, ik2_ref, bias_ref, tri_ref, o_ref,
                key_ref, wb_ref, iqm_ref, qm_ref, t_ref, m_ref, l_ref, acc_ref, *, k_top):
    i = pl.program_id(1)
    tq = DSA_TQ
    row0 = i * tq
    n_valid_cols = row0 + tq

    iw = iw_ref[...]
    for h in range(IDX_HEADS):
        col = GROUP_HEADS + h
        wb_ref[h] = jnp.broadcast_to(iw[:, col:col + 1], (tq, LANES))
        blk = iq_ref[:, (h // 2) * LANES:(h // 2 + 1) * LANES]
        iqm_ref[h] = jnp.where(_own_lane_mask(blk.shape, h), blk, jnp.zeros_like(blk))
    for h in range(GROUP_HEADS):
        blk = q_ref[:, (h // 2) * LANES:(h // 2 + 1) * LANES]
        qm_ref[h] = jnp.where(_own_lane_mask(blk.shape, h), blk, jnp.zeros_like(blk))

    qpos = row0 + lax.broadcasted_iota(I32, (tq, DSA_KC1), 0)
    iq_all = iqm_ref[...].reshape(IDX_HEADS * tq, LANES)
    wb_all = jnp.concatenate([wb_ref[...]] * (DSA_KC1 // LANES), axis=2)
    n1 =(n_valid_cols + DSA_KC1 - 1) // DSA_KC1

    def score_chunk(c, carry):
        start = pl.multiple_of(c * DSA_KC1, DSA_KC1)
        ik = ik2_ref[pl.ds(start, DSA_KC1), :]
        r = _nt_dot(iq_all, ik).reshape(IDX_HEADS, tq, DSA_KC1)
        score = jnp.sum(jnp.maximum(r, 0.0) * wb_all, axis=0)
        bits = lax.bitcast_convert_type(score, I32)
        key = bits ^ ((bits >> 31) & 0x7FFFFFFF)
        key = jnp.where(score == 0.0, 0, key)
        kpos = start + lax.broadcasted_iota(I32, (tq, DSA_KC1), 1)
        valid = kpos <= qpos
        key = jnp.where(valid, key, INT_MIN)
        smax, smin = carry
        hi = jnp.where(valid, score, -jnp.inf)
        lo = jnp.where(valid, score, jnp.inf)
        for u in range(DSA_KC1 // LANES):
            key_ref[c * (DSA_KC1 // LANES) + u] = key[:, u * LANES:(u + 1) * LANES]
            smax = jnp.maximum(smax, hi[:, u * LANES:(u + 1) * LANES])
            smin = jnp.minimum(smin, lo[:, u * LANES:(u + 1) * LANES])
        return smax, smin

    smax, smin = lax.fori_loop(0, n1, score_chunk, (jnp.full((tq, LANES), -jnp.inf, F32),
                                                    jnp.full((tq, LANES), jnp.inf, F32)))
    nsel = (n_valid_cols + DSA_KC - 1) // DSA_KC
    n1_blocks = n1 * (DSA_KC1 // LANES)

    @pl.when(n1_blocks < nsel * DSA_SUB)
    def _():
        for u in range(DSA_KC1 // LANES):
            key_ref[n1_blocks + u] = jnp.full((tq, LANES), INT_MIN, I32)

    def load_keys(c):
        blk = key_ref[pl.ds(c * DSA_SUB, DSA_SUB)]
        return jnp.concatenate([blk[u] for u in range(DSA_SUB)], axis=1)

    def count_ge(thr, strict):
        def body(c, acc):
            blk = key_ref[pl.ds(c * DSA_SUB, DSA_SUB)]
            for u in range(DSA_SUB):
                hit = (blk[u] > thr) if strict else (blk[u] >= thr)
                acc = jnp.where(hit, acc + 1.0, acc)
            return acc
        acc = lax.fori_loop(0, nsel, body, jnp.zeros((tq, LANES), F32))
        return jnp.broadcast_to(jnp.sum(acc, axis=1, keepdims=True), (tq, LANES))

    t_ref[...] = jnp.full((tq, LANES), INT_MIN, I32)
    kf = float(k_top)

    def float_key(x):
        bits = lax.bitcast_convert_type(x, I32)
        return jnp.where(x == 0.0, 0, bits ^ ((bits >> 31) & 0x7FFFFFFF))

    def bisect_values():
        lower = jnp.broadcast_to(jnp.min(smin, axis=1, keepdims=True), (tq, LANES))
        upper = jnp.broadcast_to(jnp.max(smax, axis=1, keepdims=True), (tq, LANES))
        n_causal = (row0 + 1 + lax.broadcasted_iota(I32, (tq, LANES), 0)).astype(F32)
        keep_all = n_causal <= kf
        open0 = jnp.where(keep_all, 0.0, 1.0)

        def step(lower, upper, still_open):
            mid = 0.5 * lower + 0.5 * upper
            cnt = count_ge(float_key(mid), False)
            active = still_open > 0.0
            go_up = jnp.logical_and(active, cnt >= kf)
            go_down = jnp.logical_and(active, cnt < kf)
            lower = jnp.where(go_up, mid, lower)
            upper = jnp.where(go_down, mid, upper)
            still_open = jnp.where(jnp.logical_and(go_up, cnt == kf), 0.0, still_open)
            return lower, upper, still_open

        lower, upper, still_open = lax.fori_loop(
            0, DSA_VALUE_BLIND_STEPS, lambda _, c: step(*c), (lower, upper, open0))

        def cond(carry):
            it, _, _, still_open = carry
            return jnp.logical_and(it < DSA_VALUE_STEPS, jnp.max(still_open) > 0.0)

        def body(carry):
            it, lower, upper, still_open = carry
            return (it + 1,) + step(lower, upper, still_open)

        _, lower, _, still_open = lax.while_loop(
            cond, body, (jnp.int32(DSA_VALUE_BLIND_STEPS), lower, upper, still_open))
        return jnp.where(keep_all, INT_MIN, float_key(lower)), jnp.max(still_open) == 0.0

    def exact_select():
        def bit_step(it, carry):
            thr, cnt_thr = carry
            cand = thr + jnp.left_shift(jnp.int32(1), 31 - it)
            cnt = count_ge(cand, False)
            take = cnt >= kf
            return jnp.where(take, cand, thr), jnp.where(take, cnt, cnt_thr)

        thr0 = jnp.full((tq, LANES), INT_MIN, I32)
        cnt0 = jnp.full((tq, LANES), float(2 ** 24), F32)
        thr, cnt_thr = lax.fori_loop(0, 32, bit_step, (thr0, cnt0))
        t_ref[...] = thr

        excess = jnp.where(thr > INT_MIN, cnt_thr - kf, 0.0)

        @pl.when(jnp.max(excess) > 0.0)
        def _():
            need = kf - count_ge(thr, True)

            def tie_chunk(c, seen):
                keys = load_keys(c)
                eq = keys == _lane_tile(thr, DSA_SUB)
                eqf = jnp.where(eq, 1.0, 0.0)
                rank = _dot(eqf.astype(BF16), tri_ref[...]) + _lane_tile(seen, DSA_SUB)
                dropped = jnp.where(rank >= _lane_tile(need, DSA_SUB), INT_MIN, keys)
                keys = jnp.where(eq, dropped, keys)
                for u in range(DSA_SUB):
                    key_ref[c * DSA_SUB + u] = keys[:, u * LANES:(u + 1) * LANES]
                return seen + jnp.broadcast_to(jnp.sum(eqf, axis=1, keepdims=True), (tq, LANES))

            lax.fori_loop(0, nsel, tie_chunk, jnp.zeros((tq, LANES), F32))

    @pl.when(n_valid_cols > k_top)
    def _():
        thr_fast, finished = bisect_values()
        t_ref[...] = thr_fast

        @pl.when(jnp.logical_not(finished))
        def _():
            exact_select()

    thr = jnp.maximum(t_ref[...], INT_MIN + 1)

    m_ref[...] = jnp.full(m_ref.shape, NEG, F32)
    l_ref[...] = jnp.zeros(l_ref.shape, F32)
    acc_ref[...] = jnp.zeros(acc_ref.shape, F32)
    q_all = qm_ref[...].reshape(GROUP_HEADS * tq, LANES)

    def attend(k_blk, v_blk, madd, bias):
        width = k_blk.shape[0]
        s = _nt_dot(q_all, k_blk).reshape(GROUP_HEADS, tq, width) + madd[None]
        if bias is not None:
            s = s + bias
        _softmax_step_stacked(s, v_blk, m_ref, l_ref, acc_ref)

    win_blk = jnp.maximum(i - 1, 0)
    win_start = pl.multiple_of(win_blk * LANES, LANES)
    near_keys = key_ref[pl.ds(win_blk, 2)]
    near_keys = jnp.concatenate([near_keys[0], near_keys[1]], axis=1)
    near_madd = jnp.where(near_keys >= _lane_tile(thr, 2), 0.0, NEG)
    attend(k2_ref[pl.ds(win_start, 2 * LANES), :], v2_ref[pl.ds(win_start, 2 * LANES), :],
           near_madd, bias_ref[jnp.minimum(i, 1)])
    for u in range(2):
        key_ref[win_blk + u] = jnp.full((tq, LANES), INT_MIN, I32)

    n_far = (win_blk * LANES + DSA_KC - 1) // DSA_KC

    def far_chunk(c, carry):
        start = pl.multiple_of(c * DSA_KC, DSA_KC)
        madd = jnp.where(load_keys(c) >= _lane_tile(thr, DSA_SUB), 0.0, NEG)
        attend(k2_ref[pl.ds(start, DSA_KC), :], v2_ref[pl.ds(start, DSA_KC), :], madd, None)
        return carry

    lax.fori_loop(0, n_far, far_chunk, 0)
    _store_pairs(o_ref, [acc_ref[h] / l_ref[h] for h in range(GROUP_HEADS)])


def _mixer_dsa(p, side, bias, tri, batch, seq):
    t = p.shape[0]
    tq = DSA_TQ
    nq = seq // tq
    k_top = min(TOPK_MAX, seq // 4)
    cw = PT // LANES
    return pl.pallas_call(
        functools.partial(_dsa_kernel, k_top=k_top),
        grid=(batch, nq),
        in_specs=[
            pl.BlockSpec((tq, PT), lambda b, i: (b * nq + i, T_AQ)),
            pl.BlockSpec((tq, 2 * PT), lambda b, i: (b * nq + i, T_IQ // 2)),
            pl.BlockSpec((tq, LANES), lambda b, i: (b * nq + i, 0)),
            pl.BlockSpec((seq, LANES), lambda b, i: (b, T_K2 * cw)),
            pl.BlockSpec((seq, LANES), lambda b, i: (b, T_MISC * cw)),
            pl.BlockSpec((seq, LANES), lambda b, i: (b, T_MISC * cw + 2)),
            pl.BlockSpec((2, GROUP_HEADS, tq, 2 * LANES), lambda b, i: (0, 0, 0, 0)),
            pl.BlockSpec((DSA_KC, DSA_KC), lambda b, i: (0, 0)),
        ],
        out_specs=pl.BlockSpec((tq, GROUP_WIDTH), lambda b, i: (b * nq + i, 0)),
        out_shape=jax.ShapeDtypeStruct((t, GROUP_WIDTH), F32),
        scratch_shapes=[
            pltpu.VMEM((max(seq // LANES, DSA_SUB), tq, LANES), I32),
            pltpu.VMEM((IDX_HEADS, tq, LANES), F32),
            pltpu.VMEM((IDX_HEADS, tq, LANES), BF16),
            pltpu.VMEM((GROUP_HEADS, tq, LANES), BF16),
            pltpu.VMEM((tq, LANES), I32),
            pltpu.VMEM((GROUP_HEADS, tq, LANES), F32),
            pltpu.VMEM((GROUP_HEADS, tq, LANES), F32),
            pltpu.VMEM((GROUP_HEADS, tq, LANES), F32),
        ],
        compiler_params=_cparams(("arbitrary", "arbitrary")),
        name="mixer_dsa",
    )(p, p, side, p, p, p, bias, tri)


SWA_TQ = 128


def _swa_kernel(sink_ref, q_ref, kp_ref, kc_ref, vp_ref, vc_ref, bias_ref, o_ref):
    i = pl.program_id(1)
    tq = SWA_TQ
    t_l = lax.broadcasted_iota(I32, (tq, LANES), 0)
    s_l = lax.broadcasted_iota(I32, (tq, LANES), 1)
    ok_prev = s_l > t_l + jnp.where(i > 0, 0, LANES)
    ok_cur = s_l <= t_l
    kp, kc, vp, vc = kp_ref[...], kc_ref[...], vp_ref[...], vc_ref[...]
    outs = []
    for h in range(GROUP_HEADS):
        blk = q_ref[:, (h // 2) * LANES:(h // 2 + 1) * LANES]
        qm = jnp.where(_own_lane_mask(blk.shape, h), blk, jnp.zeros_like(blk))
        s_p = jnp.where(ok_prev, _nt_dot(qm, kp) + bias_ref[h, 0], NEG)
        s_c = jnp.where(ok_cur, _nt_dot(qm, kc) + bias_ref[h, 1], NEG)
        sink = sink_ref[h]
        m = jnp.maximum(jnp.max(jnp.maximum(s_p, s_c), axis=1, keepdims=True), sink)
        p_p = jnp.exp(s_p - m)
        p_c = jnp.exp(s_c - m)
        l = jnp.sum(p_p + p_c, axis=1, keepdims=True) + jnp.exp(sink - m)
        outs.append((_dot(p_p.astype(BF16), vp) + _dot(p_c.astype(BF16), vc)) / l)
    _store_pairs(o_ref, outs)


def _mixer_swa(p, sinks, bias, batch, seq):
    t = p.shape[0]
    tq = SWA_TQ
    nq = seq // tq
    cw = PT // LANES
    cur = lambda col: (lambda b, i: (b * nq + i, col))
    prev = lambda col: (lambda b, i: (b * nq + jnp.maximum(i - 1, 0), col))
    return pl.pallas_call(
        _swa_kernel,
        grid=(batch, nq),
        in_specs=[
            pl.BlockSpec(memory_space=pltpu.SMEM),
            pl.BlockSpec((tq, PT), lambda b, i: (b * nq + i, T_BQ)),
            pl.BlockSpec((tq, LANES), prev(T_K2 * cw + 1)),
            pl.BlockSpec((tq, LANES), cur(T_K2 * cw + 1)),
            pl.BlockSpec((tq, LANES), prev(T_MISC * cw + 1)),
            pl.BlockSpec((tq, LANES), cur(T_MISC * cw + 1)),
            pl.BlockSpec((GROUP_HEADS, 2, tq, LANES), lambda b, i: (1, 0, 0, 0)),
        ],
        out_specs=pl.BlockSpec((tq, GROUP_WIDTH), lambda b, i: (b * nq + i, 0)),
        out_shape=jax.ShapeDtypeStruct((t, GROUP_WIDTH), F32),
        compiler_params=_cparams(("arbitrary", "arbitrary")),
        name="mixer_swa",
    )(sinks, p, p, p, p, p, bias)


def _logsig(x):
    return jnp.minimum(x, 0.0) - jnp.log(1.0 + jnp.exp(-jnp.abs(x)))


def _foxcum_kernel(fb_ref, f_ref, tri_ref, o_ref, *, nchunk):
    def body(c, carry):
        start = pl.multiple_of(c * LANES, LANES)
        lf = _logsig(f_ref[pl.ds(start, LANES), :] + fb_ref[...]).T
        p1 = lf.astype(BF16)
        r1 = lf - p1.astype(F32)
        p2 = r1.astype(BF16)
        p3 = (r1 - p2.astype(F32)).astype(BF16)
        tri = tri_ref[...]
        cum = _dot(p1, tri) + _dot(p2, tri) + _dot(p3, tri) + carry
        o_ref[0, c] = cum[:GROUP_HEADS]
        return jnp.broadcast_to(cum[:, LANES - 1:LANES], cum.shape)

    lax.fori_loop(0, nchunk, body, jnp.zeros((LANES, LANES), F32))


def _fox_cum(side, fb_row, tri_incl, batch, seq):
    nchunk = seq // LANES
    return pl.pallas_call(
        functools.partial(_foxcum_kernel, nchunk=nchunk),
        grid=(batch,),
        in_specs=[
            pl.BlockSpec((1, LANES), lambda b: (0, 0)),
            pl.BlockSpec((seq, LANES), lambda b: (b, 0)),
            pl.BlockSpec((LANES, LANES), lambda b: (0, 0)),
        ],
        out_specs=pl.BlockSpec((1, nchunk, GROUP_HEADS, LANES), lambda b: (b, 0, 0, 0)),
        out_shape=jax.ShapeDtypeStruct((batch, nchunk, GROUP_HEADS, LANES), F32),
        compiler_params=_cparams(("arbitrary",)),
        name="fox_cumsum",
    )(fb_row, side, tri_incl)


FOX_TQ = 256
FOX_KC = 512
FOX_SUB = FOX_KC // LANES


def _fox_kernel(q_ref, g_ref, k_ref, v_ref, cum_ref, o_ref, qm_ref, m_ref, l_ref, acc_ref):
    i = pl.program_id(1)
    tq = FOX_TQ
    for h in range(GROUP_HEADS):
        blk = q_ref[:, (h // 2) * LANES:(h // 2 + 1) * LANES]
        qm_ref[h] = jnp.where(_own_lane_mask(blk.shape, h), blk, jnp.zeros_like(blk))
    m_ref[...] = jnp.full(m_ref.shape, NEG, F32)
    l_ref[...] = jnp.zeros(l_ref.shape, F32)
    acc_ref[...] = jnp.zeros(acc_ref.shape, F32)
    f_ref0 = cum_ref[0, i * (tq // LANES)][:, 0:1]

    def chunk(c, masked):
        start = pl.multiple_of(c * FOX_KC, FOX_KC)
        cum = cum_ref[0, pl.ds(c * FOX_SUB, FOX_SUB)]
        cum = jnp.concatenate([cum[u] for u in range(FOX_SUB)], axis=1)
        fbias = f_ref0 - cum
        per_head = []
        for pr in range(PAIRS):
            pair = slice(pr * LANES, (pr + 1) * LANES)
            q_pair = qm_ref[2 * pr:2 * pr + 2].reshape(2 * tq, LANES)
            s_pair = _nt_dot(q_pair, k_ref[pl.ds(start, FOX_KC), pair])
            for e in range(2):
                h = 2 * pr + e
                per_head.append(s_pair[e * tq:(e + 1) * tq] + fbias[h:h + 1, :])
        s = jnp.stack(per_head)
        if masked:
            shape = (GROUP_HEADS, tq, FOX_KC)
            ok = start + lax.broadcasted_iota(I32, shape, 2) <= i * tq + lax.broadcasted_iota(I32, shape, 1)
            s = jnp.where(ok, s, NEG)
        reps = FOX_KC // LANES
        m_prev = m_ref[...]
        m_new = jnp.maximum(m_prev, jnp.max(s, axis=2, keepdims=True))
        p = jnp.exp(s - jnp.concatenate([m_new] * reps, axis=2))
        alpha = jnp.exp(m_prev - m_new)
        l_ref[...] = alpha * l_ref[...] + jnp.sum(p, axis=2, keepdims=True)
        m_ref[...] = m_new
        p = p.astype(BF16)
        pvs = []
        for pr in range(PAIRS):
            p_pair = p[2 * pr:2 * pr + 2].reshape(2 * tq, FOX_KC)
            pvs.append(_dot(p_pair, v_ref[pl.ds(start, FOX_KC), pr * LANES:(pr + 1) * LANES]))
        acc_ref[...] = alpha * acc_ref[...] + jnp.concatenate(pvs, axis=0).reshape(GROUP_HEADS, tq, LANES)

    def body(c, carry):
        chunk(c, False)
        return carry

    n_full = (i * tq) // FOX_KC
    lax.fori_loop(0, n_full, body, 0)
    chunk(n_full, True)
    outs = []
    for h in range(GROUP_HEADS):
        gate = _sigmoid(g_ref[:, (h // 2) * LANES:(h // 2 + 1) * LANES].astype(F32))
        outs.append(acc_ref[h] / l_ref[h] * gate)
    _store_pairs(o_ref, outs)


def _mixer_fox(p, cum, batch, seq):
    t = p.shape[0]
    tq = FOX_TQ
    nq = seq // tq
    return pl.pallas_call(
        _fox_kernel,
        grid=(batch, nq),
        in_specs=[
            pl.BlockSpec((tq, PT), lambda b, i: (b * nq + i, T_CQ)),
            pl.BlockSpec((tq, PT), lambda b, i: (b * nq + i, T_CG)),
            pl.BlockSpec((seq, PT), lambda b, i: (b, T_CK)),
            pl.BlockSpec((seq, PT), lambda b, i: (b, T_CV)),
            pl.BlockSpec((1, seq // LANES, GROUP_HEADS, LANES), lambda b, i: (b, 0, 0, 0)),
        ],
        out_specs=pl.BlockSpec((tq, GROUP_WIDTH), lambda b, i: (b * nq + i, 0)),
        out_shape=jax.ShapeDtypeStruct((t, GROUP_WIDTH), F32),
        scratch_shapes=[
            pltpu.VMEM((GROUP_HEADS, tq, LANES), BF16),
            pltpu.VMEM((GROUP_HEADS, tq, LANES), F32),
            pltpu.VMEM((GROUP_HEADS, tq, LANES), F32),
            pltpu.VMEM((GROUP_HEADS, tq, LANES), F32),
        ],
        compiler_params=_cparams(("arbitrary", "arbitrary")),
        name="mixer_fox",
    )(p, p, p, p, cum)


STK_TQ = 128


def _stick_kernel(q_ref, k_ref, v_ref, tri_ref, o_ref, qm_ref, acc_ref, r_ref):
    i = pl.program_id(1)
    tq = STK_TQ
    nh = GROUP_HEADS
    for h in range(nh):
        blk = q_ref[:, (h // 2) * LANES:(h // 2 + 1) * LANES]
        qm_ref[h] = jnp.where(_own_lane_mask(blk.shape, h), blk, jnp.zeros_like(blk))
    acc_ref[...] = jnp.zeros(acc_ref.shape, F32)
    r_ref[...] = jnp.zeros(r_ref.shape, F32)
    before = (lax.broadcasted_iota(I32, (nh, tq, LANES), 2) < lax.broadcasted_iota(I32, (nh, tq, LANES), 1))
    tri = tri_ref[...]

    def block(j, masked):
        start = pl.multiple_of(j * LANES, LANES)
        zs = []
        for pr in range(PAIRS):
            q_pair = qm_ref[2 * pr:2 * pr + 2].reshape(2 * tq, LANES)
            zs.append(_nt_dot(q_pair, k_ref[pl.ds(start, LANES), pr * LANES:(pr + 1) * LANES]))
        z = jnp.concatenate(zs, axis=0).reshape(nh, tq, LANES)
        lsz = _logsig(z)
        u = lsz - z
        if masked:
            u = jnp.where(before, u, 0.0)
        u2 = u.reshape(nh * tq, LANES)
        u_hi = u2.astype(BF16)
        u_lo = (u2 - u_hi.astype(F32)).astype(BF16)
        nearer = (_dot(u_hi, tri) + _dot(u_lo, tri)).reshape(nh, tq, LANES)
        run = r_ref[...]
        w = jnp.exp(lsz + nearer + run)
        if masked:
            w = jnp.where(before, w, 0.0)
        wb = w.astype(BF16)
        pvs = []
        for pr in range(PAIRS):
            w_pair = wb[2 * pr:2 * pr + 2].reshape(2 * tq, LANES)
            pvs.append(_dot(w_pair, v_ref[pl.ds(start, LANES), pr * LANES:(pr + 1) * LANES]))
        acc_ref[...] += jnp.concatenate(pvs, axis=0).reshape(nh, tq, LANES)
        run = run + jnp.sum(u, axis=2, keepdims=True)
        r_ref[...] = run
        return jnp.max(run)

    rmax = block(i, True)

    def cond(carry):
        j, rmax = carry
        return jnp.logical_and(j >= 0, rmax >= STICK_EXIT)

    def body(carry):
        j, _ = carry
        return j - 1, block(j, False)

    lax.while_loop(cond, body, (i - 1, rmax))
    _store_pairs(o_ref, [acc_ref[h] for h in range(nh)])


def _mixer_stick(p, tri_excl, batch, seq):
    t = p.shape[0]
    tq = STK_TQ
    nq = seq // tq
    return pl.pallas_call(
        _stick_kernel,
        grid=(batch, nq),
        in_specs=[
            pl.BlockSpec((tq, PT), lambda b, i: (b * nq + i, T_DQ)),
            pl.BlockSpec((seq, PT), lambda b, i: (b, T_DK)),
            pl.BlockSpec((seq, PT), lambda b, i: (b, T_DV)),
            pl.BlockSpec((LANES, LANES), lambda b, i: (0, 0)),
        ],
        out_specs=pl.BlockSpec((tq, GROUP_WIDTH), lambda b, i: (b * nq + i, 0)),
        out_shape=jax.ShapeDtypeStruct((t, GROUP_WIDTH), F32),
        scratch_shapes=[pltpu.VMEM((GROUP_HEADS, tq, LANES), BF16),
                        pltpu.VMEM((GROUP_HEADS, tq, LANES), F32),
                        pltpu.VMEM((GROUP_HEADS, tq, LANES), F32)],
        compiler_params=_cparams(("arbitrary", "arbitrary")),
        name="mixer_stick",
    )(p, p, p, tri_excl)


def _outproj_kernel(x_ref, gt_ref, oa_ref, ob_ref, oc_ref, od_ref, gg_ref, w_ref, o_ref):
    acc = None
    for m, ref in enumerate((oa_ref, ob_ref, oc_ref, od_ref)):
        o = ref[...]
        y = o * lax.rsqrt(jnp.mean(o * o, axis=-1, keepdims=True) + EPS) * gg_ref[m:m + 1, :]
        part = _dot(y.astype(BF16), w_ref[m * GROUP_WIDTH:(m + 1) * GROUP_WIDTH, :])
        acc = part if acc is None else acc + part
    o_ref[...] = x_ref[...] + gt_ref[0] * acc


def _outproj(x, gt, outs, gg, w, l, seq):
    t, d = x.shape
    tm = 512
    per_b = seq // tm
    mix = pl.BlockSpec((tm, GROUP_WIDTH), lambda i: (i, 0))
    return pl.pallas_call(
        _outproj_kernel,
        grid=(t // tm,),
        in_specs=[
            pl.BlockSpec((tm, d), lambda i: (i, 0)),
            pl.BlockSpec((1, 1, d), lambda i: (i // per_b, 0, 0)),
            mix, mix, mix, mix,
            pl.BlockSpec((None, N_MIXERS, GROUP_WIDTH), lambda i: (l, 0, 0)),
            pl.BlockSpec((None, N_MIXERS * GROUP_WIDTH, d), lambda i: (l, 0, 0)),
        ],
        out_specs=pl.BlockSpec((tm, d), lambda i: (i, 0)),
        out_shape=jax.ShapeDtypeStruct((t, d), F32),
        compiler_params=_cparams(("arbitrary",)),
        name="outproj",
    )(x, gt, *outs, gg, w)


def _prep_in_weights(w_in, qk_g):
    depth, d, _ = w_in.shape
    sizes = (GROUP_WIDTH, HEAD_DIM, HEAD_DIM, IDX_HEADS * IDX_DIM, IDX_DIM, IDX_HEADS,
             GROUP_WIDTH, HEAD_DIM, HEAD_DIM,
             GROUP_WIDTH, GROUP_WIDTH, GROUP_WIDTH, GROUP_HEADS, GROUP_WIDTH,
             GROUP_WIDTH, GROUP_WIDTH, GROUP_WIDTH)
    pts = np.cumsum(sizes)[:-1].tolist()
    (a_q, a_k, a_v, a_iq, a_ik, a_iw, b_q, b_k, b_v,
     c_q, c_k, c_v, c_f, c_g, d_q, d_k, d_v) = jnp.split(w_in, pts, axis=-1)
    z = lambda n: jnp.zeros((depth, d, n), w_in.dtype)
    qscale = HEAD_DIM ** -0.5
    main = jnp.concatenate([
        a_q, b_q, c_q, c_k,
        a_k, a_k, b_k, b_k, z(2 * LANES),
        c_v, c_g, d_q * qscale, d_k, d_v, a_iq * (IDX_DIM ** -0.5),
        a_v, a_v, b_v, b_v, a_ik, a_ik, z(LANES),
    ], axis=-1).astype(BF16)
    side = jnp.concatenate([c_f, a_iw * (IDX_HEADS ** -0.5), z(LANES - GROUP_HEADS - IDX_HEADS)],
                           axis=-1).astype(BF16)
    rep = lambda g, n: jnp.tile(g, (1, n))
    gain = jnp.concatenate([
        rep(qk_g[:, 0], 8) * qscale, rep(qk_g[:, 2], 8) * qscale, rep(qk_g[:, 4], 8) * qscale,
        rep(qk_g[:, 5], 8), rep(qk_g[:, 1], 2), rep(qk_g[:, 3], 2),
        jnp.zeros((depth, P_COLS - 4 * PT - 2 * LANES), F32),
    ], axis=-1).reshape(depth, 1, P_COLS)
    return main, side, gain


def kernel(x, c, w_ada, b_ada, norm_g, w_in, qk_g, forget_b, sinks, rel_table, group_g, w_out,
           w_ffn_gate, w_ffn_up, w_ffn_down):
    batch, seq, d = x.shape
    depth = w_ada.shape[0]
    t = batch * seq

    mod = _modulation(c, w_ada, b_ada)
    wg = w_ffn_gate.astype(BF16)
    wu = w_ffn_up.astype(BF16)
    wd = w_ffn_down.astype(BF16)
    wo = w_out.astype(BF16)
    w_main, w_side, gain = _prep_in_weights(w_in, qk_g)
    gain_side = jnp.zeros((depth, 1, LANES), F32)

    r = np.arange(PT)
    bd = jnp.asarray((r[:, None] // HEAD_DIM == r[None, :] // HEAD_DIM).astype(np.float32) / HEAD_DIM, BF16)
    bd_side = jnp.zeros((LANES, LANES), BF16)
    r = np.arange(LANES)
    tri_incl = jnp.asarray(r[:, None] <= r[None, :], BF16)
    tri_after = jnp.asarray(r[:, None] > r[None, :], BF16)
    r = np.arange(DSA_KC)
    tri_before = jnp.asarray(r[:, None] < r[None, :], BF16)

    bias = _bias_tiles(rel_table, DSA_TQ)
    bias_dsa = _dsa_window_bias(bias)
    fb_rows = jnp.zeros((depth, 1, LANES), F32).at[:, 0, :GROUP_HEADS].set(forget_b)
    gg = group_g.reshape(depth, N_MIXERS, GROUP_WIDTH)

    xt = x.reshape(t, d)
    for l in range(depth):
        parts = [m.reshape(batch, 1, d) for m in jnp.split(mod[l], 9, axis=-1)]
        sh1, sc1, g1, sh2, sc2, g2, sh3, sc3, g3 = parts
        xt = _ffn(xt, norm_g[l, 0:1], sh1, sc1, g1, wg, wu, wd, l, 0, seq)
        ng2 = norm_g[l, 1:2]
        p = _inproj(xt, ng2, sh2, sc2, w_main, gain, bd, l, seq, N_NORM_TILES, BF16)
        side = _inproj(xt, ng2, sh2, sc2, w_side, gain_side, bd_side, l, seq, 0, F32)
        o_a = _mixer_dsa(p, side, bias_dsa, tri_before, batch, seq)
        o_b = _mixer_swa(p, sinks[l], bias, batch, seq)
        cum = _fox_cum(side, fb_rows[l], tri_incl, batch, seq)
        o_c = _mixer_fox(p, cum, batch, seq)
        o_d = _mixer_stick(p, tri_after, batch, seq)
        xt = _outproj(xt, g2, (o_a, o_b, o_c, o_d), gg, wo, l, seq)
        xt = _ffn(xt, norm_g[l, 2:3], sh3, sc3, g3, wg, wu, wd, l, 1, seq)
    return xt.reshape(batch, seq, d)
```

```python
import functools
import math

import numpy as np
import jax
import jax.numpy as jnp
from jax import lax
from jax.experimental import pallas as pl
from jax.experimental.pallas import tpu as pltpu

F32 = jnp.float32
BF16 = jnp.bfloat16
I32 = jnp.int32

HEAD_DIM = 64
N_MIXERS = 4
GROUP_HEADS = 8
GROUP_WIDTH = GROUP_HEADS * HEAD_DIM
IDX_HEADS = 16
IDX_DIM = 64
TOPK_MAX = 256
WINDOW = 128
REL_BUCKETS = 32
REL_MAX_DIST = 128
EPS = 1e-6

LANES = 128
PAIRS = GROUP_HEADS // 2
NEG = -1e30
LOG2E = math.log2(math.e)
INT_MIN = -2 ** 31
STICK_EXIT = -110.0
VMEM_LIMIT = 56 * 1024 * 1024

PT = 512
T_AQ, T_BQ, T_CQ, T_CK, T_K2 = 0, 1, 2, 3, 4
N_NORM_TILES = 5
T_CV, T_CG, T_DQ, T_DK, T_DV, T_IQ, T_MISC = 5, 6, 7, 8, 9, 10, 12
N_TILES = 13
P_COLS = N_TILES * PT


def _cparams(sem, vmem=VMEM_LIMIT):
    return pltpu.CompilerParams(dimension_semantics=sem, vmem_limit_bytes=vmem)


def _nt_dot(a, b):
    return lax.dot_general(a, b, (((1,), (1,)), ((), ())), preferred_element_type=F32)


def _dot(a, b):
    return jnp.dot(a, b, preferred_element_type=F32)


def _sigmoid(x):
    return 1.0 / (1.0 + jnp.exp(-x))


def _lane_tile(x, n):
    return x if n == 1 else jnp.concatenate([x] * n, axis=1)


def _own_lane_mask(shape, h):
    lane = lax.broadcasted_iota(I32, shape, 1)
    return (lane < HEAD_DIM) if h % 2 == 0 else (lane >= HEAD_DIM)


def _mod_kernel(c_ref, w_ref, b_ref, o_ref):
    c = c_ref[...]
    cond = (c * _sigmoid(c)).astype(BF16)
    o_ref[0] = _dot(cond, w_ref[0].astype(BF16)) + b_ref[0]


def _modulation(c, w_ada, b_ada):
    depth, d, n = w_ada.shape
    b = c.shape[0]
    rows = 8
    tn = 1024
    c_pad = jnp.zeros((rows, d), F32).at[:b].set(c)
    out = pl.pallas_call(
        _mod_kernel,
        grid=(depth, n // tn),
        in_specs=[
            pl.BlockSpec((rows, d), lambda l, j: (0, 0)),
            pl.BlockSpec((1, d, tn), lambda l, j: (l, 0, j)),
            pl.BlockSpec((1, 1, tn), lambda l, j: (l, 0, j)),
        ],
        out_specs=pl.BlockSpec((1, rows, tn), lambda l, j: (l, 0, j)),
        out_shape=jax.ShapeDtypeStruct((depth, rows, n), F32),
        compiler_params=_cparams(("arbitrary", "arbitrary")),
        name="adaln_mod",
    )(c_pad, w_ada, b_ada.reshape(depth, 1, n))
    return out[:, :b]


def _norm_modulate(x, ng, sh, sc):
    ms = jnp.mean(x * x, axis=-1, keepdims=True)
    y = x * lax.rsqrt(ms + EPS) * ng
    return y * (1.0 + sc) + sh


def _ffn_kernel(x_ref, ng_ref, sh_ref, sc_ref, gt_ref, wg_ref, wu_ref, wd_ref, o_ref, h_ref, a_ref):
    f = pl.program_id(1)
    last = pl.num_programs(1) - 1

    def activation():
        h = h_ref[...]
        g = _dot(h, wg_ref[...])
        u = _dot(h, wu_ref[...])
        return (g * _sigmoid(g) * u).astype(BF16)

    @pl.when(f == 0)
    def _():
        h = _norm_modulate(x_ref[...], ng_ref[...], sh_ref[0], sc_ref[0])
        h_ref[...] = h.astype(BF16)
        o_ref[...] = jnp.zeros_like(o_ref)
        a_ref[...] = activation()

    @pl.when(jnp.logical_and(f > 0, f < last))
    def _():
        o_ref[...] += _dot(a_ref[...], wd_ref[...])
        a_ref[...] = activation()

    @pl.when(f == last)
    def _():
        acc = o_ref[...] + _dot(a_ref[...], wd_ref[...])
        o_ref[...] = x_ref[...] + 0.5 * gt_ref[0] * acc


def _ffn(x, ng, sh, sc, gt, wg, wu, wd, l, s, seq):
    t, d = x.shape
    tm, tf = min(1024, seq), 512
    nf = wg.shape[-1] // tf
    per_b = seq // tm
    vec = pl.BlockSpec((1, 1, d), lambda i, f: (i // per_b, 0, 0))
    up = lambda i, f: (l, s, 0, jnp.minimum(f, nf - 1))
    return pl.pallas_call(
        _ffn_kernel,
        grid=(t // tm, nf + 1),
        in_specs=[
            pl.BlockSpec((tm, d), lambda i, f: (i, 0), pipeline_mode=pl.Buffered(1)),
            pl.BlockSpec((1, d), lambda i, f: (0, 0)),
            vec, vec, vec,
            pl.BlockSpec((None, None, d, tf), up),
            pl.BlockSpec((None, None, d, tf), up),
            pl.BlockSpec((None, None, tf, d), lambda i, f: (l, s, jnp.maximum(f - 1, 0), 0)),
        ],
        out_specs=pl.BlockSpec((tm, d), lambda i, f: (i, 0)),
        out_shape=jax.ShapeDtypeStruct((t, d), F32),
        scratch_shapes=[pltpu.VMEM((tm, d), BF16), pltpu.VMEM((tm, tf), BF16)],
        compiler_params=_cparams(("arbitrary", "arbitrary")),
        name="ffn",
    )(x, ng, sh, sc, gt, wg, wu, wd)


def _inproj_kernel(x_ref, ng_ref, sh_ref, sc_ref, w_ref, gain_ref, bd_ref, o_ref, h_ref, *, n_norm):
    j = pl.program_id(1)

    @pl.when(j == 0)
    def _():
        h = _norm_modulate(x_ref[...], ng_ref[...], sh_ref[0], sc_ref[0])
        h_ref[...] = h.astype(BF16)

    y = _dot(h_ref[...], w_ref[...])

    @pl.when(j < n_norm)
    def _():
        sq = y * y
        hi = sq.astype(BF16)
        lo = (sq - hi.astype(F32)).astype(BF16)
        ms = _dot(hi, bd_ref[...]) + _dot(lo, bd_ref[...])
        o_ref[...] = (y * lax.rsqrt(ms + EPS) * gain_ref[...]).astype(o_ref.dtype)

    @pl.when(j >= n_norm)
    def _():
        o_ref[...] = y.astype(o_ref.dtype)


def _inproj(x, ng, sh, sc, w, gain, bd, l, seq, n_norm, out_dtype):
    t, d = x.shape
    n = w.shape[-1]
    tm = min(1024, seq)
    tn = bd.shape[0]
    per_b = seq // tm
    vec = pl.BlockSpec((1, 1, d), lambda i, j: (i // per_b, 0, 0))
    return pl.pallas_call(
        functools.partial(_inproj_kernel, n_norm=n_norm),
        grid=(t // tm, n // tn),
        in_specs=[
            pl.BlockSpec((tm, d), lambda i, j: (i, 0)),
            pl.BlockSpec((1, d), lambda i, j: (0, 0)),
            vec, vec,
            pl.BlockSpec((None, d, tn), lambda i, j: (l, 0, j)),
            pl.BlockSpec((None, 1, tn), lambda i, j: (l, 0, j)),
            pl.BlockSpec((tn, tn), lambda i, j: (0, 0)),
        ],
        out_specs=pl.BlockSpec((tm, tn), lambda i, j: (i, j)),
        out_shape=jax.ShapeDtypeStruct((t, n), out_dtype),
        scratch_shapes=[pltpu.VMEM((tm, d), BF16)],
        compiler_params=_cparams(("arbitrary", "arbitrary")),
        name="inproj",
    )(x, ng, sh, sc, w, gain, bd)


def _rel_bucket_np(dist):
    n = np.maximum(dist, 0)
    max_exact = REL_BUCKETS // 2
    nf = np.maximum(n, 1).astype(np.float32)
    large = max_exact + (np.log(nf / np.float32(max_exact)) / np.float32(math.log(REL_MAX_DIST / max_exact))
                         * np.float32(REL_BUCKETS - max_exact)).astype(np.int32)
    large = np.minimum(large, REL_BUCKETS - 1)
    return np.where(n < max_exact, n, large).astype(np.int32)


def _bias_kernel(tab_ref, bkt_ref, o_ref, *, sub_far):
    h = pl.program_id(0)
    far = tab_ref[REL_BUCKETS - 1, h]
    for which in range(2):
        bkt = bkt_ref[which]
        acc = jnp.zeros(bkt.shape, F32)
        for b in range(REL_BUCKETS):
            acc = jnp.where(bkt == b, tab_ref[b, h], acc)
        o_ref[0, which] = (acc - jnp.where(h < sub_far, far, 0.0)) * LOG2E


def _bias_tiles(rel_table, tq):
    t_l = np.arange(tq)[:, None]
    s_l = np.arange(LANES)[None, :]
    bkt = np.stack([_rel_bucket_np(t_l + LANES - s_l), _rel_bucket_np(t_l - s_l)]).astype(np.int32)
    nh = rel_table.shape[1]
    return pl.pallas_call(
        functools.partial(_bias_kernel, sub_far=GROUP_HEADS),
        grid=(nh,),
        in_specs=[
            pl.BlockSpec(memory_space=pltpu.SMEM),
            pl.BlockSpec((2, tq, LANES), lambda h: (0, 0, 0)),
        ],
        out_specs=pl.BlockSpec((1, 2, tq, LANES), lambda h: (h, 0, 0, 0)),
        out_shape=jax.ShapeDtypeStruct((nh, 2, tq, LANES), F32),
        compiler_params=_cparams(("arbitrary",)),
        name="rel_bias_tiles",
    )(rel_table, jnp.asarray(bkt))


def _dsa_window_bias(bias):
    prev, cur = bias[:GROUP_HEADS, 0], bias[:GROUP_HEADS, 1]
    return jnp.stack([jnp.concatenate([cur, cur], axis=-1), jnp.concatenate([prev, cur], axis=-1)])


def _swa_window_bias(bias):
    return jnp.concatenate([bias[GROUP_HEADS:, 0], bias[GROUP_HEADS:, 1]], axis=-1)


def _softmax_step_stacked(s, v, m_ref, l_ref, acc_ref):
    nh, rows, width = s.shape
    reps = width // LANES
    m_prev = m_ref[...]
    m_new = jnp.maximum(m_prev, jnp.max(s, axis=2, keepdims=True))
    p = jnp.exp2(s - jnp.concatenate([m_new] * reps, axis=2))
    alpha = jnp.exp2(m_prev - m_new)
    l_ref[...] = alpha * l_ref[...] + jnp.sum(p, axis=2, keepdims=True)
    m_ref[...] = m_new
    pv = _dot(p.astype(BF16).reshape(nh * rows, width), v).reshape(nh, rows, LANES)
    acc_ref[...] = alpha * acc_ref[...] + pv


def _store_pairs(o_ref, per_head):
    for j in range(PAIRS):
        lane = lax.broadcasted_iota(I32, per_head[0].shape, 1)
        o_ref[:, j * LANES:(j + 1) * LANES] = jnp.where(
            lane < HEAD_DIM, per_head[2 * j], per_head[2 * j + 1]).astype(o_ref.dtype)


DSA_TQ = 128
DSA_KC1 = 512
DSA_KC = 512
DSA_SUB = DSA_KC // LANES
DSA_VALUE_STEPS = 26
DSA_VALUE_BLIND_STEPS = 10


def _dsa_kernel(q_ref, iq_ref, iw_ref, k2_ref, v2_ref, ik2_ref, bias_ref, tri_ref, o_ref,
                key_ref, wb_ref, iqm_ref, qm_ref, t_ref, m_ref, l_ref, acc_ref, *, k_top):
    i = pl.program_id(1)
    tq = DSA_TQ
    row0 = i * tq
    n_valid_cols = row0 + tq

    iw = iw_ref[...]
    for h in range(IDX_HEADS):
        col = GROUP_HEADS + h
        wb_ref[h] = jnp.broadcast_to(iw[:, col:col + 1], (tq, LANES))
        blk = iq_ref[:, (h // 2) * LANES:(h // 2 + 1) * LANES]
        iqm_ref[h] = jnp.where(_own_lane_mask(blk.shape, h), blk, jnp.zeros_like(blk))
    for h in range(GROUP_HEADS):
        blk = q_ref[:, (h // 2) * LANES:(h // 2 + 1) * LANES]
        qm_ref[h] = jnp.where(_own_lane_mask(blk.shape, h), blk, jnp.zeros_like(blk))

    qpos = row0 + lax.broadcasted_iota(I32, (tq, DSA_KC1), 0)
    iq_all = iqm_ref[...].reshape(IDX_HEADS * tq, LANES)
    wb_all = jnp.concatenate([wb_ref[...]] * (DSA_KC1 // LANES), axis=2)
    n1 =(n_valid_cols + DSA_KC1 - 1) // DSA_KC1

    def score_chunk(c, carry):
        start = pl.multiple_of(c * DSA_KC1, DSA_KC1)
        ik = ik2_ref[pl.ds(start, DSA_KC1), :]
        r = _nt_dot(iq_all, ik).reshape(IDX_HEADS, tq, DSA_KC1)
        score = jnp.sum(jnp.maximum(r, 0.0) * wb_all, axis=0)
        bits = lax.bitcast_convert_type(score, I32)
        key = bits ^ ((bits >> 31) & 0x7FFFFFFF)
        key = jnp.where(score == 0.0, 0, key)
        kpos = start + lax.broadcasted_iota(I32, (tq, DSA_KC1), 1)
        valid = kpos <= qpos
        key = jnp.where(valid, key, INT_MIN)
        smax, smin = carry
        hi = jnp.where(valid, score, -jnp.inf)
        lo = jnp.where(valid, score, jnp.inf)
        for u in range(DSA_KC1 // LANES):
            key_ref[c * (DSA_KC1 // LANES) + u] = key[:, u * LANES:(u + 1) * LANES]
            smax = jnp.maximum(smax, hi[:, u * LANES:(u + 1) * LANES])
            smin = jnp.minimum(smin, lo[:, u * LANES:(u + 1) * LANES])
        return smax, smin

    smax, smin = lax.fori_loop(0, n1, score_chunk, (jnp.full((tq, LANES), -jnp.inf, F32),
                                                    jnp.full((tq, LANES), jnp.inf, F32)))
    nsel = (n_valid_cols + DSA_KC - 1) // DSA_KC
    n1_blocks = n1 * (DSA_KC1 // LANES)

    @pl.when(n1_blocks < nsel * DSA_SUB)
    def _():
        for u in range(DSA_KC1 // LANES):
            key_ref[n1_blocks + u] = jnp.full((tq, LANES), INT_MIN, I32)

    def load_keys(c):
        blk = key_ref[pl.ds(c * DSA_SUB, DSA_SUB)]
        return jnp.concatenate([blk[u] for u in range(DSA_SUB)], axis=1)

    def count_ge(thr, strict):
        def body(c, acc):
            blk = key_ref[pl.ds(c * DSA_SUB, DSA_SUB)]
            for u in range(DSA_SUB):
                hit = (blk[u] > thr) if strict else (blk[u] >= thr)
                acc = jnp.where(hit, acc + 1.0, acc)
            return acc
        acc = lax.fori_loop(0, nsel, body, jnp.zeros((tq, LANES), F32))
        return jnp.broadcast_to(jnp.sum(acc, axis=1, keepdims=True), (tq, LANES))

    t_ref[...] = jnp.full((tq, LANES), INT_MIN, I32)
    kf = float(k_top)

    def float_key(x):
        bits = lax.bitcast_convert_type(x, I32)
        return jnp.where(x == 0.0, 0, bits ^ ((bits >> 31) & 0x7FFFFFFF))

    def bisect_values():
        lower = jnp.broadcast_to(jnp.min(smin, axis=1, keepdims=True), (tq, LANES))
        upper = jnp.broadcast_to(jnp.max(smax, axis=1, keepdims=True), (tq, LANES))
        n_causal = (row0 + 1 + lax.broadcasted_iota(I32, (tq, LANES), 0)).astype(F32)
        keep_all = n_causal <= kf
        open0 = jnp.where(keep_all, 0.0, 1.0)

        def step(lower, upper, still_open):
            mid = 0.5 * lower + 0.5 * upper
            cnt = count_ge(float_key(mid), False)
            active = still_open > 0.0
            go_up = jnp.logical_and(active, cnt >= kf)
            go_down = jnp.logical_and(active, cnt < kf)
            lower = jnp.where(go_up, mid, lower)
            upper = jnp.where(go_down, mid, upper)
            still_open = jnp.where(jnp.logical_and(go_up, cnt == kf), 0.0, still_open)
            return lower, upper, still_open

        lower, upper, still_open = lax.fori_loop(
            0, DSA_VALUE_BLIND_STEPS, lambda _, c: step(*c), (lower, upper, open0))

        def cond(carry):
            it, _, _, still_open = carry
            return jnp.logical_and(it < DSA_VALUE_STEPS, jnp.max(still_open) > 0.0)

        def body(carry):
            it, lower, upper, still_open = carry
            return (it + 1,) + step(lower, upper, still_open)

        _, lower, _, still_open = lax.while_loop(
            cond, body, (jnp.int32(DSA_VALUE_BLIND_STEPS), lower, upper, still_open))
        return jnp.where(keep_all, INT_MIN, float_key(lower)), jnp.max(still_open) == 0.0

    def exact_select():
        def bit_step(it, carry):
            thr, cnt_thr = carry
            cand = thr + jnp.left_shift(jnp.int32(1), 31 - it)
            cnt = count_ge(cand, False)
            take = cnt >= kf
            return jnp.where(take, cand, thr), jnp.where(take, cnt, cnt_thr)

        thr0 = jnp.full((tq, LANES), INT_MIN, I32)
        cnt0 = jnp.full((tq, LANES), float(2 ** 24), F32)
        thr, cnt_thr = lax.fori_loop(0, 32, bit_step, (thr0, cnt0))
        t_ref[...] = thr

        excess = jnp.where(thr > INT_MIN, cnt_thr - kf, 0.0)

        @pl.when(jnp.max(excess) > 0.0)
        def _():
            need = kf - count_ge(thr, True)

            def tie_chunk(c, seen):
                keys = load_keys(c)
                eq = keys == _lane_tile(thr, DSA_SUB)
                eqf = jnp.where(eq, 1.0, 0.0)
                rank = _dot(eqf.astype(BF16), tri_ref[...]) + _lane_tile(seen, DSA_SUB)
                dropped = jnp.where(rank >= _lane_tile(need, DSA_SUB), INT_MIN, keys)
                keys = jnp.where(eq, dropped, keys)
                for u in range(DSA_SUB):
                    key_ref[c * DSA_SUB + u] = keys[:, u * LANES:(u + 1) * LANES]
                return seen + jnp.broadcast_to(jnp.sum(eqf, axis=1, keepdims=True), (tq, LANES))

            lax.fori_loop(0, nsel, tie_chunk, jnp.zeros((tq, LANES), F32))

    @pl.when(n_valid_cols > k_top)
    def _():
        thr_fast, finished = bisect_values()
        t_ref[...] = thr_fast

        @pl.when(jnp.logical_not(finished))
        def _():
            exact_select()

    thr = jnp.maximum(t_ref[...], INT_MIN + 1)

    m_ref[...] = jnp.full(m_ref.shape, NEG, F32)
    l_ref[...] = jnp.zeros(l_ref.shape, F32)
    acc_ref[...] = jnp.zeros(acc_ref.shape, F32)
    q_all = qm_ref[...].reshape(GROUP_HEADS * tq, LANES)

    def attend(k_blk, v_blk, madd, bias):
        width = k_blk.shape[0]
        s = _nt_dot(q_all, k_blk).reshape(GROUP_HEADS, tq, width) + madd[None]
        if bias is not None:
            s = s + bias
        _softmax_step_stacked(s, v_blk, m_ref, l_ref, acc_ref)

    win_blk = jnp.maximum(i - 1, 0)
    win_start = pl.multiple_of(win_blk * LANES, LANES)
    near_keys = key_ref[pl.ds(win_blk, 2)]
    near_keys = jnp.concatenate([near_keys[0], near_keys[1]], axis=1)
    near_madd = jnp.where(near_keys >= _lane_tile(thr, 2), 0.0, NEG)
    attend(k2_ref[pl.ds(win_start, 2 * LANES), :], v2_ref[pl.ds(win_start, 2 * LANES), :],
           near_madd, bias_ref[jnp.minimum(i, 1)])
    for u in range(2):
        key_ref[win_blk + u] = jnp.full((tq, LANES), INT_MIN, I32)

    n_far = (win_blk * LANES + DSA_KC - 1) // DSA_KC

    def far_chunk(c, carry):
        start = pl.multiple_of(c * DSA_KC, DSA_KC)
        madd = jnp.where(load_keys(c) >= _lane_tile(thr, DSA_SUB), 0.0, NEG)
        attend(k2_ref[pl.ds(start, DSA_KC), :], v2_ref[pl.ds(start, DSA_KC), :], madd, None)
        return carry

    lax.fori_loop(0, n_far, far_chunk, 0)
    _store_pairs(o_ref, [acc_ref[h] / l_ref[h] for h in range(GROUP_HEADS)])


def _mixer_dsa(p, side, bias, tri, batch, seq):
    t = p.shape[0]
    tq = DSA_TQ
    nq = seq // tq
    k_top = min(TOPK_MAX, seq // 4)
    cw = PT // LANES
    return pl.pallas_call(
        functools.partial(_dsa_kernel, k_top=k_top),
        grid=(batch, nq),
        in_specs=[
            pl.BlockSpec((tq, PT), lambda b, i: (b * nq + i, T_AQ)),
            pl.BlockSpec((tq, 2 * PT), lambda b, i: (b * nq + i, T_IQ // 2)),
            pl.BlockSpec((tq, LANES), lambda b, i: (b * nq + i, 0)),
            pl.BlockSpec((seq, LANES), lambda b, i: (b, T_K2 * cw)),
            pl.BlockSpec((seq, LANES), lambda b, i: (b, T_MISC * cw)),
            pl.BlockSpec((seq, LANES), lambda b, i: (b, T_MISC * cw + 2)),
            pl.BlockSpec((2, GROUP_HEADS, tq, 2 * LANES), lambda b, i: (0, 0, 0, 0)),
            pl.BlockSpec((DSA_KC, DSA_KC), lambda b, i: (0, 0)),
        ],
        out_specs=pl.BlockSpec((tq, GROUP_WIDTH), lambda b, i: (b * nq + i, 0)),
        out_shape=jax.ShapeDtypeStruct((t, GROUP_WIDTH), F32),
        scratch_shapes=[
            pltpu.VMEM((max(seq // LANES, DSA_SUB), tq, LANES), I32),
            pltpu.VMEM((IDX_HEADS, tq, LANES), F32),
            pltpu.VMEM((IDX_HEADS, tq, LANES), BF16),
            pltpu.VMEM((GROUP_HEADS, tq, LANES), BF16),
            pltpu.VMEM((tq, LANES), I32),
            pltpu.VMEM((GROUP_HEADS, tq, LANES), F32),
            pltpu.VMEM((GROUP_HEADS, tq, LANES), F32),
            pltpu.VMEM((GROUP_HEADS, tq, LANES), F32),
        ],
        compiler_params=_cparams(("arbitrary", "arbitrary")),
        name="mixer_dsa",
    )(p, p, side, p, p, p, bias, tri)


SWA_TQ = 128


def _swa_kernel(sink_ref, q_ref, kp_ref, kc_ref, vp_ref, vc_ref, bias_ref, o_ref):
    i = pl.program_id(1)
    tq = SWA_TQ
    nh = GROUP_HEADS
    t_l = lax.broadcasted_iota(I32, (tq, 2 * LANES), 0)
    w = lax.broadcasted_iota(I32, (tq, 2 * LANES), 1)
    first = jnp.where(i > 0, t_l + 1, LANES)
    madd = jnp.where(w >= first, jnp.where(w <= t_l + LANES, 0.0, NEG), NEG)
    q_heads = []
    sink_rows = []
    for h in range(nh):
        blk = q_ref[:, (h // 2) * LANES:(h // 2 + 1) * LANES]
        q_heads.append(jnp.where(_own_lane_mask(blk.shape, h), blk, jnp.zeros_like(blk)))
        sink_rows.append(jnp.full((1, tq, LANES), sink_ref[h] * LOG2E, F32))
    sinks = jnp.concatenate(sink_rows, axis=0)
    k_win = jnp.concatenate([kp_ref[...], kc_ref[...]], axis=0)
    v_win = jnp.concatenate([vp_ref[...], vc_ref[...]], axis=0)
    s = _nt_dot(jnp.concatenate(q_heads, axis=0), k_win).reshape(nh, tq, 2 * LANES) + bias_ref[...] + madd[None]
    m = jnp.maximum(jnp.max(s, axis=2, keepdims=True), sinks)
    p = jnp.exp2(s - jnp.concatenate([m, m], axis=2))
    l = jnp.sum(p, axis=2, keepdims=True) + jnp.exp2(sinks - m)
    out = _dot(p.astype(BF16).reshape(nh * tq, 2 * LANES), v_win).reshape(nh, tq, LANES) / l
    _store_pairs(o_ref, [out[h] for h in range(nh)])


def _mixer_swa(p, sinks, bias, batch, seq):
    t = p.shape[0]
    tq = SWA_TQ
    nq = seq // tq
    cw = PT // LANES
    cur = lambda col: (lambda b, i: (b * nq + i, col))
    prev = lambda col: (lambda b, i: (b * nq + jnp.maximum(i - 1, 0), col))
    return pl.pallas_call(
        _swa_kernel,
        grid=(batch, nq),
        in_specs=[
            pl.BlockSpec(memory_space=pltpu.SMEM),
            pl.BlockSpec((tq, PT), lambda b, i: (b * nq + i, T_BQ)),
            pl.BlockSpec((tq, LANES), prev(T_K2 * cw + 1)),
            pl.BlockSpec((tq, LANES), cur(T_K2 * cw + 1)),
            pl.BlockSpec((tq, LANES), prev(T_MISC * cw + 1)),
            pl.BlockSpec((tq, LANES), cur(T_MISC * cw + 1)),
            pl.BlockSpec((GROUP_HEADS, tq, 2 * LANES), lambda b, i: (0, 0, 0)),
        ],
        out_specs=pl.BlockSpec((tq, GROUP_WIDTH), lambda b, i: (b * nq + i, 0)),
        out_shape=jax.ShapeDtypeStruct((t, GROUP_WIDTH), F32),
        compiler_params=_cparams(("arbitrary", "arbitrary")),
        name="mixer_swa",
    )(sinks, p, p, p, p, p, bias)


def _logsig(x):
    return jnp.minimum(x, 0.0) - jnp.log(1.0 + jnp.exp(-jnp.abs(x)))


def _foxcum_kernel(fb_ref, f_ref, tri_ref, o_ref, *, nchunk):
    def body(c, carry):
        start = pl.multiple_of(c * LANES, LANES)
        lf = _logsig(f_ref[pl.ds(start, LANES), :] + fb_ref[...]).T
        p1 = lf.astype(BF16)
        r1 = lf - p1.astype(F32)
        p2 = r1.astype(BF16)
        p3 = (r1 - p2.astype(F32)).astype(BF16)
        tri = tri_ref[...]
        cum = _dot(p1, tri) + _dot(p2, tri) + _dot(p3, tri) + carry
        o_ref[0, c] = cum[:GROUP_HEADS]
        return jnp.broadcast_to(cum[:, LANES - 1:LANES], cum.shape)

    lax.fori_loop(0, nchunk, body, jnp.zeros((LANES, LANES), F32))


def _fox_cum(side, fb_row, tri_incl, batch, seq):
    nchunk = seq // LANES
    return pl.pallas_call(
        functools.partial(_foxcum_kernel, nchunk=nchunk),
        grid=(batch,),
        in_specs=[
            pl.BlockSpec((1, LANES), lambda b: (0, 0)),
            pl.BlockSpec((seq, LANES), lambda b: (b, 0)),
            pl.BlockSpec((LANES, LANES), lambda b: (0, 0)),
        ],
        out_specs=pl.BlockSpec((1, nchunk, GROUP_HEADS, LANES), lambda b: (b, 0, 0, 0)),
        out_shape=jax.ShapeDtypeStruct((batch, nchunk, GROUP_HEADS, LANES), F32),
        compiler_params=_cparams(("arbitrary",)),
        name="fox_cumsum",
    )(fb_row, side, tri_incl)


FOX_TQ = 256
FOX_KC = 512
FOX_SUB = FOX_KC // LANES


def _fox_kernel(q_ref, g_ref, k_ref, v_ref, cum_ref, o_ref, qm_ref, m_ref, l_ref, acc_ref):
    i = pl.program_id(1)
    tq = FOX_TQ
    for h in range(GROUP_HEADS):
        blk = q_ref[:, (h // 2) * LANES:(h // 2 + 1) * LANES]
        qm_ref[h] = jnp.where(_own_lane_mask(blk.shape, h), blk, jnp.zeros_like(blk))
    m_ref[...] = jnp.full(m_ref.shape, NEG, F32)
    l_ref[...] = jnp.zeros(l_ref.shape, F32)
    acc_ref[...] = jnp.zeros(acc_ref.shape, F32)
    f_ref0 = cum_ref[0, i * (tq // LANES)][:, 0:1]

    def chunk(c, masked):
        start = pl.multiple_of(c * FOX_KC, FOX_KC)
        cum = cum_ref[0, pl.ds(c * FOX_SUB, FOX_SUB)]
        cum = jnp.concatenate([cum[u] for u in range(FOX_SUB)], axis=1)
        fbias = (f_ref0 - cum) * LOG2E
        per_head = []
        for pr in range(PAIRS):
            pair = slice(pr * LANES, (pr + 1) * LANES)
            q_pair = qm_ref[2 * pr:2 * pr + 2].reshape(2 * tq, LANES)
            s_pair = _nt_dot(q_pair, k_ref[pl.ds(start, FOX_KC), pair])
            for e in range(2):
                h = 2 * pr + e
                per_head.append(s_pair[e * tq:(e + 1) * tq] + fbias[h:h + 1, :])
        s = jnp.stack(per_head)
        if masked:
            shape = (GROUP_HEADS, tq, FOX_KC)
            ok = start + lax.broadcasted_iota(I32, shape, 2) <= i * tq + lax.broadcasted_iota(I32, shape, 1)
            s = jnp.where(ok, s, NEG)
        reps = FOX_KC // LANES
        m_prev = m_ref[...]
        m_new = jnp.maximum(m_prev, jnp.max(s, axis=2, keepdims=True))
        p = jnp.exp2(s - jnp.concatenate([m_new] * reps, axis=2))
        alpha = jnp.exp2(m_prev - m_new)
        l_ref[...] = alpha * l_ref[...] + jnp.sum(p, axis=2, keepdims=True)
        m_ref[...] = m_new
        p = p.astype(BF16)
        pvs = []
        for pr in range(PAIRS):
            p_pair = p[2 * pr:2 * pr + 2].reshape(2 * tq, FOX_KC)
            pvs.append(_dot(p_pair, v_ref[pl.ds(start, FOX_KC), pr * LANES:(pr + 1) * LANES]))
        acc_ref[...] = alpha * acc_ref[...] + jnp.concatenate(pvs, axis=0).reshape(GROUP_HEADS, tq, LANES)

    def body(c, carry):
        chunk(c, False)
        return carry

    n_full = (i * tq) // FOX_KC
    lax.fori_loop(0, n_full, body, 0)
    chunk(n_full, True)
    outs = []
    for h in range(GROUP_HEADS):
        gate = _sigmoid(g_ref[:, (h // 2) * LANES:(h // 2 + 1) * LANES].astype(F32))
        outs.append(acc_ref[h] / l_ref[h] * gate)
    _store_pairs(o_ref, outs)


def _mixer_fox(p, cum, batch, seq):
    t = p.shape[0]
    tq = FOX_TQ
    nq = seq // tq
    return pl.pallas_call(
        _fox_kernel,
        grid=(batch, nq),
        in_specs=[
            pl.BlockSpec((tq, PT), lambda b, i: (b * nq + i, T_CQ)),
            pl.BlockSpec((tq, PT), lambda b, i: (b * nq + i, T_CG)),
            pl.BlockSpec((seq, PT), lambda b, i: (b, T_CK)),
            pl.BlockSpec((seq, PT), lambda b, i: (b, T_CV)),
            pl.BlockSpec((1, seq // LANES, GROUP_HEADS, LANES), lambda b, i: (b, 0, 0, 0)),
        ],
        out_specs=pl.BlockSpec((tq, GROUP_WIDTH), lambda b, i: (b * nq + i, 0)),
        out_shape=jax.ShapeDtypeStruct((t, GROUP_WIDTH), F32),
        scratch_shapes=[
            pltpu.VMEM((GROUP_HEADS, tq, LANES), BF16),
            pltpu.VMEM((GROUP_HEADS, tq, LANES), F32),
            pltpu.VMEM((GROUP_HEADS, tq, LANES), F32),
            pltpu.VMEM((GROUP_HEADS, tq, LANES), F32),
        ],
        compiler_params=_cparams(("arbitrary", "arbitrary")),
        name="mixer_fox",
    )(p, p, p, p, cum)


STK_TQ = 128


def _stick_kernel(q_ref, k_ref, v_ref, tri_ref, o_ref, qm_ref, acc_ref, r_ref):
    i = pl.program_id(1)
    tq = STK_TQ
    nh = GROUP_HEADS
    for h in range(nh):
        blk = q_ref[:, (h // 2) * LANES:(h // 2 + 1) * LANES]
        qm_ref[h] = jnp.where(_own_lane_mask(blk.shape, h), blk, jnp.zeros_like(blk))
    acc_ref[...] = jnp.zeros(acc_ref.shape, F32)
    r_ref[...] = jnp.zeros(r_ref.shape, F32)
    before = (lax.broadcasted_iota(I32, (nh, tq, LANES), 2) < lax.broadcasted_iota(I32, (nh, tq, LANES), 1))
    tri = tri_ref[...]

    def block(j, masked):
        start = pl.multiple_of(j * LANES, LANES)
        zs = []
        for pr in range(PAIRS):
            q_pair = qm_ref[2 * pr:2 * pr + 2].reshape(2 * tq, LANES)
            zs.append(_nt_dot(q_pair, k_ref[pl.ds(start, LANES), pr * LANES:(pr + 1) * LANES]))
        z = jnp.concatenate(zs, axis=0).reshape(nh, tq, LANES)
        lsz = _logsig(z)
        u = lsz - z
        if masked:
            u = jnp.where(before, u, 0.0)
        u2 = u.reshape(nh * tq, LANES)
        u_hi = u2.astype(BF16)
        u_lo = (u2 - u_hi.astype(F32)).astype(BF16)
        nearer = (_dot(u_hi, tri) + _dot(u_lo, tri)).reshape(nh, tq, LANES)
        run = r_ref[...]
        w = jnp.exp(lsz + nearer + run)
        if masked:
            w = jnp.where(before, w, 0.0)
        wb = w.astype(BF16)
        pvs = []
        for pr in range(PAIRS):
            w_pair = wb[2 * pr:2 * pr + 2].reshape(2 * tq, LANES)
            pvs.append(_dot(w_pair, v_ref[pl.ds(start, LANES), pr * LANES:(pr + 1) * LANES]))
        acc_ref[...] += jnp.concatenate(pvs, axis=0).reshape(nh, tq, LANES)
        run = run + jnp.sum(u, axis=2, keepdims=True)
        r_ref[...] = run
        return jnp.max(run)

    rmax = block(i, True)

    def cond(carry):
        j, rmax = carry
        return jnp.logical_and(j >= 0, rmax >= STICK_EXIT)

    def body(carry):
        j, _ = carry
        return j - 1, block(j, False)

    lax.while_loop(cond, body, (i - 1, rmax))
    _store_pairs(o_ref, [acc_ref[h] for h in range(nh)])


def _mixer_stick(p, tri_excl, batch, seq):
    t = p.shape[0]
    tq = STK_TQ
    nq = seq // tq
    return pl.pallas_call(
        _stick_kernel,
        grid=(batch, nq),
        in_specs=[
            pl.BlockSpec((tq, PT), lambda b, i: (b * nq + i, T_DQ)),
            pl.BlockSpec((seq, PT), lambda b, i: (b, T_DK)),
            pl.BlockSpec((seq, PT), lambda b, i: (b, T_DV)),
            pl.BlockSpec((LANES, LANES), lambda b, i: (0, 0)),
        ],
        out_specs=pl.BlockSpec((tq, GROUP_WIDTH), lambda b, i: (b * nq + i, 0)),
        out_shape=jax.ShapeDtypeStruct((t, GROUP_WIDTH), F32),
        scratch_shapes=[pltpu.VMEM((GROUP_HEADS, tq, LANES), BF16),
                        pltpu.VMEM((GROUP_HEADS, tq, LANES), F32),
                        pltpu.VMEM((GROUP_HEADS, tq, LANES), F32)],
        compiler_params=_cparams(("arbitrary", "arbitrary")),
        name="mixer_stick",
    )(p, p, p, tri_excl)


def _outproj_kernel(x_ref, gt_ref, oa_ref, ob_ref, oc_ref, od_ref, gg_ref, w_ref, o_ref):
    acc = None
    for m, ref in enumerate((oa_ref, ob_ref, oc_ref, od_ref)):
        o = ref[...]
        y = o * lax.rsqrt(jnp.mean(o * o, axis=-1, keepdims=True) + EPS) * gg_ref[m:m + 1, :]
        part = _dot(y.astype(BF16), w_ref[m * GROUP_WIDTH:(m + 1) * GROUP_WIDTH, :])
        acc = part if acc is None else acc + part
    o_ref[...] = x_ref[...] + gt_ref[0] * acc


def _outproj(x, gt, outs, gg, w, l, seq):
    t, d = x.shape
    tm = 512
    per_b = seq // tm
    mix = pl.BlockSpec((tm, GROUP_WIDTH), lambda i: (i, 0))
    return pl.pallas_call(
        _outproj_kernel,
        grid=(t // tm,),
        in_specs=[
            pl.BlockSpec((tm, d), lambda i: (i, 0)),
            pl.BlockSpec((1, 1, d), lambda i: (i // per_b, 0, 0)),
            mix, mix, mix, mix,
            pl.BlockSpec((None, N_MIXERS, GROUP_WIDTH), lambda i: (l, 0, 0)),
            pl.BlockSpec((None, N_MIXERS * GROUP_WIDTH, d), lambda i: (l, 0, 0)),
        ],
        out_specs=pl.BlockSpec((tm, d), lambda i: (i, 0)),
        out_shape=jax.ShapeDtypeStruct((t, d), F32),
        compiler_params=_cparams(("arbitrary",)),
        name="outproj",
    )(x, gt, *outs, gg, w)


def _prep_in_weights(w_in, qk_g):
    depth, d, _ = w_in.shape
    sizes = (GROUP_WIDTH, HEAD_DIM, HEAD_DIM, IDX_HEADS * IDX_DIM, IDX_DIM, IDX_HEADS,
             GROUP_WIDTH, HEAD_DIM, HEAD_DIM,
             GROUP_WIDTH, GROUP_WIDTH, GROUP_WIDTH, GROUP_HEADS, GROUP_WIDTH,
             GROUP_WIDTH, GROUP_WIDTH, GROUP_WIDTH)
    pts = np.cumsum(sizes)[:-1].tolist()
    (a_q, a_k, a_v, a_iq, a_ik, a_iw, b_q, b_k, b_v,
     c_q, c_k, c_v, c_f, c_g, d_q, d_k, d_v) = jnp.split(w_in, pts, axis=-1)
    z = lambda n: jnp.zeros((depth, d, n), w_in.dtype)
    qscale = HEAD_DIM ** -0.5
    main = jnp.concatenate([
        a_q, b_q, c_q, c_k,
        a_k, a_k, b_k, b_k, z(2 * LANES),
        c_v, c_g, d_q * qscale, d_k, d_v, a_iq * (IDX_DIM ** -0.5),
        a_v, a_v, b_v, b_v, a_ik, a_ik, z(LANES),
    ], axis=-1).astype(BF16)
    side = jnp.concatenate([c_f, a_iw * (IDX_HEADS ** -0.5), z(LANES - GROUP_HEADS - IDX_HEADS)],
                           axis=-1).astype(BF16)
    rep = lambda g, n: jnp.tile(g, (1, n))
    sscale = qscale * LOG2E
    gain = jnp.concatenate([
        rep(qk_g[:, 0], 8) * sscale, rep(qk_g[:, 2], 8) * sscale, rep(qk_g[:, 4], 8) * sscale,
        rep(qk_g[:, 5], 8), rep(qk_g[:, 1], 2), rep(qk_g[:, 3], 2),
        jnp.zeros((depth, P_COLS - 4 * PT - 2 * LANES), F32),
    ], axis=-1).reshape(depth, 1, P_COLS)
    return main, side, gain


def kernel(x, c, w_ada, b_ada, norm_g, w_in, qk_g, forget_b, sinks, rel_table, group_g, w_out,
           w_ffn_gate, w_ffn_up, w_ffn_down):
    batch, seq, d = x.shape
    depth = w_ada.shape[0]
    t = batch * seq

    mod = _modulation(c, w_ada, b_ada)
    wg = w_ffn_gate.astype(BF16)
    wu = w_ffn_up.astype(BF16)
    wd = w_ffn_down.astype(BF16)
    wo = w_out.astype(BF16)
    w_main, w_side, gain = _prep_in_weights(w_in, qk_g)
    gain_side = jnp.zeros((depth, 1, LANES), F32)

    r = np.arange(PT)
    bd = jnp.asarray((r[:, None] // HEAD_DIM == r[None, :] // HEAD_DIM).astype(np.float32) / HEAD_DIM, BF16)
    bd_side = jnp.zeros((LANES, LANES), BF16)
    r = np.arange(LANES)
    tri_incl = jnp.asarray(r[:, None] <= r[None, :], BF16)
    tri_after = jnp.asarray(r[:, None] > r[None, :], BF16)
    r = np.arange(DSA_KC)
    tri_before = jnp.asarray(r[:, None] < r[None, :], BF16)

    bias = _bias_tiles(rel_table, DSA_TQ)
    bias_dsa = _dsa_window_bias(bias)
    bias_swa = _swa_window_bias(bias)
    fb_rows = jnp.zeros((depth, 1, LANES), F32).at[:, 0, :GROUP_HEADS].set(forget_b)
    gg = group_g.reshape(depth, N_MIXERS, GROUP_WIDTH)

    xt = x.reshape(t, d)
    for l in range(depth):
        parts = [m.reshape(batch, 1, d) for m in jnp.split(mod[l], 9, axis=-1)]
        sh1, sc1, g1, sh2, sc2, g2, sh3, sc3, g3 = parts
        xt = _ffn(xt, norm_g[l, 0:1], sh1, sc1, g1, wg, wu, wd, l, 0, seq)
        ng2 = norm_g[l, 1:2]
        p = _inproj(xt, ng2, sh2, sc2, w_main, gain, bd, l, seq, N_NORM_TILES, BF16)
        side = _inproj(xt, ng2, sh2, sc2, w_side, gain_side, bd_side, l, seq, 0, F32)
        o_a = _mixer_dsa(p, side, bias_dsa, tri_before, batch, seq)
        o_b = _mixer_swa(p, sinks[l], bias_swa, batch, seq)
        cum = _fox_cum(side, fb_rows[l], tri_incl, batch, seq)
        o_c = _mixer_fox(p, cum, batch, seq)
        o_d = _mixer_stick(p, tri_after, batch, seq)
        xt = _outproj(xt, g2, (o_a, o_b, o_c, o_d), gg, wo, l, seq)
        xt = _ffn(xt, norm_g[l, 2:3], sh3, sc3, g3, wg, wu, wd, l, 1, seq)
    return xt.reshape(batch, seq, d)
```

```python
import functools
import math

import numpy as np
import jax
import jax.numpy as jnp
from jax import lax
from jax.experimental import pallas as pl
from jax.experimental.pallas import tpu as pltpu

F32 = jnp.float32
BF16 = jnp.bfloat16
I32 = jnp.int32

HEAD_DIM = 64
N_MIXERS = 4
GROUP_HEADS = 8
GROUP_WIDTH = GROUP_HEADS * HEAD_DIM
IDX_HEADS = 16
IDX_DIM = 64
TOPK_MAX = 256
WINDOW = 128
REL_BUCKETS = 32
REL_MAX_DIST = 128
EPS = 1e-6

LANES = 128
PAIRS = GROUP_HEADS // 2
NEG = -1e30
LOG2E = math.log2(math.e)
INT_MIN = -2 ** 31
STICK_EXIT = -110.0
VMEM_LIMIT = 56 * 1024 * 1024

PT = 512
T_AQ, T_BQ, T_CQ, T_CK, T_K2 = 0, 1, 2, 3, 4
N_NORM_TILES = 5
T_CV, T_CG, T_DQ, T_DK, T_DV, T_IQ, T_MISC = 5, 6, 7, 8, 9, 10, 12
N_TILES = 13
P_COLS = N_TILES * PT


def _cparams(sem, vmem=VMEM_LIMIT):
    return pltpu.CompilerParams(dimension_semantics=sem, vmem_limit_bytes=vmem)


def _nt_dot(a, b):
    return lax.dot_general(a, b, (((1,), (1,)), ((), ())), preferred_element_type=F32)


def _dot(a, b):
    return jnp.dot(a, b, preferred_element_type=F32)


def _sigmoid(x):
    return 1.0 / (1.0 + jnp.exp(-x))


def _lane_tile(x, n):
    return x if n == 1 else jnp.concatenate([x] * n, axis=1)


def _own_lane_mask(shape, h):
    lane = lax.broadcasted_iota(I32, shape, 1)
    return (lane < HEAD_DIM) if h % 2 == 0 else (lane >= HEAD_DIM)


def _mod_kernel(c_ref, w_ref, b_ref, o_ref):
    c = c_ref[...]
    cond = (c * _sigmoid(c)).astype(BF16)
    o_ref[0] = _dot(cond, w_ref[0].astype(BF16)) + b_ref[0]


def _modulation(c, w_ada, b_ada):
    depth, d, n = w_ada.shape
    b = c.shape[0]
    rows = 8
    tn = 1024
    c_pad = jnp.zeros((rows, d), F32).at[:b].set(c)
    out = pl.pallas_call(
        _mod_kernel,
        grid=(depth, n // tn),
        in_specs=[
            pl.BlockSpec((rows, d), lambda l, j: (0, 0)),
            pl.BlockSpec((1, d, tn), lambda l, j: (l, 0, j)),
            pl.BlockSpec((1, 1, tn), lambda l, j: (l, 0, j)),
        ],
        out_specs=pl.BlockSpec((1, rows, tn), lambda l, j: (l, 0, j)),
        out_shape=jax.ShapeDtypeStruct((depth, rows, n), F32),
        compiler_params=_cparams(("arbitrary", "arbitrary")),
        name="adaln_mod",
    )(c_pad, w_ada, b_ada.reshape(depth, 1, n))
    return out[:, :b]


def _norm_modulate(x, ng, sh, sc):
    ms = jnp.mean(x * x, axis=-1, keepdims=True)
    y = x * lax.rsqrt(ms + EPS) * ng
    return y * (1.0 + sc) + sh


def _ffn_kernel(x_ref, ng_ref, sh_ref, sc_ref, gt_ref, wg_ref, wu_ref, wd_ref, o_ref, h_ref, a_ref):
    f = pl.program_id(1)
    last = pl.num_programs(1) - 1

    def activation():
        h = h_ref[...]
        g = _dot(h, wg_ref[...])
        u = _dot(h, wu_ref[...])
        return (g * _sigmoid(g) * u).astype(BF16)

    @pl.when(f == 0)
    def _():
        h = _norm_modulate(x_ref[...], ng_ref[...], sh_ref[0], sc_ref[0])
        h_ref[...] = h.astype(BF16)
        o_ref[...] = jnp.zeros_like(o_ref)
        a_ref[...] = activation()

    @pl.when(jnp.logical_and(f > 0, f < last))
    def _():
        o_ref[...] += _dot(a_ref[...], wd_ref[...])
        a_ref[...] = activation()

    @pl.when(f == last)
    def _():
        acc = o_ref[...] + _dot(a_ref[...], wd_ref[...])
        o_ref[...] = x_ref[...] + 0.5 * gt_ref[0] * acc


def _ffn(x, ng, sh, sc, gt, wg, wu, wd, l, s, seq):
    t, d = x.shape
    tm, tf = min(1024, seq), 512
    nf = wg.shape[-1] // tf
    per_b = seq // tm
    vec = pl.BlockSpec((1, 1, d), lambda i, f: (i // per_b, 0, 0))
    up = lambda i, f: (l, s, 0, jnp.minimum(f, nf - 1))
    return pl.pallas_call(
        _ffn_kernel,
        grid=(t // tm, nf + 1),
        in_specs=[
            pl.BlockSpec((tm, d), lambda i, f: (i, 0), pipeline_mode=pl.Buffered(1)),
            pl.BlockSpec((1, d), lambda i, f: (0, 0)),
            vec, vec, vec,
            pl.BlockSpec((None, None, d, tf), up),
            pl.BlockSpec((None, None, d, tf), up),
            pl.BlockSpec((None, None, tf, d), lambda i, f: (l, s, jnp.maximum(f - 1, 0), 0)),
        ],
        out_specs=pl.BlockSpec((tm, d), lambda i, f: (i, 0)),
        out_shape=jax.ShapeDtypeStruct((t, d), F32),
        scratch_shapes=[pltpu.VMEM((tm, d), BF16), pltpu.VMEM((tm, tf), BF16)],
        compiler_params=_cparams(("arbitrary", "arbitrary")),
        name="ffn",
    )(x, ng, sh, sc, gt, wg, wu, wd)


def _inproj_kernel(x_ref, ng_ref, sh_ref, sc_ref, w_ref, gain_ref, bd_ref, o_ref, h_ref, *, n_norm):
    j = pl.program_id(1)

    @pl.when(j == 0)
    def _():
        h = _norm_modulate(x_ref[...], ng_ref[...], sh_ref[0], sc_ref[0])
        h_ref[...] = h.astype(BF16)

    y = _dot(h_ref[...], w_ref[...])

    @pl.when(j < n_norm)
    def _():
        sq = y * y
        hi = sq.astype(BF16)
        lo = (sq - hi.astype(F32)).astype(BF16)
        ms = _dot(hi, bd_ref[...]) + _dot(lo, bd_ref[...])
        o_ref[...] = (y * lax.rsqrt(ms + EPS) * gain_ref[...]).astype(o_ref.dtype)

    @pl.when(j >= n_norm)
    def _():
        o_ref[...] = y.astype(o_ref.dtype)


def _inproj(x, ng, sh, sc, w, gain, bd, l, seq, n_norm, out_dtype):
    t, d = x.shape
    n = w.shape[-1]
    tm = min(1024, seq)
    tn = bd.shape[0]
    per_b = seq // tm
    vec = pl.BlockSpec((1, 1, d), lambda i, j: (i // per_b, 0, 0))
    return pl.pallas_call(
        functools.partial(_inproj_kernel, n_norm=n_norm),
        grid=(t // tm, n // tn),
        in_specs=[
            pl.BlockSpec((tm, d), lambda i, j: (i, 0)),
            pl.BlockSpec((1, d), lambda i, j: (0, 0)),
            vec, vec,
            pl.BlockSpec((None, d, tn), lambda i, j: (l, 0, j)),
            pl.BlockSpec((None, 1, tn), lambda i, j: (l, 0, j)),
            pl.BlockSpec((tn, tn), lambda i, j: (0, 0)),
        ],
        out_specs=pl.BlockSpec((tm, tn), lambda i, j: (i, j)),
        out_shape=jax.ShapeDtypeStruct((t, n), out_dtype),
        scratch_shapes=[pltpu.VMEM((tm, d), BF16)],
        compiler_params=_cparams(("arbitrary", "arbitrary")),
        name="inproj",
    )(x, ng, sh, sc, w, gain, bd)


def _rel_bucket_np(dist):
    n = np.maximum(dist, 0)
    max_exact = REL_BUCKETS // 2
    nf = np.maximum(n, 1).astype(np.float32)
    large = max_exact + (np.log(nf / np.float32(max_exact)) / np.float32(math.log(REL_MAX_DIST / max_exact))
                         * np.float32(REL_BUCKETS - max_exact)).astype(np.int32)
    large = np.minimum(large, REL_BUCKETS - 1)
    return np.where(n < max_exact, n, large).astype(np.int32)


def _bias_kernel(tab_ref, bkt_ref, o_ref, *, sub_far):
    h = pl.program_id(0)
    far = tab_ref[REL_BUCKETS - 1, h]
    for which in range(2):
        bkt = bkt_ref[which]
        acc = jnp.zeros(bkt.shape, F32)
        for b in range(REL_BUCKETS):
            acc = jnp.where(bkt == b, tab_ref[b, h], acc)
        o_ref[0, which] = (acc - jnp.where(h < sub_far, far, 0.0)) * LOG2E


def _bias_tiles(rel_table, tq):
    t_l = np.arange(tq)[:, None]
    s_l = np.arange(LANES)[None, :]
    bkt = np.stack([_rel_bucket_np(t_l + LANES - s_l), _rel_bucket_np(t_l - s_l)]).astype(np.int32)
    nh = rel_table.shape[1]
    return pl.pallas_call(
        functools.partial(_bias_kernel, sub_far=GROUP_HEADS),
        grid=(nh,),
        in_specs=[
            pl.BlockSpec(memory_space=pltpu.SMEM),
            pl.BlockSpec((2, tq, LANES), lambda h: (0, 0, 0)),
        ],
        out_specs=pl.BlockSpec((1, 2, tq, LANES), lambda h: (h, 0, 0, 0)),
        out_shape=jax.ShapeDtypeStruct((nh, 2, tq, LANES), F32),
        compiler_params=_cparams(("arbitrary",)),
        name="rel_bias_tiles",
    )(rel_table, jnp.asarray(bkt))


def _dsa_window_bias(bias):
    order = jnp.asarray(HEAD_SLOTS)
    prev, cur = bias[order, 0], bias[order, 1]
    return jnp.stack([jnp.concatenate([cur, cur], axis=-1), jnp.concatenate([prev, cur], axis=-1)])


def _swa_window_bias(bias):
    return jnp.concatenate([bias[GROUP_HEADS:, 0], bias[GROUP_HEADS:, 1]], axis=-1)


def _with_ones(v, ones_low):
    lane = lax.broadcasted_iota(I32, v.shape, 1)
    ones_here = (lane < HEAD_DIM) if ones_low else (lane >= HEAD_DIM)
    return jnp.where(ones_here, jnp.ones_like(v), v)


def _normalize(acc):
    return acc / pltpu.roll(acc, HEAD_DIM, 1)


def _softmax_step_stacked(s, v, m_ref, acc_ref):
    nh, rows, width = s.shape
    reps = width // LANES
    half = nh // 2
    m_prev = m_ref[...]
    m_new = jnp.maximum(m_prev, jnp.max(s, axis=2, keepdims=True))
    p = jnp.exp2((s - jnp.concatenate([m_new] * reps, axis=2)).astype(BF16))
    alpha = jnp.exp2(m_prev - m_new)
    m_ref[...] = m_new
    pv_even = _dot(p[:half].reshape(half * rows, width), _with_ones(v, ones_low=False))
    pv_odd = _dot(p[half:].reshape(half * rows, width), _with_ones(v, ones_low=True))
    pv = jnp.concatenate([pv_even, pv_odd], axis=0).reshape(nh, rows, LANES)
    acc_ref[...] = alpha * acc_ref[...] + pv


def _store_pairs(o_ref, per_head):
    for j in range(PAIRS):
        lane = lax.broadcasted_iota(I32, per_head[0].shape, 1)
        o_ref[:, j * LANES:(j + 1) * LANES] = jnp.where(
            lane < HEAD_DIM, per_head[2 * j], per_head[2 * j + 1]).astype(o_ref.dtype)


DSA_TQ = 128
DSA_KC1 = 512
DSA_KC = 512
DSA_SUB = DSA_KC // LANES
DSA_VALUE_STEPS = 26
DSA_VALUE_BLIND_STEPS = 10
HEAD_SLOTS = (0, 2, 4, 6, 1, 3, 5, 7)


def _dsa_kernel(q_ref, iq_ref, iw_ref, k2_ref, v2_ref, ik2_ref, bias_ref, tri_ref, o_ref,
                key_ref, wb_ref, iqm_ref, qm_ref, t_ref, m_ref, acc_ref, *, k_top):
    i = pl.program_id(1)
    tq = DSA_TQ
    row0 = i * tq
    n_valid_cols = row0 + tq

    iw = iw_ref[...]
    for h in range(IDX_HEADS):
        col = GROUP_HEADS + h
        wb_ref[h] = jnp.broadcast_to(iw[:, col:col + 1], (tq, LANES))
        blk = iq_ref[:, (h // 2) * LANES:(h // 2 + 1) * LANES]
        iqm_ref[h] = jnp.where(_own_lane_mask(blk.shape, h), blk, jnp.zeros_like(blk))
    for slot, h in enumerate(HEAD_SLOTS):
        blk = q_ref[:, (h // 2) * LANES:(h // 2 + 1) * LANES]
        qm_ref[slot] = jnp.where(_own_lane_mask(blk.shape, h), blk, jnp.zeros_like(blk))

    qpos = row0 + lax.broadcasted_iota(I32, (tq, DSA_KC1), 0)
    iq_all = iqm_ref[...].reshape(IDX_HEADS * tq, LANES)
    wb_all = jnp.concatenate([wb_ref[...]] * (DSA_KC1 // LANES), axis=2)
    n1 =(n_valid_cols + DSA_KC1 - 1) // DSA_KC1

    def score_chunk(c, carry):
        start = pl.multiple_of(c * DSA_KC1, DSA_KC1)
        ik = ik2_ref[pl.ds(start, DSA_KC1), :]
        r = _nt_dot(iq_all, ik).reshape(IDX_HEADS, tq, DSA_KC1)
        score = jnp.sum(jnp.maximum(r, 0.0) * wb_all, axis=0)
        bits = lax.bitcast_convert_type(score, I32)
        key = bits ^ ((bits >> 31) & 0x7FFFFFFF)
        key = jnp.where(score == 0.0, 0, key)
        kpos = start + lax.broadcasted_iota(I32, (tq, DSA_KC1), 1)
        valid = kpos <= qpos
        key = jnp.where(valid, key, INT_MIN)
        smax, smin = carry
        hi = jnp.where(valid, score, -jnp.inf)
        lo = jnp.where(valid, score, jnp.inf)
        for u in range(DSA_KC1 // LANES):
            key_ref[c * (DSA_KC1 // LANES) + u] = key[:, u * LANES:(u + 1) * LANES]
            smax = jnp.maximum(smax, hi[:, u * LANES:(u + 1) * LANES])
            smin = jnp.minimum(smin, lo[:, u * LANES:(u + 1) * LANES])
        return smax, smin

    smax, smin = lax.fori_loop(0, n1, score_chunk, (jnp.full((tq, LANES), -jnp.inf, F32),
                                                    jnp.full((tq, LANES), jnp.inf, F32)))
    nsel = (n_valid_cols + DSA_KC - 1) // DSA_KC
    n1_blocks = n1 * (DSA_KC1 // LANES)

    @pl.when(n1_blocks < nsel * DSA_SUB)
    def _():
        for u in range(DSA_KC1 // LANES):
            key_ref[n1_blocks + u] = jnp.full((tq, LANES), INT_MIN, I32)

    def load_keys(c):
        blk = key_ref[pl.ds(c * DSA_SUB, DSA_SUB)]
        return jnp.concatenate([blk[u] for u in range(DSA_SUB)], axis=1)

    def count_ge(thr, strict):
        def body(c, acc):
            blk = key_ref[pl.ds(c * DSA_SUB, DSA_SUB)]
            for u in range(DSA_SUB):
                hit = (blk[u] > thr) if strict else (blk[u] >= thr)
                acc = jnp.where(hit, acc + 1.0, acc)
            return acc
        acc = lax.fori_loop(0, nsel, body, jnp.zeros((tq, LANES), F32))
        return jnp.broadcast_to(jnp.sum(acc, axis=1, keepdims=True), (tq, LANES))

    t_ref[...] = jnp.full((tq, LANES), INT_MIN, I32)
    kf = float(k_top)

    def float_key(x):
        bits = lax.bitcast_convert_type(x, I32)
        return jnp.where(x == 0.0, 0, bits ^ ((bits >> 31) & 0x7FFFFFFF))

    def bisect_values():
        lower = jnp.broadcast_to(jnp.min(smin, axis=1, keepdims=True), (tq, LANES))
        upper = jnp.broadcast_to(jnp.max(smax, axis=1, keepdims=True), (tq, LANES))
        n_causal = (row0 + 1 + lax.broadcasted_iota(I32, (tq, LANES), 0)).astype(F32)
        keep_all = n_causal <= kf
        open0 = jnp.where(keep_all, 0.0, 1.0)

        def step(lower, upper, still_open):
            mid = 0.5 * lower + 0.5 * upper
            cnt = count_ge(float_key(mid), False)
            active = still_open > 0.0
            go_up = jnp.logical_and(active, cnt >= kf)
            go_down = jnp.logical_and(active, cnt < kf)
            lower = jnp.where(go_up, mid, lower)
            upper = jnp.where(go_down, mid, upper)
            still_open = jnp.where(jnp.logical_and(go_up, cnt == kf), 0.0, still_open)
            return lower, upper, still_open

        lower, upper, still_open = lax.fori_loop(
            0, DSA_VALUE_BLIND_STEPS, lambda _, c: step(*c), (lower, upper, open0))

        def cond(carry):
            it, _, _, still_open = carry
            return jnp.logical_and(it < DSA_VALUE_STEPS, jnp.max(still_open) > 0.0)

        def body(carry):
            it, lower, upper, still_open = carry
            return (it + 1,) + step(lower, upper, still_open)

        _, lower, _, still_open = lax.while_loop(
            cond, body, (jnp.int32(DSA_VALUE_BLIND_STEPS), lower, upper, still_open))
        return jnp.where(keep_all, INT_MIN, float_key(lower)), jnp.max(still_open) == 0.0

    def exact_select():
        def bit_step(it, carry):
            thr, cnt_thr = carry
            cand = thr + jnp.left_shift(jnp.int32(1), 31 - it)
            cnt = count_ge(cand, False)
            take = cnt >= kf
            return jnp.where(take, cand, thr), jnp.where(take, cnt, cnt_thr)

        thr0 = jnp.full((tq, LANES), INT_MIN, I32)
        cnt0 = jnp.full((tq, LANES), float(2 ** 24), F32)
        thr, cnt_thr = lax.fori_loop(0, 32, bit_step, (thr0, cnt0))
        t_ref[...] = thr

        excess = jnp.where(thr > INT_MIN, cnt_thr - kf, 0.0)

        @pl.when(jnp.max(excess) > 0.0)
        def _():
            need = kf - count_ge(thr, True)

            def tie_chunk(c, seen):
                keys = load_keys(c)
                eq = keys == _lane_tile(thr, DSA_SUB)
                eqf = jnp.where(eq, 1.0, 0.0)
                rank = _dot(eqf.astype(BF16), tri_ref[...]) + _lane_tile(seen, DSA_SUB)
                dropped = jnp.where(rank >= _lane_tile(need, DSA_SUB), INT_MIN, keys)
                keys = jnp.where(eq, dropped, keys)
                for u in range(DSA_SUB):
                    key_ref[c * DSA_SUB + u] = keys[:, u * LANES:(u + 1) * LANES]
                return seen + jnp.broadcast_to(jnp.sum(eqf, axis=1, keepdims=True), (tq, LANES))

            lax.fori_loop(0, nsel, tie_chunk, jnp.zeros((tq, LANES), F32))

    @pl.when(n_valid_cols > k_top)
    def _():
        thr_fast, finished = bisect_values()
        t_ref[...] = thr_fast

        @pl.when(jnp.logical_not(finished))
        def _():
            exact_select()

    thr = jnp.maximum(t_ref[...], INT_MIN + 1)

    m_ref[...] = jnp.full(m_ref.shape, NEG, F32)
    acc_ref[...] = jnp.zeros(acc_ref.shape, F32)
    q_all = qm_ref[...].reshape(GROUP_HEADS * tq, LANES)

    def attend(k_blk, v_blk, madd, bias):
        width = k_blk.shape[0]
        s = _nt_dot(q_all, k_blk).reshape(GROUP_HEADS, tq, width) + madd[None]
        if bias is not None:
            s = s + bias
        _softmax_step_stacked(s, v_blk, m_ref, acc_ref)

    win_blk = jnp.maximum(i - 1, 0)
    win_start = pl.multiple_of(win_blk * LANES, LANES)
    near_keys = key_ref[pl.ds(win_blk, 2)]
    near_keys = jnp.concatenate([near_keys[0], near_keys[1]], axis=1)
    near_madd = jnp.where(near_keys >= _lane_tile(thr, 2), 0.0, NEG)
    attend(k2_ref[pl.ds(win_start, 2 * LANES), :], v2_ref[pl.ds(win_start, 2 * LANES), :],
           near_madd, bias_ref[jnp.minimum(i, 1)])
    for u in range(2):
        key_ref[win_blk + u] = jnp.full((tq, LANES), INT_MIN, I32)

    n_far = (win_blk * LANES + DSA_KC - 1) // DSA_KC

    def far_chunk(c, carry):
        start = pl.multiple_of(c * DSA_KC, DSA_KC)
        madd = jnp.where(load_keys(c) >= _lane_tile(thr, DSA_SUB), 0.0, NEG)
        attend(k2_ref[pl.ds(start, DSA_KC), :], v2_ref[pl.ds(start, DSA_KC), :], madd, None)
        return carry

    lax.fori_loop(0, n_far, far_chunk, 0)
    _store_pairs(o_ref, [_normalize(acc_ref[HEAD_SLOTS.index(h)]) for h in range(GROUP_HEADS)])


def _mixer_dsa(p, side, bias, tri, batch, seq):
    t = p.shape[0]
    tq = DSA_TQ
    nq = seq // tq
    k_top = min(TOPK_MAX, seq // 4)
    cw = PT // LANES
    return pl.pallas_call(
        functools.partial(_dsa_kernel, k_top=k_top),
        grid=(batch, nq),
        in_specs=[
            pl.BlockSpec((tq, PT), lambda b, i: (b * nq + i, T_AQ)),
            pl.BlockSpec((tq, 2 * PT), lambda b, i: (b * nq + i, T_IQ // 2)),
            pl.BlockSpec((tq, LANES), lambda b, i: (b * nq + i, 0)),
            pl.BlockSpec((seq, LANES), lambda b, i: (b, T_K2 * cw)),
            pl.BlockSpec((seq, LANES), lambda b, i: (b, T_MISC * cw)),
            pl.BlockSpec((seq, LANES), lambda b, i: (b, T_MISC * cw + 2)),
            pl.BlockSpec((2, GROUP_HEADS, tq, 2 * LANES), lambda b, i: (0, 0, 0, 0)),
            pl.BlockSpec((DSA_KC, DSA_KC), lambda b, i: (0, 0)),
        ],
        out_specs=pl.BlockSpec((tq, GROUP_WIDTH), lambda b, i: (b * nq + i, 0)),
        out_shape=jax.ShapeDtypeStruct((t, GROUP_WIDTH), F32),
        scratch_shapes=[
            pltpu.VMEM((max(seq // LANES, DSA_SUB), tq, LANES), I32),
            pltpu.VMEM((IDX_HEADS, tq, LANES), F32),
            pltpu.VMEM((IDX_HEADS, tq, LANES), BF16),
            pltpu.VMEM((GROUP_HEADS, tq, LANES), BF16),
            pltpu.VMEM((tq, LANES), I32),
            pltpu.VMEM((GROUP_HEADS, tq, LANES), F32),
            pltpu.VMEM((GROUP_HEADS, tq, LANES), F32),
        ],
        compiler_params=_cparams(("arbitrary", "arbitrary")),
        name="mixer_dsa",
    )(p, p, side, p, p, p, bias, tri)


SWA_TQ = 128


def _swa_kernel(sink_ref, q_ref, kp_ref, kc_ref, vp_ref, vc_ref, bias_ref, o_ref):
    i = pl.program_id(1)
    tq = SWA_TQ
    nh = GROUP_HEADS
    t_l = lax.broadcasted_iota(I32, (tq, 2 * LANES), 0)
    w = lax.broadcasted_iota(I32, (tq, 2 * LANES), 1)
    first = jnp.where(i > 0, t_l + 1, LANES)
    madd = jnp.where(w >= first, jnp.where(w <= t_l + LANES, 0.0, NEG), NEG)
    q_heads = []
    sink_rows = []
    for h in range(nh):
        blk = q_ref[:, (h // 2) * LANES:(h // 2 + 1) * LANES]
        q_heads.append(jnp.where(_own_lane_mask(blk.shape, h), blk, jnp.zeros_like(blk)))
        sink_rows.append(jnp.full((1, tq, LANES), sink_ref[h] * LOG2E, F32))
    sinks = jnp.concatenate(sink_rows, axis=0)
    k_win = jnp.concatenate([kp_ref[...], kc_ref[...]], axis=0)
    v_win = jnp.concatenate([vp_ref[...], vc_ref[...]], axis=0)
    s = _nt_dot(jnp.concatenate(q_heads, axis=0), k_win).reshape(nh, tq, 2 * LANES) + bias_ref[...] + madd[None]
    m = jnp.maximum(jnp.max(s, axis=2, keepdims=True), sinks)
    p = jnp.exp2(s - jnp.concatenate([m, m], axis=2))
    l = jnp.sum(p, axis=2, keepdims=True) + jnp.exp2(sinks - m)
    out = _dot(p.astype(BF16).reshape(nh * tq, 2 * LANES), v_win).reshape(nh, tq, LANES) / l
    _store_pairs(o_ref, [out[h] for h in range(nh)])


def _mixer_swa(p, sinks, bias, batch, seq):
    t = p.shape[0]
    tq = SWA_TQ
    nq = seq // tq
    cw = PT // LANES
    cur = lambda col: (lambda b, i: (b * nq + i, col))
    prev = lambda col: (lambda b, i: (b * nq + jnp.maximum(i - 1, 0), col))
    return pl.pallas_call(
        _swa_kernel,
        grid=(batch, nq),
        in_specs=[
            pl.BlockSpec(memory_space=pltpu.SMEM),
            pl.BlockSpec((tq, PT), lambda b, i: (b * nq + i, T_BQ)),
            pl.BlockSpec((tq, LANES), prev(T_K2 * cw + 1)),
            pl.BlockSpec((tq, LANES), cur(T_K2 * cw + 1)),
            pl.BlockSpec((tq, LANES), prev(T_MISC * cw + 1)),
            pl.BlockSpec((tq, LANES), cur(T_MISC * cw + 1)),
            pl.BlockSpec((GROUP_HEADS, tq, 2 * LANES), lambda b, i: (0, 0, 0)),
        ],
        out_specs=pl.BlockSpec((tq, GROUP_WIDTH), lambda b, i: (b * nq + i, 0)),
        out_shape=jax.ShapeDtypeStruct((t, GROUP_WIDTH), F32),
        compiler_params=_cparams(("arbitrary", "arbitrary")),
        name="mixer_swa",
    )(sinks, p, p, p, p, p, bias)


def _logsig(x):
    return jnp.minimum(x, 0.0) - jnp.log(1.0 + jnp.exp(-jnp.abs(x)))


def _foxcum_kernel(fb_ref, f_ref, tri_ref, o_ref, *, nchunk):
    def body(c, carry):
        start = pl.multiple_of(c * LANES, LANES)
        lf = _logsig(f_ref[pl.ds(start, LANES), :] + fb_ref[...]).T
        p1 = lf.astype(BF16)
        r1 = lf - p1.astype(F32)
        p2 = r1.astype(BF16)
        p3 = (r1 - p2.astype(F32)).astype(BF16)
        tri = tri_ref[...]
        cum = _dot(p1, tri) + _dot(p2, tri) + _dot(p3, tri) + carry
        o_ref[0, c] = cum[:GROUP_HEADS]
        return jnp.broadcast_to(cum[:, LANES - 1:LANES], cum.shape)

    lax.fori_loop(0, nchunk, body, jnp.zeros((LANES, LANES), F32))


def _fox_cum(side, fb_row, tri_incl, batch, seq):
    nchunk = seq // LANES
    return pl.pallas_call(
        functools.partial(_foxcum_kernel, nchunk=nchunk),
        grid=(batch,),
        in_specs=[
            pl.BlockSpec((1, LANES), lambda b: (0, 0)),
            pl.BlockSpec((seq, LANES), lambda b: (b, 0)),
            pl.BlockSpec((LANES, LANES), lambda b: (0, 0)),
        ],
        out_specs=pl.BlockSpec((1, nchunk, GROUP_HEADS, LANES), lambda b: (b, 0, 0, 0)),
        out_shape=jax.ShapeDtypeStruct((batch, nchunk, GROUP_HEADS, LANES), F32),
        compiler_params=_cparams(("arbitrary",)),
        name="fox_cumsum",
    )(fb_row, side, tri_incl)


FOX_TQ = 256
FOX_KC = 512
FOX_SUB = FOX_KC // LANES


def _fox_kernel(q_ref, g_ref, k_ref, v_ref, cum_ref, o_ref, qm_ref, m_ref, acc_ref):
    i = pl.program_id(1)
    tq = FOX_TQ
    for h in range(GROUP_HEADS):
        blk = q_ref[:, (h // 2) * LANES:(h // 2 + 1) * LANES]
        qm_ref[h] = jnp.where(_own_lane_mask(blk.shape, h), blk, jnp.zeros_like(blk))
    m_ref[...] = jnp.full(m_ref.shape, NEG, F32)
    acc_ref[...] = jnp.zeros(acc_ref.shape, F32)
    f_ref0 = cum_ref[0, i * (tq // LANES)][:, 0:1]

    def chunk(c, masked):
        start = pl.multiple_of(c * FOX_KC, FOX_KC)
        cum = cum_ref[0, pl.ds(c * FOX_SUB, FOX_SUB)]
        cum = jnp.concatenate([cum[u] for u in range(FOX_SUB)], axis=1)
        fbias = (f_ref0 - cum) * LOG2E
        per_head = []
        for pr in range(PAIRS):
            pair = slice(pr * LANES, (pr + 1) * LANES)
            q_pair = qm_ref[2 * pr:2 * pr + 2].reshape(2 * tq, LANES)
            s_pair = _nt_dot(q_pair, k_ref[pl.ds(start, FOX_KC), pair])
            for e in range(2):
                h = 2 * pr + e
                per_head.append(s_pair[e * tq:(e + 1) * tq] + fbias[h:h + 1, :])
        s = jnp.stack(per_head)
        if masked:
            shape = (GROUP_HEADS, tq, FOX_KC)
            ok = start + lax.broadcasted_iota(I32, shape, 2) <= i * tq + lax.broadcasted_iota(I32, shape, 1)
            s = jnp.where(ok, s, NEG)
        reps = FOX_KC // LANES
        m_prev = m_ref[...]
        m_new = jnp.maximum(m_prev, jnp.max(s, axis=2, keepdims=True))
        p = jnp.exp2((s - jnp.concatenate([m_new] * reps, axis=2)).astype(BF16))
        alpha = jnp.exp2(m_prev - m_new)
        m_ref[...] = m_new
        pvs = []
        for h in range(GROUP_HEADS):
            v_pair = v_ref[pl.ds(start, FOX_KC), (h // 2) * LANES:(h // 2 + 1) * LANES]
            pvs.append(_dot(p[h], _with_ones(v_pair, ones_low=(h % 2 == 1))))
        acc_ref[...] = alpha * acc_ref[...] + jnp.stack(pvs)

    def body(c, carry):
        chunk(c, False)
        return carry

    n_full = (i * tq) // FOX_KC
    lax.fori_loop(0, n_full, body, 0)
    chunk(n_full, True)
    outs = []
    for h in range(GROUP_HEADS):
        gate = _sigmoid(g_ref[:, (h // 2) * LANES:(h // 2 + 1) * LANES].astype(F32))
        outs.append(_normalize(acc_ref[h]) * gate)
    _store_pairs(o_ref, outs)


def _mixer_fox(p, cum, batch, seq):
    t = p.shape[0]
    tq = FOX_TQ
    nq = seq // tq
    return pl.pallas_call(
        _fox_kernel,
        grid=(batch, nq),
        in_specs=[
            pl.BlockSpec((tq, PT), lambda b, i: (b * nq + i, T_CQ)),
            pl.BlockSpec((tq, PT), lambda b, i: (b * nq + i, T_CG)),
            pl.BlockSpec((seq, PT), lambda b, i: (b, T_CK)),
            pl.BlockSpec((seq, PT), lambda b, i: (b, T_CV)),
            pl.BlockSpec((1, seq // LANES, GROUP_HEADS, LANES), lambda b, i: (b, 0, 0, 0)),
        ],
        out_specs=pl.BlockSpec((tq, GROUP_WIDTH), lambda b, i: (b * nq + i, 0)),
        out_shape=jax.ShapeDtypeStruct((t, GROUP_WIDTH), F32),
        scratch_shapes=[
            pltpu.VMEM((GROUP_HEADS, tq, LANES), BF16),
            pltpu.VMEM((GROUP_HEADS, tq, LANES), F32),
            pltpu.VMEM((GROUP_HEADS, tq, LANES), F32),
        ],
        compiler_params=_cparams(("arbitrary", "arbitrary")),
        name="mixer_fox",
    )(p, p, p, p, cum)


STK_TQ = 128


def _stick_kernel(q_ref, k_ref, v_ref, tri_ref, o_ref, qm_ref, acc_ref, r_ref):
    i = pl.program_id(1)
    tq = STK_TQ
    nh = GROUP_HEADS
    for h in range(nh):
        blk = q_ref[:, (h // 2) * LANES:(h // 2 + 1) * LANES]
        qm_ref[h] = jnp.where(_own_lane_mask(blk.shape, h), blk, jnp.zeros_like(blk))
    acc_ref[...] = jnp.zeros(acc_ref.shape, F32)
    r_ref[...] = jnp.zeros(r_ref.shape, F32)
    before = (lax.broadcasted_iota(I32, (nh, tq, LANES), 2) < lax.broadcasted_iota(I32, (nh, tq, LANES), 1))
    tri = tri_ref[...]

    def block(j, masked):
        start = pl.multiple_of(j * LANES, LANES)
        zs = []
        for pr in range(PAIRS):
            q_pair = qm_ref[2 * pr:2 * pr + 2].reshape(2 * tq, LANES)
            zs.append(_nt_dot(q_pair, k_ref[pl.ds(start, LANES), pr * LANES:(pr + 1) * LANES]))
        z = jnp.concatenate(zs, axis=0).reshape(nh, tq, LANES)
        lsz = _logsig(z)
        u = lsz - z
        if masked:
            u = jnp.where(before, u, 0.0)
        u2 = u.reshape(nh * tq, LANES)
        u_hi = u2.astype(BF16)
        u_lo = (u2 - u_hi.astype(F32)).astype(BF16)
        nearer = (_dot(u_hi, tri) + _dot(u_lo, tri)).reshape(nh, tq, LANES)
        run = r_ref[...]
        w = jnp.exp(lsz + nearer + run)
        if masked:
            w = jnp.where(before, w, 0.0)
        wb = w.astype(BF16)
        pvs = []
        for pr in range(PAIRS):
            w_pair = wb[2 * pr:2 * pr + 2].reshape(2 * tq, LANES)
            pvs.append(_dot(w_pair, v_ref[pl.ds(start, LANES), pr * LANES:(pr + 1) * LANES]))
        acc_ref[...] += jnp.concatenate(pvs, axis=0).reshape(nh, tq, LANES)
        run = run + jnp.sum(u, axis=2, keepdims=True)
        r_ref[...] = run
        return jnp.max(run)

    rmax = block(i, True)

    def cond(carry):
        j, rmax = carry
        return jnp.logical_and(j >= 0, rmax >= STICK_EXIT)

    def body(carry):
        j, _ = carry
        return j - 1, block(j, False)

    lax.while_loop(cond, body, (i - 1, rmax))
    _store_pairs(o_ref, [acc_ref[h] for h in range(nh)])


def _mixer_stick(p, tri_excl, batch, seq):
    t = p.shape[0]
    tq = STK_TQ
    nq = seq // tq
    return pl.pallas_call(
        _stick_kernel,
        grid=(batch, nq),
        in_specs=[
            pl.BlockSpec((tq, PT), lambda b, i: (b * nq + i, T_DQ)),
            pl.BlockSpec((seq, PT), lambda b, i: (b, T_DK)),
            pl.BlockSpec((seq, PT), lambda b, i: (b, T_DV)),
            pl.BlockSpec((LANES, LANES), lambda b, i: (0, 0)),
        ],
        out_specs=pl.BlockSpec((tq, GROUP_WIDTH), lambda b, i: (b * nq + i, 0)),
        out_shape=jax.ShapeDtypeStruct((t, GROUP_WIDTH), F32),
        scratch_shapes=[pltpu.VMEM((GROUP_HEADS, tq, LANES), BF16),
                        pltpu.VMEM((GROUP_HEADS, tq, LANES), F32),
                        pltpu.VMEM((GROUP_HEADS, tq, LANES), F32)],
        compiler_params=_cparams(("arbitrary", "arbitrary")),
        name="mixer_stick",
    )(p, p, p, tri_excl)


def _outproj_kernel(x_ref, gt_ref, oa_ref, ob_ref, oc_ref, od_ref, gg_ref, w_ref, o_ref):
    acc = None
    for m, ref in enumerate((oa_ref, ob_ref, oc_ref, od_ref)):
        o = ref[...]
        y = o * lax.rsqrt(jnp.mean(o * o, axis=-1, keepdims=True) + EPS) * gg_ref[m:m + 1, :]
        part = _dot(y.astype(BF16), w_ref[m * GROUP_WIDTH:(m + 1) * GROUP_WIDTH, :])
        acc = part if acc is None else acc + part
    o_ref[...] = x_ref[...] + gt_ref[0] * acc


def _outproj(x, gt, outs, gg, w, l, seq):
    t, d = x.shape
    tm = 512
    per_b = seq // tm
    mix = pl.BlockSpec((tm, GROUP_WIDTH), lambda i: (i, 0))
    return pl.pallas_call(
        _outproj_kernel,
        grid=(t // tm,),
        in_specs=[
            pl.BlockSpec((tm, d), lambda i: (i, 0)),
            pl.BlockSpec((1, 1, d), lambda i: (i // per_b, 0, 0)),
            mix, mix, mix, mix,
            pl.BlockSpec((None, N_MIXERS, GROUP_WIDTH), lambda i: (l, 0, 0)),
            pl.BlockSpec((None, N_MIXERS * GROUP_WIDTH, d), lambda i: (l, 0, 0)),
        ],
        out_specs=pl.BlockSpec((tm, d), lambda i: (i, 0)),
        out_shape=jax.ShapeDtypeStruct((t, d), F32),
        compiler_params=_cparams(("arbitrary",)),
        name="outproj",
    )(x, gt, *outs, gg, w)


def _prep_in_weights(w_in, qk_g):
    depth, d, _ = w_in.shape
    sizes = (GROUP_WIDTH, HEAD_DIM, HEAD_DIM, IDX_HEADS * IDX_DIM, IDX_DIM, IDX_HEADS,
             GROUP_WIDTH, HEAD_DIM, HEAD_DIM,
             GROUP_WIDTH, GROUP_WIDTH, GROUP_WIDTH, GROUP_HEADS, GROUP_WIDTH,
             GROUP_WIDTH, GROUP_WIDTH, GROUP_WIDTH)
    pts = np.cumsum(sizes)[:-1].tolist()
    (a_q, a_k, a_v, a_iq, a_ik, a_iw, b_q, b_k, b_v,
     c_q, c_k, c_v, c_f, c_g, d_q, d_k, d_v) = jnp.split(w_in, pts, axis=-1)
    z = lambda n: jnp.zeros((depth, d, n), w_in.dtype)
    qscale = HEAD_DIM ** -0.5
    main = jnp.concatenate([
        a_q, b_q, c_q, c_k,
        a_k, a_k, b_k, b_k, z(2 * LANES),
        c_v, c_g, d_q * qscale, d_k, d_v, a_iq * (IDX_DIM ** -0.5),
        a_v, a_v, b_v, b_v, a_ik, a_ik, z(LANES),
    ], axis=-1).astype(BF16)
    side = jnp.concatenate([c_f, a_iw * (IDX_HEADS ** -0.5), z(LANES - GROUP_HEADS - IDX_HEADS)],
                           axis=-1).astype(BF16)
    rep = lambda g, n: jnp.tile(g, (1, n))
    sscale = qscale * LOG2E
    gain = jnp.concatenate([
        rep(qk_g[:, 0], 8) * sscale, rep(qk_g[:, 2], 8) * sscale, rep(qk_g[:, 4], 8) * sscale,
        rep(qk_g[:, 5], 8), rep(qk_g[:, 1], 2), rep(qk_g[:, 3], 2),
        jnp.zeros((depth, P_COLS - 4 * PT - 2 * LANES), F32),
    ], axis=-1).reshape(depth, 1, P_COLS)
    return main, side, gain


def kernel(x, c, w_ada, b_ada, norm_g, w_in, qk_g, forget_b, sinks, rel_table, group_g, w_out,
           w_ffn_gate, w_ffn_up, w_ffn_down):
    batch, seq, d = x.shape
    depth = w_ada.shape[0]
    t = batch * seq

    mod = _modulation(c, w_ada, b_ada)
    wg = w_ffn_gate.astype(BF16)
    wu = w_ffn_up.astype(BF16)
    wd = w_ffn_down.astype(BF16)
    wo = w_out.astype(BF16)
    w_main, w_side, gain = _prep_in_weights(w_in, qk_g)
    gain_side = jnp.zeros((depth, 1, LANES), F32)

    r = np.arange(PT)
    bd = jnp.asarray((r[:, None] // HEAD_DIM == r[None, :] // HEAD_DIM).astype(np.float32) / HEAD_DIM, BF16)
    bd_side = jnp.zeros((LANES, LANES), BF16)
    r = np.arange(LANES)
    tri_incl = jnp.asarray(r[:, None] <= r[None, :], BF16)
    tri_after = jnp.asarray(r[:, None] > r[None, :], BF16)
    r = np.arange(DSA_KC)
    tri_before = jnp.asarray(r[:, None] < r[None, :], BF16)

    bias = _bias_tiles(rel_table, DSA_TQ)
    bias_dsa = _dsa_window_bias(bias)
    bias_swa = _swa_window_bias(bias)
    fb_rows = jnp.zeros((depth, 1, LANES), F32).at[:, 0, :GROUP_HEADS].set(forget_b)
    gg = group_g.reshape(depth, N_MIXERS, GROUP_WIDTH)

    xt = x.reshape(t, d)
    for l in range(depth):
        parts = [m.reshape(batch, 1, d) for m in jnp.split(mod[l], 9, axis=-1)]
        sh1, sc1, g1, sh2, sc2, g2, sh3, sc3, g3 = parts
        xt = _ffn(xt, norm_g[l, 0:1], sh1, sc1, g1, wg, wu, wd, l, 0, seq)
        ng2 = norm_g[l, 1:2]
        p = _inproj(xt, ng2, sh2, sc2, w_main, gain, bd, l, seq, N_NORM_TILES, BF16)
        side = _inproj(xt, ng2, sh2, sc2, w_side, gain_side, bd_side, l, seq, 0, F32)
        o_a = _mixer_dsa(p, side, bias_dsa, tri_before, batch, seq)
        o_b = _mixer_swa(p, sinks[l], bias_swa, batch, seq)
        cum = _fox_cum(side, fb_rows[l], tri_incl, batch, seq)
        o_c = _mixer_fox(p, cum, batch, seq)
        o_d = _mixer_stick(p, tri_after, batch, seq)
        xt = _outproj(xt, g2, (o_a, o_b, o_c, o_d), gg, wo, l, seq)
        xt = _ffn(xt, norm_g[l, 2:3], sh3, sc3, g3, wg, wu, wd, l, 1, seq)
    return xt.reshape(batch, seq, d)
```

```python
import functools
import math

import numpy as np
import jax
import jax.numpy as jnp
from jax import lax
from jax.experimental import pallas as pl
from jax.experimental.pallas import tpu as pltpu

F32 = jnp.float32
BF16 = jnp.bfloat16
I32 = jnp.int32

HEAD_DIM = 64
N_MIXERS = 4
GROUP_HEADS = 8
GROUP_WIDTH = GROUP_HEADS * HEAD_DIM
IDX_HEADS = 16
IDX_DIM = 64
TOPK_MAX = 256
WINDOW = 128
REL_BUCKETS = 32
REL_MAX_DIST = 128
EPS = 1e-6

LANES = 128
PAIRS = GROUP_HEADS // 2
NEG = -1e30
LOG2E = math.log2(math.e)
INT_MIN = -2 ** 31
STICK_EXIT = -110.0
VMEM_LIMIT = 56 * 1024 * 1024

PT = 512
T_AQ, T_BQ, T_CQ, T_CK, T_K2 = 0, 1, 2, 3, 4
N_NORM_TILES = 5
T_CV, T_CG, T_DQ, T_DK, T_DV, T_IQ, T_MISC = 5, 6, 7, 8, 9, 10, 12
N_TILES = 13
P_COLS = N_TILES * PT


def _cparams(sem, vmem=VMEM_LIMIT):
    return pltpu.CompilerParams(dimension_semantics=sem, vmem_limit_bytes=vmem)


def _nt_dot(a, b):
    return lax.dot_general(a, b, (((1,), (1,)), ((), ())), preferred_element_type=F32)


def _dot(a, b):
    return jnp.dot(a, b, preferred_element_type=F32)


def _sigmoid(x):
    return 1.0 / (1.0 + jnp.exp(-x))


def _lane_tile(x, n):
    return x if n == 1 else jnp.concatenate([x] * n, axis=1)


def _own_lane_mask(shape, h):
    lane = lax.broadcasted_iota(I32, shape, 1)
    return (lane < HEAD_DIM) if h % 2 == 0 else (lane >= HEAD_DIM)


def _mod_kernel(c_ref, w_ref, b_ref, o_ref):
    c = c_ref[...]
    cond = (c * _sigmoid(c)).astype(BF16)
    o_ref[0] = _dot(cond, w_ref[0].astype(BF16)) + b_ref[0]


def _modulation(c, w_ada, b_ada):
    depth, d, n = w_ada.shape
    b = c.shape[0]
    rows = 8
    tn = 1024
    c_pad = jnp.zeros((rows, d), F32).at[:b].set(c)
    out = pl.pallas_call(
        _mod_kernel,
        grid=(depth, n // tn),
        in_specs=[
            pl.BlockSpec((rows, d), lambda l, j: (0, 0)),
            pl.BlockSpec((1, d, tn), lambda l, j: (l, 0, j)),
            pl.BlockSpec((1, 1, tn), lambda l, j: (l, 0, j)),
        ],
        out_specs=pl.BlockSpec((1, rows, tn), lambda l, j: (l, 0, j)),
        out_shape=jax.ShapeDtypeStruct((depth, rows, n), F32),
        compiler_params=_cparams(("arbitrary", "arbitrary")),
        name="adaln_mod",
    )(c_pad, w_ada, b_ada.reshape(depth, 1, n))
    return out[:, :b]


def _norm_modulate(x, ng, sh, sc):
    ms = jnp.mean(x * x, axis=-1, keepdims=True)
    y = x * lax.rsqrt(ms + EPS) * ng
    return y * (1.0 + sc) + sh


def _ffn_kernel(x_ref, ng_ref, sh_ref, sc_ref, gt_ref, wg_ref, wu_ref, wd_ref, o_ref, h_ref, a_ref):
    f = pl.program_id(1)
    last = pl.num_programs(1) - 1

    def activation():
        h = h_ref[...]
        g = _dot(h, wg_ref[...])
        u = _dot(h, wu_ref[...])
        return (g * _sigmoid(g) * u).astype(BF16)

    @pl.when(f == 0)
    def _():
        h = _norm_modulate(x_ref[...], ng_ref[...], sh_ref[0], sc_ref[0])
        h_ref[...] = h.astype(BF16)
        o_ref[...] = jnp.zeros_like(o_ref)
        a_ref[...] = activation()

    @pl.when(jnp.logical_and(f > 0, f < last))
    def _():
        o_ref[...] += _dot(a_ref[...], wd_ref[...])
        a_ref[...] = activation()

    @pl.when(f == last)
    def _():
        acc = o_ref[...] + _dot(a_ref[...], wd_ref[...])
        o_ref[...] = x_ref[...] + 0.5 * gt_ref[0] * acc


def _ffn(x, ng, sh, sc, gt, wg, wu, wd, l, s, seq):
    t, d = x.shape
    tm, tf = min(1024, seq), 512
    nf = wg.shape[-1] // tf
    per_b = seq // tm
    vec = pl.BlockSpec((1, 1, d), lambda i, f: (i // per_b, 0, 0))
    up = lambda i, f: (l, s, 0, jnp.minimum(f, nf - 1))
    return pl.pallas_call(
        _ffn_kernel,
        grid=(t // tm, nf + 1),
        in_specs=[
            pl.BlockSpec((tm, d), lambda i, f: (i, 0), pipeline_mode=pl.Buffered(1)),
            pl.BlockSpec((1, d), lambda i, f: (0, 0)),
            vec, vec, vec,
            pl.BlockSpec((None, None, d, tf), up),
            pl.BlockSpec((None, None, d, tf), up),
            pl.BlockSpec((None, None, tf, d), lambda i, f: (l, s, jnp.maximum(f - 1, 0), 0)),
        ],
        out_specs=pl.BlockSpec((tm, d), lambda i, f: (i, 0)),
        out_shape=jax.ShapeDtypeStruct((t, d), F32),
        scratch_shapes=[pltpu.VMEM((tm, d), BF16), pltpu.VMEM((tm, tf), BF16)],
        compiler_params=_cparams(("arbitrary", "arbitrary")),
        name="ffn",
    )(x, ng, sh, sc, gt, wg, wu, wd)


def _inproj_kernel(x_ref, ng_ref, sh_ref, sc_ref, w_ref, gain_ref, bd_ref, o_ref, h_ref, *, n_norm):
    j = pl.program_id(1)

    @pl.when(j == 0)
    def _():
        h = _norm_modulate(x_ref[...], ng_ref[...], sh_ref[0], sc_ref[0])
        h_ref[...] = h.astype(BF16)

    y = _dot(h_ref[...], w_ref[...])

    @pl.when(j < n_norm)
    def _():
        sq = y * y
        hi = sq.astype(BF16)
        lo = (sq - hi.astype(F32)).astype(BF16)
        ms = _dot(hi, bd_ref[...]) + _dot(lo, bd_ref[...])
        o_ref[...] = (y * lax.rsqrt(ms + EPS) * gain_ref[...]).astype(o_ref.dtype)

    @pl.when(j >= n_norm)
    def _():
        o_ref[...] = y.astype(o_ref.dtype)


def _inproj(x, ng, sh, sc, w, gain, bd, l, seq, n_norm, out_dtype):
    t, d = x.shape
    n = w.shape[-1]
    tm = min(1024, seq)
    tn = bd.shape[0]
    per_b = seq // tm
    vec = pl.BlockSpec((1, 1, d), lambda i, j: (i // per_b, 0, 0))
    return pl.pallas_call(
        functools.partial(_inproj_kernel, n_norm=n_norm),
        grid=(t // tm, n // tn),
        in_specs=[
            pl.BlockSpec((tm, d), lambda i, j: (i, 0)),
            pl.BlockSpec((1, d), lambda i, j: (0, 0)),
            vec, vec,
            pl.BlockSpec((None, d, tn), lambda i, j: (l, 0, j)),
            pl.BlockSpec((None, 1, tn), lambda i, j: (l, 0, j)),
            pl.BlockSpec((tn, tn), lambda i, j: (0, 0)),
        ],
        out_specs=pl.BlockSpec((tm, tn), lambda i, j: (i, j)),
        out_shape=jax.ShapeDtypeStruct((t, n), out_dtype),
        scratch_shapes=[pltpu.VMEM((tm, d), BF16)],
        compiler_params=_cparams(("arbitrary", "arbitrary")),
        name="inproj",
    )(x, ng, sh, sc, w, gain, bd)


def _rel_bucket_np(dist):
    n = np.maximum(dist, 0)
    max_exact = REL_BUCKETS // 2
    nf = np.maximum(n, 1).astype(np.float32)
    large = max_exact + (np.log(nf / np.float32(max_exact)) / np.float32(math.log(REL_MAX_DIST / max_exact))
                         * np.float32(REL_BUCKETS - max_exact)).astype(np.int32)
    large = np.minimum(large, REL_BUCKETS - 1)
    return np.where(n < max_exact, n, large).astype(np.int32)


def _bias_kernel(tab_ref, bkt_ref, o_ref, *, sub_far):
    h = pl.program_id(0)
    far = tab_ref[REL_BUCKETS - 1, h]
    for which in range(2):
        bkt = bkt_ref[which]
        acc = jnp.zeros(bkt.shape, F32)
        for b in range(REL_BUCKETS):
            acc = jnp.where(bkt == b, tab_ref[b, h], acc)
        o_ref[0, which] = (acc - jnp.where(h < sub_far, far, 0.0)) * LOG2E


def _bias_windows(rel_table, tq):
    width = tq + LANES
    t_l = np.arange(tq)[:, None]
    w = np.arange(width)[None, :]
    bkt = np.stack([_rel_bucket_np(t_l - w), _rel_bucket_np(t_l + LANES - w)]).astype(np.int32)
    nh = rel_table.shape[1]
    return pl.pallas_call(
        functools.partial(_bias_kernel, sub_far=GROUP_HEADS),
        grid=(nh,),
        in_specs=[
            pl.BlockSpec(memory_space=pltpu.SMEM),
            pl.BlockSpec((2, tq, width), lambda h: (0, 0, 0)),
        ],
        out_specs=pl.BlockSpec((1, 2, tq, width), lambda h: (h, 0, 0, 0)),
        out_shape=jax.ShapeDtypeStruct((nh, 2, tq, width), F32),
        compiler_params=_cparams(("arbitrary",)),
        name="rel_bias_windows",
    )(rel_table, jnp.asarray(bkt))


def _softmax_step_stacked(s, v, m_ref, l_ref, acc_ref):
    nh, rows, width = s.shape
    reps = width // LANES
    m_prev = m_ref[...]
    m_new = jnp.maximum(m_prev, jnp.max(s, axis=2, keepdims=True))
    p = jnp.exp2(s - jnp.concatenate([m_new] * reps, axis=2))
    alpha = jnp.exp2(m_prev - m_new)
    l_ref[...] = alpha * l_ref[...] + jnp.sum(p, axis=2, keepdims=True)
    m_ref[...] = m_new
    pv = _dot(p.astype(BF16).reshape(nh * rows, width), v).reshape(nh, rows, LANES)
    acc_ref[...] = alpha * acc_ref[...] + pv


def _store_pairs(o_ref, per_head):
    for j in range(PAIRS):
        lane = lax.broadcasted_iota(I32, per_head[0].shape, 1)
        o_ref[:, j * LANES:(j + 1) * LANES] = jnp.where(
            lane < HEAD_DIM, per_head[2 * j], per_head[2 * j + 1]).astype(o_ref.dtype)


DSA_TQ = 256
DSA_NB = DSA_TQ // LANES
DSA_KC1 = 512
DSA_KC = 512
DSA_SUB = DSA_KC // LANES
DSA_VALUE_STEPS = 26
DSA_VALUE_BLIND_STEPS = 10


def _dsa_kernel(q_ref, iq_ref, iw_ref, k2_ref, v2_ref, ik2_ref, bias_ref, tri_ref, o_ref,
                key_ref, wb_ref, iqm_ref, qm_ref, t_ref, cand_ref, m_ref, l_ref, acc_ref, *, k_top):
    i = pl.program_id(1)
    tq = DSA_TQ
    row0 = i * tq
    n_valid_cols = row0 + tq

    iw = iw_ref[...]
    for h in range(IDX_HEADS):
        col = GROUP_HEADS + h
        wb_ref[h] = jnp.broadcast_to(iw[:, col:col + 1], (tq, LANES))
        blk = iq_ref[:, (h // 2) * LANES:(h // 2 + 1) * LANES]
        iqm_ref[h] = jnp.where(_own_lane_mask(blk.shape, h), blk, jnp.zeros_like(blk))
    for h in range(GROUP_HEADS):
        blk = q_ref[:, (h // 2) * LANES:(h // 2 + 1) * LANES]
        qm_ref[h] = jnp.where(_own_lane_mask(blk.shape, h), blk, jnp.zeros_like(blk))

    qpos = row0 + lax.broadcasted_iota(I32, (tq, DSA_KC1), 0)
    iq_all = iqm_ref[...].reshape(IDX_HEADS * tq, LANES)
    wb_all = jnp.concatenate([wb_ref[...]] * (DSA_KC1 // LANES), axis=2)
    n1 =(n_valid_cols + DSA_KC1 - 1) // DSA_KC1

    def score_chunk(c, carry):
        start = pl.multiple_of(c * DSA_KC1, DSA_KC1)
        ik = ik2_ref[pl.ds(start, DSA_KC1), :]
        r = _nt_dot(iq_all, ik).reshape(IDX_HEADS, tq, DSA_KC1)
        score = jnp.sum(jnp.maximum(r, 0.0) * wb_all, axis=0)
        bits = lax.bitcast_convert_type(score, I32)
        key = bits ^ ((bits >> 31) & 0x7FFFFFFF)
        key = jnp.where(score == 0.0, 0, key)
        kpos = start + lax.broadcasted_iota(I32, (tq, DSA_KC1), 1)
        valid = kpos <= qpos
        key = jnp.where(valid, key, INT_MIN)
        smax, smin = carry
        hi = jnp.where(valid, score, -jnp.inf)
        lo = jnp.where(valid, score, jnp.inf)
        for u in range(DSA_KC1 // LANES):
            key_ref[c * (DSA_KC1 // LANES) + u] = key[:, u * LANES:(u + 1) * LANES]
            smax = jnp.maximum(smax, hi[:, u * LANES:(u + 1) * LANES])
            smin = jnp.minimum(smin, lo[:, u * LANES:(u + 1) * LANES])
        return smax, smin

    smax, smin = lax.fori_loop(0, n1, score_chunk, (jnp.full((tq, LANES), -jnp.inf, F32),
                                                    jnp.full((tq, LANES), jnp.inf, F32)))
    nsel = (n_valid_cols + DSA_KC - 1) // DSA_KC
    n1_blocks = n1 * (DSA_KC1 // LANES)

    @pl.when(n1_blocks < nsel * DSA_SUB)
    def _():
        for u in range(DSA_KC1 // LANES):
            key_ref[n1_blocks + u] = jnp.full((tq, LANES), INT_MIN, I32)

    def load_keys(c):
        blk = key_ref[pl.ds(c * DSA_SUB, DSA_SUB)]
        return jnp.concatenate([blk[u] for u in range(DSA_SUB)], axis=1)

    def count_ge(thr, strict):
        cand_ref[...] = thr

        def body(c, acc):
            out = []
            for r in range(DSA_NB):
                rows = slice(r * LANES, (r + 1) * LANES)
                cand = cand_ref[rows, :]
                a = acc[rows]
                for u in range(DSA_SUB):
                    keys = key_ref[c * DSA_SUB + u, rows, :]
                    hit = (keys > cand) if strict else (keys >= cand)
                    a = jnp.where(hit, a + 1.0, a)
                out.append(a)
            return jnp.concatenate(out, axis=0)
        acc = lax.fori_loop(0, nsel, body, jnp.zeros((tq, LANES), F32))
        return jnp.broadcast_to(jnp.sum(acc, axis=1, keepdims=True), (tq, LANES))

    t_ref[...] = jnp.full((tq, LANES), INT_MIN, I32)
    kf = float(k_top)

    def float_key(x):
        bits = lax.bitcast_convert_type(x, I32)
        return jnp.where(x == 0.0, 0, bits ^ ((bits >> 31) & 0x7FFFFFFF))

    def bisect_values():
        lower = jnp.broadcast_to(jnp.min(smin, axis=1, keepdims=True), (tq, LANES))
        upper = jnp.broadcast_to(jnp.max(smax, axis=1, keepdims=True), (tq, LANES))
        n_causal = (row0 + 1 + lax.broadcasted_iota(I32, (tq, LANES), 0)).astype(F32)
        keep_all = n_causal <= kf
        open0 = jnp.where(keep_all, 0.0, 1.0)

        def step(lower, upper, still_open):
            mid = 0.5 * lower + 0.5 * upper
            cnt = count_ge(float_key(mid), False)
            active = still_open > 0.0
            go_up = jnp.logical_and(active, cnt >= kf)
            go_down = jnp.logical_and(active, cnt < kf)
            lower = jnp.where(go_up, mid, lower)
            upper = jnp.where(go_down, mid, upper)
            still_open = jnp.where(jnp.logical_and(go_up, cnt == kf), 0.0, still_open)
            return lower, upper, still_open

        lower, upper, still_open = lax.fori_loop(
            0, DSA_VALUE_BLIND_STEPS, lambda _, c: step(*c), (lower, upper, open0))

        def cond(carry):
            it, _, _, still_open = carry
            return jnp.logical_and(it < DSA_VALUE_STEPS, jnp.max(still_open) > 0.0)

        def body(carry):
            it, lower, upper, still_open = carry
            return (it + 1,) + step(lower, upper, still_open)

        _, lower, _, still_open = lax.while_loop(
            cond, body, (jnp.int32(DSA_VALUE_BLIND_STEPS), lower, upper, still_open))
        return jnp.where(keep_all, INT_MIN, float_key(lower)), jnp.max(still_open) == 0.0

    def exact_select():
        def bit_step(it, carry):
            thr, cnt_thr = carry
            cand = thr + jnp.left_shift(jnp.int32(1), 31 - it)
            cnt = count_ge(cand, False)
            take = cnt >= kf
            return jnp.where(take, cand, thr), jnp.where(take, cnt, cnt_thr)

        thr0 = jnp.full((tq, LANES), INT_MIN, I32)
        cnt0 = jnp.full((tq, LANES), float(2 ** 24), F32)
        thr, cnt_thr = lax.fori_loop(0, 32, bit_step, (thr0, cnt0))
        t_ref[...] = thr

        excess = jnp.where(thr > INT_MIN, cnt_thr - kf, 0.0)

        @pl.when(jnp.max(excess) > 0.0)
        def _():
            need = kf - count_ge(thr, True)

            def tie_chunk(c, seen):
                keys = load_keys(c)
                eq = keys == _lane_tile(thr, DSA_SUB)
                eqf = jnp.where(eq, 1.0, 0.0)
                rank = _dot(eqf.astype(BF16), tri_ref[...]) + _lane_tile(seen, DSA_SUB)
                dropped = jnp.where(rank >= _lane_tile(need, DSA_SUB), INT_MIN, keys)
                keys = jnp.where(eq, dropped, keys)
                for u in range(DSA_SUB):
                    key_ref[c * DSA_SUB + u] = keys[:, u * LANES:(u + 1) * LANES]
                return seen + jnp.broadcast_to(jnp.sum(eqf, axis=1, keepdims=True), (tq, LANES))

            lax.fori_loop(0, nsel, tie_chunk, jnp.zeros((tq, LANES), F32))

    @pl.when(n_valid_cols > k_top)
    def _():
        thr_fast, finished = bisect_values()
        t_ref[...] = thr_fast

        @pl.when(jnp.logical_not(finished))
        def _():
            exact_select()

    thr = jnp.maximum(t_ref[...], INT_MIN + 1)

    m_ref[...] = jnp.full(m_ref.shape, NEG, F32)
    l_ref[...] = jnp.zeros(l_ref.shape, F32)
    acc_ref[...] = jnp.zeros(acc_ref.shape, F32)
    q_all = qm_ref[...].reshape(GROUP_HEADS * tq, LANES)

    def attend(k_blk, v_blk, madd, bias):
        width = k_blk.shape[0]
        s = _nt_dot(q_all, k_blk).reshape(GROUP_HEADS, tq, width) + madd[None]
        if bias is not None:
            s = s + bias
        _softmax_step_stacked(s, v_blk, m_ref, l_ref, acc_ref)

    nwin = DSA_NB + 1
    win_blk = jnp.maximum(i * DSA_NB - 1, 0)
    win_start = pl.multiple_of(win_blk * LANES, LANES)
    near_keys = key_ref[pl.ds(win_blk, nwin)]
    near_keys = jnp.concatenate([near_keys[u] for u in range(nwin)], axis=1)
    near_madd = jnp.where(near_keys >= _lane_tile(thr, nwin), 0.0, NEG)
    attend(k2_ref[pl.ds(win_start, nwin * LANES), :], v2_ref[pl.ds(win_start, nwin * LANES), :],
           near_madd, bias_ref[jnp.minimum(i, 1)])
    for u in range(nwin):
        key_ref[win_blk + u] = jnp.full((tq, LANES), INT_MIN, I32)

    n_far = (win_blk * LANES + DSA_KC - 1) // DSA_KC

    def far_chunk(c, carry):
        start = pl.multiple_of(c * DSA_KC, DSA_KC)
        madd = jnp.where(load_keys(c) >= _lane_tile(thr, DSA_SUB), 0.0, NEG)
        attend(k2_ref[pl.ds(start, DSA_KC), :], v2_ref[pl.ds(start, DSA_KC), :], madd, None)
        return carry

    lax.fori_loop(0, n_far, far_chunk, 0)
    _store_pairs(o_ref, [acc_ref[h] / l_ref[h] for h in range(GROUP_HEADS)])


def _mixer_dsa(p, side, bias, tri, batch, seq):
    t = p.shape[0]
    tq = DSA_TQ
    nq = seq // tq
    k_top = min(TOPK_MAX, seq // 4)
    cw = PT // LANES
    return pl.pallas_call(
        functools.partial(_dsa_kernel, k_top=k_top),
        grid=(batch, nq),
        in_specs=[
            pl.BlockSpec((tq, PT), lambda b, i: (b * nq + i, T_AQ)),
            pl.BlockSpec((tq, 2 * PT), lambda b, i: (b * nq + i, T_IQ // 2)),
            pl.BlockSpec((tq, LANES), lambda b, i: (b * nq + i, 0)),
            pl.BlockSpec((seq, LANES), lambda b, i: (b, T_K2 * cw), pipeline_mode=pl.Buffered(1)),
            pl.BlockSpec((seq, LANES), lambda b, i: (b, T_MISC * cw), pipeline_mode=pl.Buffered(1)),
            pl.BlockSpec((seq, LANES), lambda b, i: (b, T_MISC * cw + 2), pipeline_mode=pl.Buffered(1)),
            pl.BlockSpec((2, GROUP_HEADS, tq, tq + LANES), lambda b, i: (0, 0, 0, 0),
                         pipeline_mode=pl.Buffered(1)),
            pl.BlockSpec((DSA_KC, DSA_KC), lambda b, i: (0, 0), pipeline_mode=pl.Buffered(1)),
        ],
        out_specs=pl.BlockSpec((tq, GROUP_WIDTH), lambda b, i: (b * nq + i, 0)),
        out_shape=jax.ShapeDtypeStruct((t, GROUP_WIDTH), F32),
        scratch_shapes=[
            pltpu.VMEM((max(seq // LANES, DSA_SUB, DSA_NB + 1), tq, LANES), I32),
            pltpu.VMEM((IDX_HEADS, tq, LANES), F32),
            pltpu.VMEM((IDX_HEADS, tq, LANES), BF16),
            pltpu.VMEM((GROUP_HEADS, tq, LANES), BF16),
            pltpu.VMEM((tq, LANES), I32),
            pltpu.VMEM((tq, LANES), I32),
            pltpu.VMEM((GROUP_HEADS, tq, LANES), F32),
            pltpu.VMEM((GROUP_HEADS, tq, LANES), F32),
            pltpu.VMEM((GROUP_HEADS, tq, LANES), F32),
        ],
        compiler_params=_cparams(("arbitrary", "arbitrary")),
        name="mixer_dsa",
    )(p, p, side, p, p, p, bias, tri)


SWA_TQ = 128


def _swa_kernel(sink_ref, q_ref, kp_ref, kc_ref, vp_ref, vc_ref, bias_ref, o_ref):
    i = pl.program_id(1)
    tq = SWA_TQ
    nh = GROUP_HEADS
    t_l = lax.broadcasted_iota(I32, (tq, 2 * LANES), 0)
    w = lax.broadcasted_iota(I32, (tq, 2 * LANES), 1)
    first = jnp.where(i > 0, t_l + 1, LANES)
    madd = jnp.where(w >= first, jnp.where(w <= t_l + LANES, 0.0, NEG), NEG)
    q_heads = []
    sink_rows = []
    for h in range(nh):
        blk = q_ref[:, (h // 2) * LANES:(h // 2 + 1) * LANES]
        q_heads.append(jnp.where(_own_lane_mask(blk.shape, h), blk, jnp.zeros_like(blk)))
        sink_rows.append(jnp.full((1, tq, LANES), sink_ref[h] * LOG2E, F32))
    sinks = jnp.concatenate(sink_rows, axis=0)
    k_win = jnp.concatenate([kp_ref[...], kc_ref[...]], axis=0)
    v_win = jnp.concatenate([vp_ref[...], vc_ref[...]], axis=0)
    s = _nt_dot(jnp.concatenate(q_heads, axis=0), k_win).reshape(nh, tq, 2 * LANES) + bias_ref[...] + madd[None]
    m = jnp.maximum(jnp.max(s, axis=2, keepdims=True), sinks)
    p = jnp.exp2(s - jnp.concatenate([m, m], axis=2))
    l = jnp.sum(p, axis=2, keepdims=True) + jnp.exp2(sinks - m)
    out = _dot(p.astype(BF16).reshape(nh * tq, 2 * LANES), v_win).reshape(nh, tq, LANES) / l
    _store_pairs(o_ref, [out[h] for h in range(nh)])


def _mixer_swa(p, sinks, bias, batch, seq):
    t = p.shape[0]
    tq = SWA_TQ
    nq = seq // tq
    cw = PT // LANES
    cur = lambda col: (lambda b, i: (b * nq + i, col))
    prev = lambda col: (lambda b, i: (b * nq + jnp.maximum(i - 1, 0), col))
    return pl.pallas_call(
        _swa_kernel,
        grid=(batch, nq),
        in_specs=[
            pl.BlockSpec(memory_space=pltpu.SMEM),
            pl.BlockSpec((tq, PT), lambda b, i: (b * nq + i, T_BQ)),
            pl.BlockSpec((tq, LANES), prev(T_K2 * cw + 1)),
            pl.BlockSpec((tq, LANES), cur(T_K2 * cw + 1)),
            pl.BlockSpec((tq, LANES), prev(T_MISC * cw + 1)),
            pl.BlockSpec((tq, LANES), cur(T_MISC * cw + 1)),
            pl.BlockSpec((GROUP_HEADS, tq, 2 * LANES), lambda b, i: (0, 0, 0)),
        ],
        out_specs=pl.BlockSpec((tq, GROUP_WIDTH), lambda b, i: (b * nq + i, 0)),
        out_shape=jax.ShapeDtypeStruct((t, GROUP_WIDTH), F32),
        compiler_params=_cparams(("arbitrary", "arbitrary")),
        name="mixer_swa",
    )(sinks, p, p, p, p, p, bias)


def _logsig(x):
    return jnp.minimum(x, 0.0) - jnp.log(1.0 + jnp.exp(-jnp.abs(x)))


def _foxcum_kernel(fb_ref, f_ref, tri_ref, o_ref, *, nchunk):
    def body(c, carry):
        start = pl.multiple_of(c * LANES, LANES)
        lf = _logsig(f_ref[pl.ds(start, LANES), :] + fb_ref[...]).T
        p1 = lf.astype(BF16)
        r1 = lf - p1.astype(F32)
        p2 = r1.astype(BF16)
        p3 = (r1 - p2.astype(F32)).astype(BF16)
        tri = tri_ref[...]
        cum = _dot(p1, tri) + _dot(p2, tri) + _dot(p3, tri) + carry
        o_ref[0, c] = cum[:GROUP_HEADS]
        return jnp.broadcast_to(cum[:, LANES - 1:LANES], cum.shape)

    lax.fori_loop(0, nchunk, body, jnp.zeros((LANES, LANES), F32))


def _fox_cum(side, fb_row, tri_incl, batch, seq):
    nchunk = seq // LANES
    return pl.pallas_call(
        functools.partial(_foxcum_kernel, nchunk=nchunk),
        grid=(batch,),
        in_specs=[
            pl.BlockSpec((1, LANES), lambda b: (0, 0)),
            pl.BlockSpec((seq, LANES), lambda b: (b, 0)),
            pl.BlockSpec((LANES, LANES), lambda b: (0, 0)),
        ],
        out_specs=pl.BlockSpec((1, nchunk, GROUP_HEADS, LANES), lambda b: (b, 0, 0, 0)),
        out_shape=jax.ShapeDtypeStruct((batch, nchunk, GROUP_HEADS, LANES), F32),
        compiler_params=_cparams(("arbitrary",)),
        name="fox_cumsum",
    )(fb_row, side, tri_incl)


FOX_TQ = 256
FOX_KC = 512
FOX_SUB = FOX_KC // LANES


def _fox_kernel(q_ref, g_ref, k_ref, v_ref, cum_ref, o_ref, qm_ref, m_ref, l_ref, acc_ref):
    i = pl.program_id(1)
    tq = FOX_TQ
    for h in range(GROUP_HEADS):
        blk = q_ref[:, (h // 2) * LANES:(h // 2 + 1) * LANES]
        qm_ref[h] = jnp.where(_own_lane_mask(blk.shape, h), blk, jnp.zeros_like(blk))
    m_ref[...] = jnp.full(m_ref.shape, NEG, F32)
    l_ref[...] = jnp.zeros(l_ref.shape, F32)
    acc_ref[...] = jnp.zeros(acc_ref.shape, F32)
    f_ref0 = cum_ref[0, i * (tq // LANES)][:, 0:1]

    def chunk(c, masked):
        start = pl.multiple_of(c * FOX_KC, FOX_KC)
        cum = cum_ref[0, pl.ds(c * FOX_SUB, FOX_SUB)]
        cum = jnp.concatenate([cum[u] for u in range(FOX_SUB)], axis=1)
        fbias = (f_ref0 - cum) * LOG2E
        per_head = []
        for pr in range(PAIRS):
            pair = slice(pr * LANES, (pr + 1) * LANES)
            q_pair = qm_ref[2 * pr:2 * pr + 2].reshape(2 * tq, LANES)
            s_pair = _nt_dot(q_pair, k_ref[pl.ds(start, FOX_KC), pair])
            for e in range(2):
                h = 2 * pr + e
                per_head.append(s_pair[e * tq:(e + 1) * tq] + fbias[h:h + 1, :])
        s = jnp.stack(per_head)
        if masked:
            shape = (GROUP_HEADS, tq, FOX_KC)
            ok = start + lax.broadcasted_iota(I32, shape, 2) <= i * tq + lax.broadcasted_iota(I32, shape, 1)
            s = jnp.where(ok, s, NEG)
        reps = FOX_KC // LANES
        m_prev = m_ref[...]
        m_new = jnp.maximum(m_prev, jnp.max(s, axis=2, keepdims=True))
        p = jnp.exp2(s - jnp.concatenate([m_new] * reps, axis=2))
        alpha = jnp.exp2(m_prev - m_new)
        l_ref[...] = alpha * l_ref[...] + jnp.sum(p, axis=2, keepdims=True)
        m_ref[...] = m_new
        p = p.astype(BF16)
        pvs = []
        for pr in range(PAIRS):
            p_pair = p[2 * pr:2 * pr + 2].reshape(2 * tq, FOX_KC)
            pvs.append(_dot(p_pair, v_ref[pl.ds(start, FOX_KC), pr * LANES:(pr + 1) * LANES]))
        acc_ref[...] = alpha * acc_ref[...] + jnp.concatenate(pvs, axis=0).reshape(GROUP_HEADS, tq, LANES)

    def body(c, carry):
        chunk(c, False)
        return carry

    n_full = (i * tq) // FOX_KC
    lax.fori_loop(0, n_full, body, 0)
    chunk(n_full, True)
    outs = []
    for h in range(GROUP_HEADS):
        gate = _sigmoid(g_ref[:, (h // 2) * LANES:(h // 2 + 1) * LANES].astype(F32))
        outs.append(acc_ref[h] / l_ref[h] * gate)
    _store_pairs(o_ref, outs)


def _mixer_fox(p, cum, batch, seq):
    t = p.shape[0]
    tq = FOX_TQ
    nq = seq // tq
    return pl.pallas_call(
        _fox_kernel,
        grid=(batch, nq),
        in_specs=[
            pl.BlockSpec((tq, PT), lambda b, i: (b * nq + i, T_CQ)),
            pl.BlockSpec((tq, PT), lambda b, i: (b * nq + i, T_CG)),
            pl.BlockSpec((seq, PT), lambda b, i: (b, T_CK)),
            pl.BlockSpec((seq, PT), lambda b, i: (b, T_CV)),
            pl.BlockSpec((1, seq // LANES, GROUP_HEADS, LANES), lambda b, i: (b, 0, 0, 0)),
        ],
        out_specs=pl.BlockSpec((tq, GROUP_WIDTH), lambda b, i: (b * nq + i, 0)),
        out_shape=jax.ShapeDtypeStruct((t, GROUP_WIDTH), F32),
        scratch_shapes=[
            pltpu.VMEM((GROUP_HEADS, tq, LANES), BF16),
            pltpu.VMEM((GROUP_HEADS, tq, LANES), F32),
            pltpu.VMEM((GROUP_HEADS, tq, LANES), F32),
            pltpu.VMEM((GROUP_HEADS, tq, LANES), F32),
        ],
        compiler_params=_cparams(("arbitrary", "arbitrary")),
        name="mixer_fox",
    )(p, p, p, p, cum)


STK_TQ = 128


def _stick_kernel(q_ref, k_ref, v_ref, tri_ref, o_ref, qm_ref, acc_ref, r_ref):
    i = pl.program_id(1)
    tq = STK_TQ
    nh = GROUP_HEADS
    for h in range(nh):
        blk = q_ref[:, (h // 2) * LANES:(h // 2 + 1) * LANES]
        qm_ref[h] = jnp.where(_own_lane_mask(blk.shape, h), blk, jnp.zeros_like(blk))
    acc_ref[...] = jnp.zeros(acc_ref.shape, F32)
    r_ref[...] = jnp.zeros(r_ref.shape, F32)
    before = (lax.broadcasted_iota(I32, (nh, tq, LANES), 2) < lax.broadcasted_iota(I32, (nh, tq, LANES), 1))
    tri = tri_ref[...]

    def block(j, masked):
        start = pl.multiple_of(j * LANES, LANES)
        zs = []
        for pr in range(PAIRS):
            q_pair = qm_ref[2 * pr:2 * pr + 2].reshape(2 * tq, LANES)
            zs.append(_nt_dot(q_pair, k_ref[pl.ds(start, LANES), pr * LANES:(pr + 1) * LANES]))
        z = jnp.concatenate(zs, axis=0).reshape(nh, tq, LANES)
        lsz = _logsig(z)
        u = lsz - z
        if masked:
            u = jnp.where(before, u, 0.0)
        u2 = u.reshape(nh * tq, LANES)
        u_hi = u2.astype(BF16)
        u_lo = (u2 - u_hi.astype(F32)).astype(BF16)
        nearer = (_dot(u_hi, tri) + _dot(u_lo, tri)).reshape(nh, tq, LANES)
        run = r_ref[...]
        w = jnp.exp(lsz + nearer + run)
        if masked:
            w = jnp.where(before, w, 0.0)
        wb = w.astype(BF16)
        pvs = []
        for pr in range(PAIRS):
            w_pair = wb[2 * pr:2 * pr + 2].reshape(2 * tq, LANES)
            pvs.append(_dot(w_pair, v_ref[pl.ds(start, LANES), pr * LANES:(pr + 1) * LANES]))
        acc_ref[...] += jnp.concatenate(pvs, axis=0).reshape(nh, tq, LANES)
        run = run + jnp.sum(u, axis=2, keepdims=True)
        r_ref[...] = run
        return jnp.max(run)

    rmax = block(i, True)

    def cond(carry):
        j, rmax = carry
        return jnp.logical_and(j >= 0, rmax >= STICK_EXIT)

    def body(carry):
        j, _ = carry
        return j - 1, block(j, False)

    lax.while_loop(cond, body, (i - 1, rmax))
    _store_pairs(o_ref, [acc_ref[h] for h in range(nh)])


def _mixer_stick(p, tri_excl, batch, seq):
    t = p.shape[0]
    tq = STK_TQ
    nq = seq // tq
    return pl.pallas_call(
        _stick_kernel,
        grid=(batch, nq),
        in_specs=[
            pl.BlockSpec((tq, PT), lambda b, i: (b * nq + i, T_DQ)),
            pl.BlockSpec((seq, PT), lambda b, i: (b, T_DK)),
            pl.BlockSpec((seq, PT), lambda b, i: (b, T_DV)),
            pl.BlockSpec((LANES, LANES), lambda b, i: (0, 0)),
        ],
        out_specs=pl.BlockSpec((tq, GROUP_WIDTH), lambda b, i: (b * nq + i, 0)),
        out_shape=jax.ShapeDtypeStruct((t, GROUP_WIDTH), F32),
        scratch_shapes=[pltpu.VMEM((GROUP_HEADS, tq, LANES), BF16),
                        pltpu.VMEM((GROUP_HEADS, tq, LANES), F32),
                        pltpu.VMEM((GROUP_HEADS, tq, LANES), F32)],
        compiler_params=_cparams(("arbitrary", "arbitrary")),
        name="mixer_stick",
    )(p, p, p, tri_excl)


def _outproj_kernel(x_ref, gt_ref, oa_ref, ob_ref, oc_ref, od_ref, gg_ref, w_ref, o_ref):
    acc = None
    for m, ref in enumerate((oa_ref, ob_ref, oc_ref, od_ref)):
        o = ref[...]
        y = o * lax.rsqrt(jnp.mean(o * o, axis=-1, keepdims=True) + EPS) * gg_ref[m:m + 1, :]
        part = _dot(y.astype(BF16), w_ref[m * GROUP_WIDTH:(m + 1) * GROUP_WIDTH, :])
        acc = part if acc is None else acc + part
    o_ref[...] = x_ref[...] + gt_ref[0] * acc


def _outproj(x, gt, outs, gg, w, l, seq):
    t, d = x.shape
    tm = 512
    per_b = seq // tm
    mix = pl.BlockSpec((tm, GROUP_WIDTH), lambda i: (i, 0))
    return pl.pallas_call(
        _outproj_kernel,
        grid=(t // tm,),
        in_specs=[
            pl.BlockSpec((tm, d), lambda i: (i, 0)),
            pl.BlockSpec((1, 1, d), lambda i: (i // per_b, 0, 0)),
            mix, mix, mix, mix,
            pl.BlockSpec((None, N_MIXERS, GROUP_WIDTH), lambda i: (l, 0, 0)),
            pl.BlockSpec((None, N_MIXERS * GROUP_WIDTH, d), lambda i: (l, 0, 0)),
        ],
        out_specs=pl.BlockSpec((tm, d), lambda i: (i, 0)),
        out_shape=jax.ShapeDtypeStruct((t, d), F32),
        compiler_params=_cparams(("arbitrary",)),
        name="outproj",
    )(x, gt, *outs, gg, w)


def _prep_in_weights(w_in, qk_g):
    depth, d, _ = w_in.shape
    sizes = (GROUP_WIDTH, HEAD_DIM, HEAD_DIM, IDX_HEADS * IDX_DIM, IDX_DIM, IDX_HEADS,
             GROUP_WIDTH, HEAD_DIM, HEAD_DIM,
             GROUP_WIDTH, GROUP_WIDTH, GROUP_WIDTH, GROUP_HEADS, GROUP_WIDTH,
             GROUP_WIDTH, GROUP_WIDTH, GROUP_WIDTH)
    pts = np.cumsum(sizes)[:-1].tolist()
    (a_q, a_k, a_v, a_iq, a_ik, a_iw, b_q, b_k, b_v,
     c_q, c_k, c_v, c_f, c_g, d_q, d_k, d_v) = jnp.split(w_in, pts, axis=-1)
    z = lambda n: jnp.zeros((depth, d, n), w_in.dtype)
    qscale = HEAD_DIM ** -0.5
    main = jnp.concatenate([
        a_q, b_q, c_q, c_k,
        a_k, a_k, b_k, b_k, z(2 * LANES),
        c_v, c_g, d_q * qscale, d_k, d_v, a_iq * (IDX_DIM ** -0.5),
        a_v, a_v, b_v, b_v, a_ik, a_ik, z(LANES),
    ], axis=-1).astype(BF16)
    side = jnp.concatenate([c_f, a_iw * (IDX_HEADS ** -0.5), z(LANES - GROUP_HEADS - IDX_HEADS)],
                           axis=-1).astype(BF16)
    rep = lambda g, n: jnp.tile(g, (1, n))
    sscale = qscale * LOG2E
    gain = jnp.concatenate([
        rep(qk_g[:, 0], 8) * sscale, rep(qk_g[:, 2], 8) * sscale, rep(qk_g[:, 4], 8) * sscale,
        rep(qk_g[:, 5], 8), rep(qk_g[:, 1], 2), rep(qk_g[:, 3], 2),
        jnp.zeros((depth, P_COLS - 4 * PT - 2 * LANES), F32),
    ], axis=-1).reshape(depth, 1, P_COLS)
    return main, side, gain


def kernel(x, c, w_ada, b_ada, norm_g, w_in, qk_g, forget_b, sinks, rel_table, group_g, w_out,
           w_ffn_gate, w_ffn_up, w_ffn_down):
    batch, seq, d = x.shape
    depth = w_ada.shape[0]
    t = batch * seq

    mod = _modulation(c, w_ada, b_ada)
    wg = w_ffn_gate.astype(BF16)
    wu = w_ffn_up.astype(BF16)
    wd = w_ffn_down.astype(BF16)
    wo = w_out.astype(BF16)
    w_main, w_side, gain = _prep_in_weights(w_in, qk_g)
    gain_side = jnp.zeros((depth, 1, LANES), F32)

    r = np.arange(PT)
    bd = jnp.asarray((r[:, None] // HEAD_DIM == r[None, :] // HEAD_DIM).astype(np.float32) / HEAD_DIM, BF16)
    bd_side = jnp.zeros((LANES, LANES), BF16)
    r = np.arange(LANES)
    tri_incl = jnp.asarray(r[:, None] <= r[None, :], BF16)
    tri_after = jnp.asarray(r[:, None] > r[None, :], BF16)
    r = np.arange(DSA_KC)
    tri_before = jnp.asarray(r[:, None] < r[None, :], BF16)

    bias_dsa = jnp.swapaxes(_bias_windows(rel_table, DSA_TQ)[:GROUP_HEADS], 0, 1)
    bias_swa = _bias_windows(rel_table, SWA_TQ)[GROUP_HEADS:, 1]
    fb_rows = jnp.zeros((depth, 1, LANES), F32).at[:, 0, :GROUP_HEADS].set(forget_b)
    gg = group_g.reshape(depth, N_MIXERS, GROUP_WIDTH)

    xt = x.reshape(t, d)
    for l in range(depth):
        parts = [m.reshape(batch, 1, d) for m in jnp.split(mod[l], 9, axis=-1)]
        sh1, sc1, g1, sh2, sc2, g2, sh3, sc3, g3 = parts
        xt = _ffn(xt, norm_g[l, 0:1], sh1, sc1, g1, wg, wu, wd, l, 0, seq)
        ng2 = norm_g[l, 1:2]
        p = _inproj(xt, ng2, sh2, sc2, w_main, gain, bd, l, seq, N_NORM_TILES, BF16)
        side = _inproj(xt, ng2, sh2, sc2, w_side, gain_side, bd_side, l, seq, 0, F32)
        o_a = _mixer_dsa(p, side, bias_dsa, tri_before, batch, seq)
        o_b = _mixer_swa(p, sinks[l], bias_swa, batch, seq)
        cum = _fox_cum(side, fb_rows[l], tri_incl, batch, seq)
        o_c = _mixer_fox(p, cum, batch, seq)
        o_d = _mixer_stick(p, tri_after, batch, seq)
        xt = _outproj(xt, g2, (o_a, o_b, o_c, o_d), gg, wo, l, seq)
        xt = _ffn(xt, norm_g[l, 2:3], sh3, sc3, g3, wg, wu, wd, l, 1, seq)
    return xt.reshape(batch, seq, d)
```

```python
import functools
import math

import numpy as np
import jax
import jax.numpy as jnp
from jax import lax
from jax.experimental import pallas as pl
from jax.experimental.pallas import tpu as pltpu

F32 = jnp.float32
BF16 = jnp.bfloat16
I32 = jnp.int32

HEAD_DIM = 64
N_MIXERS = 4
GROUP_HEADS = 8
GROUP_WIDTH = GROUP_HEADS * HEAD_DIM
IDX_HEADS = 16
IDX_DIM = 64
TOPK_MAX = 256
WINDOW = 128
REL_BUCKETS = 32
REL_MAX_DIST = 128
EPS = 1e-6

LANES = 128
PAIRS = GROUP_HEADS // 2
NEG = -1e30
LOG2E = math.log2(math.e)
INT_MIN = -2 ** 31
STICK_EXIT = -110.0
VMEM_LIMIT = 56 * 1024 * 1024

PT = 512
T_AQ, T_BQ, T_CQ, T_CK, T_K2 = 0, 1, 2, 3, 4
N_NORM_TILES = 5
T_CV, T_CG, T_DQ, T_DK, T_DV, T_IQ, T_MISC = 5, 6, 7, 8, 9, 10, 12
N_TILES = 13
P_COLS = N_TILES * PT


def _cparams(sem, vmem=VMEM_LIMIT):
    return pltpu.CompilerParams(dimension_semantics=sem, vmem_limit_bytes=vmem)


def _nt_dot(a, b):
    return lax.dot_general(a, b, (((1,), (1,)), ((), ())), preferred_element_type=F32)


def _dot(a, b):
    return jnp.dot(a, b, preferred_element_type=F32)


def _sigmoid(x):
    return 1.0 / (1.0 + jnp.exp(-x))


def _lane_tile(x, n):
    return x if n == 1 else jnp.concatenate([x] * n, axis=1)


def _own_lane_mask(shape, h):
    lane = lax.broadcasted_iota(I32, shape, 1)
    return (lane < HEAD_DIM) if h % 2 == 0 else (lane >= HEAD_DIM)


def _mod_kernel(c_ref, w_ref, b_ref, o_ref):
    c = c_ref[...]
    cond = (c * _sigmoid(c)).astype(BF16)
    o_ref[0] = _dot(cond, w_ref[0].astype(BF16)) + b_ref[0]


def _modulation(c, w_ada, b_ada):
    depth, d, n = w_ada.shape
    b = c.shape[0]
    rows = 8
    tn = 1024
    c_pad = jnp.zeros((rows, d), F32).at[:b].set(c)
    out = pl.pallas_call(
        _mod_kernel,
        grid=(depth, n // tn),
        in_specs=[
            pl.BlockSpec((rows, d), lambda l, j: (0, 0)),
            pl.BlockSpec((1, d, tn), lambda l, j: (l, 0, j)),
            pl.BlockSpec((1, 1, tn), lambda l, j: (l, 0, j)),
        ],
        out_specs=pl.BlockSpec((1, rows, tn), lambda l, j: (l, 0, j)),
        out_shape=jax.ShapeDtypeStruct((depth, rows, n), F32),
        compiler_params=_cparams(("arbitrary", "arbitrary")),
        name="adaln_mod",
    )(c_pad, w_ada, b_ada.reshape(depth, 1, n))
    return out[:, :b]


def _norm_modulate(x, ng, sh, sc):
    ms = jnp.mean(x * x, axis=-1, keepdims=True)
    y = x * lax.rsqrt(ms + EPS) * ng
    return y * (1.0 + sc) + sh


def _ffn_kernel(x_ref, ng_ref, sh_ref, sc_ref, gt_ref, wg_ref, wu_ref, wd_ref, o_ref, h_ref, a_ref):
    f = pl.program_id(1)
    last = pl.num_programs(1) - 1

    def activation():
        h = h_ref[...]
        g = _dot(h, wg_ref[...])
        u = _dot(h, wu_ref[...])
        return (g * _sigmoid(g) * u).astype(BF16)

    @pl.when(f == 0)
    def _():
        h = _norm_modulate(x_ref[...], ng_ref[...], sh_ref[0], sc_ref[0])
        h_ref[...] = h.astype(BF16)
        o_ref[...] = jnp.zeros_like(o_ref)
        a_ref[...] = activation()

    @pl.when(jnp.logical_and(f > 0, f < last))
    def _():
        o_ref[...] += _dot(a_ref[...], wd_ref[...])
        a_ref[...] = activation()

    @pl.when(f == last)
    def _():
        acc = o_ref[...] + _dot(a_ref[...], wd_ref[...])
        o_ref[...] = x_ref[...] + 0.5 * gt_ref[0] * acc


def _ffn(x, ng, sh, sc, gt, wg, wu, wd, l, s, seq):
    t, d = x.shape
    tm, tf = min(1024, seq), 512
    nf = wg.shape[-1] // tf
    per_b = seq // tm
    vec = pl.BlockSpec((1, 1, d), lambda i, f: (i // per_b, 0, 0))
    up = lambda i, f: (l, s, 0, jnp.minimum(f, nf - 1))
    return pl.pallas_call(
        _ffn_kernel,
        grid=(t // tm, nf + 1),
        in_specs=[
            pl.BlockSpec((tm, d), lambda i, f: (i, 0), pipeline_mode=pl.Buffered(1)),
            pl.BlockSpec((1, d), lambda i, f: (0, 0)),
            vec, vec, vec,
            pl.BlockSpec((None, None, d, tf), up),
            pl.BlockSpec((None, None, d, tf), up),
            pl.BlockSpec((None, None, tf, d), lambda i, f: (l, s, jnp.maximum(f - 1, 0), 0)),
        ],
        out_specs=pl.BlockSpec((tm, d), lambda i, f: (i, 0)),
        out_shape=jax.ShapeDtypeStruct((t, d), F32),
        scratch_shapes=[pltpu.VMEM((tm, d), BF16), pltpu.VMEM((tm, tf), BF16)],
        compiler_params=_cparams(("arbitrary", "arbitrary")),
        name="ffn",
    )(x, ng, sh, sc, gt, wg, wu, wd)


def _inproj_kernel(x_ref, ng_ref, sh_ref, sc_ref, w_ref, gain_ref, bd_ref, ws_ref, o_ref, side_ref, h_ref,
                   *, n_norm):
    j = pl.program_id(1)

    @pl.when(j == 0)
    def _():
        h = _norm_modulate(x_ref[...], ng_ref[...], sh_ref[0], sc_ref[0]).astype(BF16)
        h_ref[...] = h
        side_ref[...] = _dot(h, ws_ref[...])

    y = _dot(h_ref[...], w_ref[...])

    @pl.when(j < n_norm)
    def _():
        ms = _dot((y * y).astype(BF16), bd_ref[...])
        o_ref[...] = (y * lax.rsqrt(ms + EPS) * gain_ref[...]).astype(o_ref.dtype)

    @pl.when(j >= n_norm)
    def _():
        o_ref[...] = y.astype(o_ref.dtype)


def _inproj(x, ng, sh, sc, w, gain, bd, w_side, l, seq, n_norm):
    t, d = x.shape
    n = w.shape[-1]
    tm = min(1024, seq)
    tn = bd.shape[0]
    per_b = seq // tm
    vec = pl.BlockSpec((1, 1, d), lambda i, j: (i // per_b, 0, 0))
    return pl.pallas_call(
        functools.partial(_inproj_kernel, n_norm=n_norm),
        grid=(t // tm, n // tn),
        in_specs=[
            pl.BlockSpec((tm, d), lambda i, j: (i, 0)),
            pl.BlockSpec((1, d), lambda i, j: (0, 0)),
            vec, vec,
            pl.BlockSpec((None, d, tn), lambda i, j: (l, 0, j)),
            pl.BlockSpec((None, 1, tn), lambda i, j: (l, 0, j)),
            pl.BlockSpec((tn, tn), lambda i, j: (0, 0)),
            pl.BlockSpec((None, d, LANES), lambda i, j: (l, 0, 0)),
        ],
        out_specs=[pl.BlockSpec((tm, tn), lambda i, j: (i, j)),
                   pl.BlockSpec((tm, LANES), lambda i, j: (i, 0))],
        out_shape=[jax.ShapeDtypeStruct((t, n), BF16), jax.ShapeDtypeStruct((t, LANES), F32)],
        scratch_shapes=[pltpu.VMEM((tm, d), BF16)],
        compiler_params=_cparams(("arbitrary", "arbitrary")),
        name="inproj",
    )(x, ng, sh, sc, w, gain, bd, w_side)


def _rel_bucket_np(dist):
    n = np.maximum(dist, 0)
    max_exact = REL_BUCKETS // 2
    nf = np.maximum(n, 1).astype(np.float32)
    large = max_exact + (np.log(nf / np.float32(max_exact)) / np.float32(math.log(REL_MAX_DIST / max_exact))
                         * np.float32(REL_BUCKETS - max_exact)).astype(np.int32)
    large = np.minimum(large, REL_BUCKETS - 1)
    return np.where(n < max_exact, n, large).astype(np.int32)


def _bias_kernel(tab_ref, bkt_ref, o_ref, *, sub_far):
    h = pl.program_id(0)
    far = tab_ref[REL_BUCKETS - 1, h]
    for which in range(2):
        bkt = bkt_ref[which]
        acc = jnp.zeros(bkt.shape, F32)
        for b in range(REL_BUCKETS):
            acc = jnp.where(bkt == b, tab_ref[b, h], acc)
        o_ref[0, which] = (acc - jnp.where(h < sub_far, far, 0.0)) * LOG2E


def _bias_windows(rel_table, tq):
    width = tq + LANES
    t_l = np.arange(tq)[:, None]
    w = np.arange(width)[None, :]
    bkt = np.stack([_rel_bucket_np(t_l - w), _rel_bucket_np(t_l + LANES - w)]).astype(np.int32)
    nh = rel_table.shape[1]
    return pl.pallas_call(
        functools.partial(_bias_kernel, sub_far=GROUP_HEADS),
        grid=(nh,),
        in_specs=[
            pl.BlockSpec(memory_space=pltpu.SMEM),
            pl.BlockSpec((2, tq, width), lambda h: (0, 0, 0)),
        ],
        out_specs=pl.BlockSpec((1, 2, tq, width), lambda h: (h, 0, 0, 0)),
        out_shape=jax.ShapeDtypeStruct((nh, 2, tq, width), F32),
        compiler_params=_cparams(("arbitrary",)),
        name="rel_bias_windows",
    )(rel_table, jnp.asarray(bkt))


def _softmax_step_stacked(s, v, m_ref, l_ref, acc_ref):
    nh, rows, width = s.shape
    reps = width // LANES
    m_prev = m_ref[...]
    m_new = jnp.maximum(m_prev, jnp.max(s, axis=2, keepdims=True))
    p = jnp.exp2(s - jnp.concatenate([m_new] * reps, axis=2))
    alpha = jnp.exp2(m_prev - m_new)
    l_ref[...] = alpha * l_ref[...] + jnp.sum(p, axis=2, keepdims=True)
    m_ref[...] = m_new
    pv = _dot(p.astype(BF16).reshape(nh * rows, width), v).reshape(nh, rows, LANES)
    acc_ref[...] = alpha * acc_ref[...] + pv


def _store_pairs(o_ref, per_head):
    for j in range(PAIRS):
        lane = lax.broadcasted_iota(I32, per_head[0].shape, 1)
        o_ref[:, j * LANES:(j + 1) * LANES] = jnp.where(
            lane < HEAD_DIM, per_head[2 * j], per_head[2 * j + 1]).astype(o_ref.dtype)


DSA_TQ = 256
DSA_NB = DSA_TQ // LANES
DSA_KC1 = 512
DSA_KC = 512
DSA_SUB = DSA_KC // LANES
DSA_VALUE_STEPS = 26
DSA_VALUE_BLIND_STEPS = 10


def _dsa_kernel(q_ref, iq_ref, iw_ref, k2_ref, v2_ref, ik2_ref, bias_ref, tri_ref, o_ref,
                key_ref, wb_ref, iqm_ref, qm_ref, t_ref, cand_ref, m_ref, l_ref, acc_ref, *, k_top):
    i = pl.program_id(1)
    tq = DSA_TQ
    row0 = i * tq
    n_valid_cols = row0 + tq

    iw = iw_ref[...]
    for h in range(IDX_HEADS):
        col = GROUP_HEADS + h
        wb_ref[h] = jnp.broadcast_to(iw[:, col:col + 1], (tq, LANES))
        blk = iq_ref[:, (h // 2) * LANES:(h // 2 + 1) * LANES]
        iqm_ref[h] = jnp.where(_own_lane_mask(blk.shape, h), blk, jnp.zeros_like(blk))
    for h in range(GROUP_HEADS):
        blk = q_ref[:, (h // 2) * LANES:(h // 2 + 1) * LANES]
        qm_ref[h] = jnp.where(_own_lane_mask(blk.shape, h), blk, jnp.zeros_like(blk))

    qpos = row0 + lax.broadcasted_iota(I32, (tq, DSA_KC1), 0)
    iq_all = iqm_ref[...].reshape(IDX_HEADS * tq, LANES)
    wb_all = jnp.concatenate([wb_ref[...]] * (DSA_KC1 // LANES), axis=2)
    n1 =(n_valid_cols + DSA_KC1 - 1) // DSA_KC1

    def score_chunk(c, carry):
        start = pl.multiple_of(c * DSA_KC1, DSA_KC1)
        ik = ik2_ref[pl.ds(start, DSA_KC1), :]
        r = _nt_dot(iq_all, ik).reshape(IDX_HEADS, tq, DSA_KC1)
        score = jnp.sum(jnp.maximum(r, 0.0) * wb_all, axis=0)
        bits = lax.bitcast_convert_type(score, I32)
        key = bits ^ ((bits >> 31) & 0x7FFFFFFF)
        key = jnp.where(score == 0.0, 0, key)
        kpos = start + lax.broadcasted_iota(I32, (tq, DSA_KC1), 1)
        valid = kpos <= qpos
        key = jnp.where(valid, key, INT_MIN)
        smax, smin = carry
        hi = jnp.where(valid, score, -jnp.inf)
        lo = jnp.where(valid, score, jnp.inf)
        for u in range(DSA_KC1 // LANES):
            key_ref[c * (DSA_KC1 // LANES) + u] = key[:, u * LANES:(u + 1) * LANES]
            smax = jnp.maximum(smax, hi[:, u * LANES:(u + 1) * LANES])
            smin = jnp.minimum(smin, lo[:, u * LANES:(u + 1) * LANES])
        return smax, smin

    smax, smin = lax.fori_loop(0, n1, score_chunk, (jnp.full((tq, LANES), -jnp.inf, F32),
                                                    jnp.full((tq, LANES), jnp.inf, F32)))
    nsel = (n_valid_cols + DSA_KC - 1) // DSA_KC
    n1_blocks = n1 * (DSA_KC1 // LANES)

    @pl.when(n1_blocks < nsel * DSA_SUB)
    def _():
        for u in range(DSA_KC1 // LANES):
            key_ref[n1_blocks + u] = jnp.full((tq, LANES), INT_MIN, I32)

    def load_keys(c):
        blk = key_ref[pl.ds(c * DSA_SUB, DSA_SUB)]
        return jnp.concatenate([blk[u] for u in range(DSA_SUB)], axis=1)

    def count_ge(thr, strict):
        cand_ref[...] = thr

        def body(c, acc):
            out = []
            for r in range(DSA_NB):
                rows = slice(r * LANES, (r + 1) * LANES)
                cand = cand_ref[rows, :]
                a = acc[rows]
                for u in range(DSA_SUB):
                    keys = key_ref[c * DSA_SUB + u, rows, :]
                    hit = (keys > cand) if strict else (keys >= cand)
                    a = jnp.where(hit, a + 1.0, a)
                out.append(a)
            return jnp.concatenate(out, axis=0)
        acc = lax.fori_loop(0, nsel, body, jnp.zeros((tq, LANES), F32))
        return jnp.broadcast_to(jnp.sum(acc, axis=1, keepdims=True), (tq, LANES))

    t_ref[...] = jnp.full((tq, LANES), INT_MIN, I32)
    kf = float(k_top)

    def float_key(x):
        bits = lax.bitcast_convert_type(x, I32)
        return jnp.where(x == 0.0, 0, bits ^ ((bits >> 31) & 0x7FFFFFFF))

    def bisect_values():
        lower = jnp.broadcast_to(jnp.min(smin, axis=1, keepdims=True), (tq, LANES))
        upper = jnp.broadcast_to(jnp.max(smax, axis=1, keepdims=True), (tq, LANES))
        n_causal = (row0 + 1 + lax.broadcasted_iota(I32, (tq, LANES), 0)).astype(F32)
        keep_all = n_causal <= kf
        open0 = jnp.where(keep_all, 0.0, 1.0)

        def step(lower, upper, still_open):
            mid = 0.5 * lower + 0.5 * upper
            cnt = count_ge(float_key(mid), False)
            active = still_open > 0.0
            go_up = jnp.logical_and(active, cnt >= kf)
            go_down = jnp.logical_and(active, cnt < kf)
            lower = jnp.where(go_up, mid, lower)
            upper = jnp.where(go_down, mid, upper)
            still_open = jnp.where(jnp.logical_and(go_up, cnt == kf), 0.0, still_open)
            return lower, upper, still_open

        lower, upper, still_open = lax.fori_loop(
            0, DSA_VALUE_BLIND_STEPS, lambda _, c: step(*c), (lower, upper, open0))

        def cond(carry):
            it, _, _, still_open = carry
            return jnp.logical_and(it < DSA_VALUE_STEPS, jnp.max(still_open) > 0.0)

        def body(carry):
            it, lower, upper, still_open = carry
            return (it + 1,) + step(lower, upper, still_open)

        _, lower, _, still_open = lax.while_loop(
            cond, body, (jnp.int32(DSA_VALUE_BLIND_STEPS), lower, upper, still_open))
        return jnp.where(keep_all, INT_MIN, float_key(lower)), jnp.max(still_open) == 0.0

    def exact_select():
        def bit_step(it, carry):
            thr, cnt_thr = carry
            cand = thr + jnp.left_shift(jnp.int32(1), 31 - it)
            cnt = count_ge(cand, False)
            take = cnt >= kf
            return jnp.where(take, cand, thr), jnp.where(take, cnt, cnt_thr)

        thr0 = jnp.full((tq, LANES), INT_MIN, I32)
        cnt0 = jnp.full((tq, LANES), float(2 ** 24), F32)
        thr, cnt_thr = lax.fori_loop(0, 32, bit_step, (thr0, cnt0))
        t_ref[...] = thr

        excess = jnp.where(thr > INT_MIN, cnt_thr - kf, 0.0)

        @pl.when(jnp.max(excess) > 0.0)
        def _():
            need = kf - count_ge(thr, True)

            def tie_chunk(c, seen):
                keys = load_keys(c)
                eq = keys == _lane_tile(thr, DSA_SUB)
                eqf = jnp.where(eq, 1.0, 0.0)
                rank = _dot(eqf.astype(BF16), tri_ref[...]) + _lane_tile(seen, DSA_SUB)
                dropped = jnp.where(rank >= _lane_tile(need, DSA_SUB), INT_MIN, keys)
                keys = jnp.where(eq, dropped, keys)
                for u in range(DSA_SUB):
                    key_ref[c * DSA_SUB + u] = keys[:, u * LANES:(u + 1) * LANES]
                return seen + jnp.broadcast_to(jnp.sum(eqf, axis=1, keepdims=True), (tq, LANES))

            lax.fori_loop(0, nsel, tie_chunk, jnp.zeros((tq, LANES), F32))

    @pl.when(n_valid_cols > k_top)
    def _():
        thr_fast, finished = bisect_values()
        t_ref[...] = thr_fast

        @pl.when(jnp.logical_not(finished))
        def _():
            exact_select()

    thr = jnp.maximum(t_ref[...], INT_MIN + 1)

    m_ref[...] = jnp.full(m_ref.shape, NEG, F32)
    l_ref[...] = jnp.zeros(l_ref.shape, F32)
    acc_ref[...] = jnp.zeros(acc_ref.shape, F32)
    q_all = qm_ref[...].reshape(GROUP_HEADS * tq, LANES)

    def attend(k_blk, v_blk, madd, bias):
        width = k_blk.shape[0]
        s = _nt_dot(q_all, k_blk).reshape(GROUP_HEADS, tq, width) + madd[None]
        if bias is not None:
            s = s + bias
        _softmax_step_stacked(s, v_blk, m_ref, l_ref, acc_ref)

    nwin = DSA_NB + 1
    win_blk = jnp.maximum(i * DSA_NB - 1, 0)
    win_start = pl.multiple_of(win_blk * LANES, LANES)
    near_keys = key_ref[pl.ds(win_blk, nwin)]
    near_keys = jnp.concatenate([near_keys[u] for u in range(nwin)], axis=1)
    near_madd = jnp.where(near_keys >= _lane_tile(thr, nwin), 0.0, NEG)
    attend(k2_ref[pl.ds(win_start, nwin * LANES), :], v2_ref[pl.ds(win_start, nwin * LANES), :],
           near_madd, bias_ref[jnp.minimum(i, 1)])
    for u in range(nwin):
        key_ref[win_blk + u] = jnp.full((tq, LANES), INT_MIN, I32)

    n_far = (win_blk * LANES + DSA_KC - 1) // DSA_KC

    def far_chunk(c, carry):
        start = pl.multiple_of(c * DSA_KC, DSA_KC)
        madd = jnp.where(load_keys(c) >= _lane_tile(thr, DSA_SUB), 0.0, NEG)
        attend(k2_ref[pl.ds(start, DSA_KC), :], v2_ref[pl.ds(start, DSA_KC), :], madd, None)
        return carry

    lax.fori_loop(0, n_far, far_chunk, 0)
    _store_pairs(o_ref, [acc_ref[h] / l_ref[h] for h in range(GROUP_HEADS)])


def _mixer_dsa(p, side, bias, tri, batch, seq):
    t = p.shape[0]
    tq = DSA_TQ
    nq = seq // tq
    k_top = min(TOPK_MAX, seq // 4)
    cw = PT // LANES
    return pl.pallas_call(
        functools.partial(_dsa_kernel, k_top=k_top),
        grid=(batch, nq),
        in_specs=[
            pl.BlockSpec((tq, PT), lambda b, i: (b * nq + i, T_AQ)),
            pl.BlockSpec((tq, 2 * PT), lambda b, i: (b * nq + i, T_IQ // 2)),
            pl.BlockSpec((tq, LANES), lambda b, i: (b * nq + i, 0)),
            pl.BlockSpec((seq, LANES), lambda b, i: (b, T_K2 * cw), pipeline_mode=pl.Buffered(1)),
            pl.BlockSpec((seq, LANES), lambda b, i: (b, T_MISC * cw), pipeline_mode=pl.Buffered(1)),
            pl.BlockSpec((seq, LANES), lambda b, i: (b, T_MISC * cw + 2), pipeline_mode=pl.Buffered(1)),
            pl.BlockSpec((2, GROUP_HEADS, tq, tq + LANES), lambda b, i: (0, 0, 0, 0),
                         pipeline_mode=pl.Buffered(1)),
            pl.BlockSpec((DSA_KC, DSA_KC), lambda b, i: (0, 0), pipeline_mode=pl.Buffered(1)),
        ],
        out_specs=pl.BlockSpec((tq, GROUP_WIDTH), lambda b, i: (b * nq + i, 0)),
        out_shape=jax.ShapeDtypeStruct((t, GROUP_WIDTH), F32),
        scratch_shapes=[
            pltpu.VMEM((max(seq // LANES, DSA_SUB, DSA_NB + 1), tq, LANES), I32),
            pltpu.VMEM((IDX_HEADS, tq, LANES), F32),
            pltpu.VMEM((IDX_HEADS, tq, LANES), BF16),
            pltpu.VMEM((GROUP_HEADS, tq, LANES), BF16),
            pltpu.VMEM((tq, LANES), I32),
            pltpu.VMEM((tq, LANES), I32),
            pltpu.VMEM((GROUP_HEADS, tq, LANES), F32),
            pltpu.VMEM((GROUP_HEADS, tq, LANES), F32),
            pltpu.VMEM((GROUP_HEADS, tq, LANES), F32),
        ],
        compiler_params=_cparams(("arbitrary", "arbitrary")),
        name="mixer_dsa",
    )(p, p, side, p, p, p, bias, tri)


SWA_TQ = 128


def _swa_kernel(sink_ref, q_ref, kp_ref, kc_ref, vp_ref, vc_ref, bias_ref, o_ref):
    i = pl.program_id(1)
    tq = SWA_TQ
    nh = GROUP_HEADS
    t_l = lax.broadcasted_iota(I32, (tq, 2 * LANES), 0)
    w = lax.broadcasted_iota(I32, (tq, 2 * LANES), 1)
    first = jnp.where(i > 0, t_l + 1, LANES)
    madd = jnp.where(w >= first, jnp.where(w <= t_l + LANES, 0.0, NEG), NEG)
    q_heads = []
    sink_rows = []
    for h in range(nh):
        blk = q_ref[:, (h // 2) * LANES:(h // 2 + 1) * LANES]
        q_heads.append(jnp.where(_own_lane_mask(blk.shape, h), blk, jnp.zeros_like(blk)))
        sink_rows.append(jnp.full((1, tq, LANES), sink_ref[h] * LOG2E, F32))
    sinks = jnp.concatenate(sink_rows, axis=0)
    k_win = jnp.concatenate([kp_ref[...], kc_ref[...]], axis=0)
    v_win = jnp.concatenate([vp_ref[...], vc_ref[...]], axis=0)
    s = _nt_dot(jnp.concatenate(q_heads, axis=0), k_win).reshape(nh, tq, 2 * LANES) + bias_ref[...] + madd[None]
    m = jnp.maximum(jnp.max(s, axis=2, keepdims=True), sinks)
    p = jnp.exp2(s - jnp.concatenate([m, m], axis=2))
    l = jnp.sum(p, axis=2, keepdims=True) + jnp.exp2(sinks - m)
    out = _dot(p.astype(BF16).reshape(nh * tq, 2 * LANES), v_win).reshape(nh, tq, LANES) / l
    _store_pairs(o_ref, [out[h] for h in range(nh)])


def _mixer_swa(p, sinks, bias, batch, seq):
    t = p.shape[0]
    tq = SWA_TQ
    nq = seq // tq
    cw = PT // LANES
    cur = lambda col: (lambda b, i: (b * nq + i, col))
    prev = lambda col: (lambda b, i: (b * nq + jnp.maximum(i - 1, 0), col))
    return pl.pallas_call(
        _swa_kernel,
        grid=(batch, nq),
        in_specs=[
            pl.BlockSpec(memory_space=pltpu.SMEM),
            pl.BlockSpec((tq, PT), lambda b, i: (b * nq + i, T_BQ)),
            pl.BlockSpec((tq, LANES), prev(T_K2 * cw + 1)),
            pl.BlockSpec((tq, LANES), cur(T_K2 * cw + 1)),
            pl.BlockSpec((tq, LANES), prev(T_MISC * cw + 1)),
            pl.BlockSpec((tq, LANES), cur(T_MISC * cw + 1)),
            pl.BlockSpec((GROUP_HEADS, tq, 2 * LANES), lambda b, i: (0, 0, 0)),
        ],
        out_specs=pl.BlockSpec((tq, GROUP_WIDTH), lambda b, i: (b * nq + i, 0)),
        out_shape=jax.ShapeDtypeStruct((t, GROUP_WIDTH), F32),
        compiler_params=_cparams(("arbitrary", "arbitrary")),
        name="mixer_swa",
    )(sinks, p, p, p, p, p, bias)


def _logsig(x):
    return jnp.minimum(x, 0.0) - jnp.log(1.0 + jnp.exp(-jnp.abs(x)))


def _foxcum_kernel(fb_ref, f_ref, tri_ref, o_ref, *, nchunk):
    def body(c, carry):
        start = pl.multiple_of(c * LANES, LANES)
        lf = _logsig(f_ref[pl.ds(start, LANES), :] + fb_ref[...]).T
        p1 = lf.astype(BF16)
        r1 = lf - p1.astype(F32)
        p2 = r1.astype(BF16)
        p3 = (r1 - p2.astype(F32)).astype(BF16)
        tri = tri_ref[...]
        cum = _dot(p1, tri) + _dot(p2, tri) + _dot(p3, tri) + carry
        o_ref[0, c] = cum[:GROUP_HEADS]
        return jnp.broadcast_to(cum[:, LANES - 1:LANES], cum.shape)

    lax.fori_loop(0, nchunk, body, jnp.zeros((LANES, LANES), F32))


def _fox_cum(side, fb_row, tri_incl, batch, seq):
    nchunk = seq // LANES
    return pl.pallas_call(
        functools.partial(_foxcum_kernel, nchunk=nchunk),
        grid=(batch,),
        in_specs=[
            pl.BlockSpec((1, LANES), lambda b: (0, 0)),
            pl.BlockSpec((seq, LANES), lambda b: (b, 0)),
            pl.BlockSpec((LANES, LANES), lambda b: (0, 0)),
        ],
        out_specs=pl.BlockSpec((1, nchunk, GROUP_HEADS, LANES), lambda b: (b, 0, 0, 0)),
        out_shape=jax.ShapeDtypeStruct((batch, nchunk, GROUP_HEADS, LANES), F32),
        compiler_params=_cparams(("arbitrary",)),
        name="fox_cumsum",
    )(fb_row, side, tri_incl)


FOX_TQ = 256
FOX_KC = 512
FOX_SUB = FOX_KC // LANES


def _fox_kernel(q_ref, g_ref, k_ref, v_ref, cum_ref, o_ref, qm_ref, m_ref, l_ref, acc_ref):
    i = pl.program_id(1)
    tq = FOX_TQ
    for h in range(GROUP_HEADS):
        blk = q_ref[:, (h // 2) * LANES:(h // 2 + 1) * LANES]
        qm_ref[h] = jnp.where(_own_lane_mask(blk.shape, h), blk, jnp.zeros_like(blk))
    m_ref[...] = jnp.full(m_ref.shape, NEG, F32)
    l_ref[...] = jnp.zeros(l_ref.shape, F32)
    acc_ref[...] = jnp.zeros(acc_ref.shape, F32)
    f_ref0 = cum_ref[0, i * (tq // LANES)][:, 0:1]

    def chunk(c, masked):
        start = pl.multiple_of(c * FOX_KC, FOX_KC)
        cum = cum_ref[0, pl.ds(c * FOX_SUB, FOX_SUB)]
        cum = jnp.concatenate([cum[u] for u in range(FOX_SUB)], axis=1)
        fbias = (f_ref0 - cum) * LOG2E
        per_head = []
        for pr in range(PAIRS):
            pair = slice(pr * LANES, (pr + 1) * LANES)
            q_pair = qm_ref[2 * pr:2 * pr + 2].reshape(2 * tq, LANES)
            s_pair = _nt_dot(q_pair, k_ref[pl.ds(start, FOX_KC), pair])
            for e in range(2):
                h = 2 * pr + e
                per_head.append(s_pair[e * tq:(e + 1) * tq] + fbias[h:h + 1, :])
        s = jnp.stack(per_head)
        if masked:
            shape = (GROUP_HEADS, tq, FOX_KC)
            ok = start + lax.broadcasted_iota(I32, shape, 2) <= i * tq + lax.broadcasted_iota(I32, shape, 1)
            s = jnp.where(ok, s, NEG)
        reps = FOX_KC // LANES
        m_prev = m_ref[...]
        m_new = jnp.maximum(m_prev, jnp.max(s, axis=2, keepdims=True))
        p = jnp.exp2(s - jnp.concatenate([m_new] * reps, axis=2))
        alpha = jnp.exp2(m_prev - m_new)
        l_ref[...] = alpha * l_ref[...] + jnp.sum(p, axis=2, keepdims=True)
        m_ref[...] = m_new
        p = p.astype(BF16)
        pvs = []
        for pr in range(PAIRS):
            p_pair = p[2 * pr:2 * pr + 2].reshape(2 * tq, FOX_KC)
            pvs.append(_dot(p_pair, v_ref[pl.ds(start, FOX_KC), pr * LANES:(pr + 1) * LANES]))
        acc_ref[...] = alpha * acc_ref[...] + jnp.concatenate(pvs, axis=0).reshape(GROUP_HEADS, tq, LANES)

    def body(c, carry):
        chunk(c, False)
        return carry

    n_full = (i * tq) // FOX_KC
    lax.fori_loop(0, n_full, body, 0)
    chunk(n_full, True)
    outs = []
    for h in range(GROUP_HEADS):
        gate = _sigmoid(g_ref[:, (h // 2) * LANES:(h // 2 + 1) * LANES].astype(F32))
        outs.append(acc_ref[h] / l_ref[h] * gate)
    _store_pairs(o_ref, outs)


def _mixer_fox(p, cum, batch, seq):
    t = p.shape[0]
    tq = FOX_TQ
    nq = seq // tq
    return pl.pallas_call(
        _fox_kernel,
        grid=(batch, nq),
        in_specs=[
            pl.BlockSpec((tq, PT), lambda b, i: (b * nq + i, T_CQ)),
            pl.BlockSpec((tq, PT), lambda b, i: (b * nq + i, T_CG)),
            pl.BlockSpec((seq, PT), lambda b, i: (b, T_CK)),
            pl.BlockSpec((seq, PT), lambda b, i: (b, T_CV)),
            pl.BlockSpec((1, seq // LANES, GROUP_HEADS, LANES), lambda b, i: (b, 0, 0, 0)),
        ],
        out_specs=pl.BlockSpec((tq, GROUP_WIDTH), lambda b, i: (b * nq + i, 0)),
        out_shape=jax.ShapeDtypeStruct((t, GROUP_WIDTH), F32),
        scratch_shapes=[
            pltpu.VMEM((GROUP_HEADS, tq, LANES), BF16),
            pltpu.VMEM((GROUP_HEADS, tq, LANES), F32),
            pltpu.VMEM((GROUP_HEADS, tq, LANES), F32),
            pltpu.VMEM((GROUP_HEADS, tq, LANES), F32),
        ],
        compiler_params=_cparams(("arbitrary", "arbitrary")),
        name="mixer_fox",
    )(p, p, p, p, cum)


STK_TQ = 128


def _stick_kernel(q_ref, k_ref, v_ref, tri_ref, o_ref, qm_ref, acc_ref, r_ref):
    i = pl.program_id(1)
    tq = STK_TQ
    nh = GROUP_HEADS
    for h in range(nh):
        blk = q_ref[:, (h // 2) * LANES:(h // 2 + 1) * LANES]
        qm_ref[h] = jnp.where(_own_lane_mask(blk.shape, h), blk, jnp.zeros_like(blk))
    acc_ref[...] = jnp.zeros(acc_ref.shape, F32)
    r_ref[...] = jnp.zeros(r_ref.shape, F32)
    before = (lax.broadcasted_iota(I32, (nh, tq, LANES), 2) < lax.broadcasted_iota(I32, (nh, tq, LANES), 1))
    tri = tri_ref[...]

    def block(j, masked):
        start = pl.multiple_of(j * LANES, LANES)
        zs = []
        for pr in range(PAIRS):
            q_pair = qm_ref[2 * pr:2 * pr + 2].reshape(2 * tq, LANES)
            zs.append(_nt_dot(q_pair, k_ref[pl.ds(start, LANES), pr * LANES:(pr + 1) * LANES]))
        z = jnp.concatenate(zs, axis=0).reshape(nh, tq, LANES)
        lsz = _logsig(z)
        u = lsz - z
        if masked:
            u = jnp.where(before, u, 0.0)
        u2 = u.reshape(nh * tq, LANES)
        u_hi = u2.astype(BF16)
        u_lo = (u2 - u_hi.astype(F32)).astype(BF16)
        nearer = (_dot(u_hi, tri) + _dot(u_lo, tri)).reshape(nh, tq, LANES)
        run = r_ref[...]
        w = jnp.exp(lsz + nearer + run)
        if masked:
            w = jnp.where(before, w, 0.0)
        wb = w.astype(BF16)
        pvs = []
        for pr in range(PAIRS):
            w_pair = wb[2 * pr:2 * pr + 2].reshape(2 * tq, LANES)
            pvs.append(_dot(w_pair, v_ref[pl.ds(start, LANES), pr * LANES:(pr + 1) * LANES]))
        acc_ref[...] += jnp.concatenate(pvs, axis=0).reshape(nh, tq, LANES)
        run = run + jnp.sum(u, axis=2, keepdims=True)
        r_ref[...] = run
        return jnp.max(run)

    rmax = block(i, True)

    def cond(carry):
        j, rmax = carry
        return jnp.logical_and(j >= 0, rmax >= STICK_EXIT)

    def body(carry):
        j, _ = carry
        return j - 1, block(j, False)

    lax.while_loop(cond, body, (i - 1, rmax))
    _store_pairs(o_ref, [acc_ref[h] for h in range(nh)])


def _mixer_stick(p, tri_excl, batch, seq):
    t = p.shape[0]
    tq = STK_TQ
    nq = seq // tq
    return pl.pallas_call(
        _stick_kernel,
        grid=(batch, nq),
        in_specs=[
            pl.BlockSpec((tq, PT), lambda b, i: (b * nq + i, T_DQ)),
            pl.BlockSpec((seq, PT), lambda b, i: (b, T_DK)),
            pl.BlockSpec((seq, PT), lambda b, i: (b, T_DV)),
            pl.BlockSpec((LANES, LANES), lambda b, i: (0, 0)),
        ],
        out_specs=pl.BlockSpec((tq, GROUP_WIDTH), lambda b, i: (b * nq + i, 0)),
        out_shape=jax.ShapeDtypeStruct((t, GROUP_WIDTH), F32),
        scratch_shapes=[pltpu.VMEM((GROUP_HEADS, tq, LANES), BF16),
                        pltpu.VMEM((GROUP_HEADS, tq, LANES), F32),
                        pltpu.VMEM((GROUP_HEADS, tq, LANES), F32)],
        compiler_params=_cparams(("arbitrary", "arbitrary")),
        name="mixer_stick",
    )(p, p, p, tri_excl)


def _outproj_kernel(x_ref, gt_ref, oa_ref, ob_ref, oc_ref, od_ref, gg_ref, w_ref, o_ref):
    acc = None
    for m, ref in enumerate((oa_ref, ob_ref, oc_ref, od_ref)):
        o = ref[...]
        y = o * lax.rsqrt(jnp.mean(o * o, axis=-1, keepdims=True) + EPS) * gg_ref[m:m + 1, :]
        part = _dot(y.astype(BF16), w_ref[m * GROUP_WIDTH:(m + 1) * GROUP_WIDTH, :])
        acc = part if acc is None else acc + part
    o_ref[...] = x_ref[...] + gt_ref[0] * acc


def _outproj(x, gt, outs, gg, w, l, seq):
    t, d = x.shape
    tm = 512
    per_b = seq // tm
    mix = pl.BlockSpec((tm, GROUP_WIDTH), lambda i: (i, 0))
    return pl.pallas_call(
        _outproj_kernel,
        grid=(t // tm,),
        in_specs=[
            pl.BlockSpec((tm, d), lambda i: (i, 0)),
            pl.BlockSpec((1, 1, d), lambda i: (i // per_b, 0, 0)),
            mix, mix, mix, mix,
            pl.BlockSpec((None, N_MIXERS, GROUP_WIDTH), lambda i: (l, 0, 0)),
            pl.BlockSpec((None, N_MIXERS * GROUP_WIDTH, d), lambda i: (l, 0, 0)),
        ],
        out_specs=pl.BlockSpec((tm, d), lambda i: (i, 0)),
        out_shape=jax.ShapeDtypeStruct((t, d), F32),
        compiler_params=_cparams(("arbitrary",)),
        name="outproj",
    )(x, gt, *outs, gg, w)


def _prep_in_weights(w_in, qk_g):
    depth, d, _ = w_in.shape
    sizes = (GROUP_WIDTH, HEAD_DIM, HEAD_DIM, IDX_HEADS * IDX_DIM, IDX_DIM, IDX_HEADS,
             GROUP_WIDTH, HEAD_DIM, HEAD_DIM,
             GROUP_WIDTH, GROUP_WIDTH, GROUP_WIDTH, GROUP_HEADS, GROUP_WIDTH,
             GROUP_WIDTH, GROUP_WIDTH, GROUP_WIDTH)
    pts = np.cumsum(sizes)[:-1].tolist()
    (a_q, a_k, a_v, a_iq, a_ik, a_iw, b_q, b_k, b_v,
     c_q, c_k, c_v, c_f, c_g, d_q, d_k, d_v) = jnp.split(w_in, pts, axis=-1)
    z = lambda n: jnp.zeros((depth, d, n), w_in.dtype)
    qscale = HEAD_DIM ** -0.5
    main = jnp.concatenate([
        a_q, b_q, c_q, c_k,
        a_k, a_k, b_k, b_k, z(2 * LANES),
        c_v, c_g, d_q * qscale, d_k, d_v, a_iq * (IDX_DIM ** -0.5),
        a_v, a_v, b_v, b_v, a_ik, a_ik, z(LANES),
    ], axis=-1).astype(BF16)
    side = jnp.concatenate([c_f, a_iw * (IDX_HEADS ** -0.5), z(LANES - GROUP_HEADS - IDX_HEADS)],
                           axis=-1).astype(BF16)
    rep = lambda g, n: jnp.tile(g, (1, n))
    sscale = qscale * LOG2E
    gain = jnp.concatenate([
        rep(qk_g[:, 0], 8) * sscale, rep(qk_g[:, 2], 8) * sscale, rep(qk_g[:, 4], 8) * sscale,
        rep(qk_g[:, 5], 8), rep(qk_g[:, 1], 2), rep(qk_g[:, 3], 2),
        jnp.zeros((depth, P_COLS - 4 * PT - 2 * LANES), F32),
    ], axis=-1).reshape(depth, 1, P_COLS)
    return main, side, gain


def kernel(x, c, w_ada, b_ada, norm_g, w_in, qk_g, forget_b, sinks, rel_table, group_g, w_out,
           w_ffn_gate, w_ffn_up, w_ffn_down):
    batch, seq, d = x.shape
    depth = w_ada.shape[0]
    t = batch * seq

    mod = _modulation(c, w_ada, b_ada)
    wg = w_ffn_gate.astype(BF16)
    wu = w_ffn_up.astype(BF16)
    wd = w_ffn_down.astype(BF16)
    wo = w_out.astype(BF16)
    w_main, w_side, gain = _prep_in_weights(w_in.astype(BF16), qk_g)

    r = np.arange(PT)
    bd = jnp.asarray((r[:, None] // HEAD_DIM == r[None, :] // HEAD_DIM).astype(np.float32) / HEAD_DIM, BF16)
    r = np.arange(LANES)
    tri_incl = jnp.asarray(r[:, None] <= r[None, :], BF16)
    tri_after = jnp.asarray(r[:, None] > r[None, :], BF16)
    r = np.arange(DSA_KC)
    tri_before = jnp.asarray(r[:, None] < r[None, :], BF16)

    bias_dsa = jnp.swapaxes(_bias_windows(rel_table, DSA_TQ)[:GROUP_HEADS], 0, 1)
    bias_swa = _bias_windows(rel_table, SWA_TQ)[GROUP_HEADS:, 1]
    fb_rows = jnp.zeros((depth, 1, LANES), F32).at[:, 0, :GROUP_HEADS].set(forget_b)
    gg = group_g.reshape(depth, N_MIXERS, GROUP_WIDTH)

    xt = x.reshape(t, d)
    for l in range(depth):
        parts = [m.reshape(batch, 1, d) for m in jnp.split(mod[l], 9, axis=-1)]
        sh1, sc1, g1, sh2, sc2, g2, sh3, sc3, g3 = parts
        xt = _ffn(xt, norm_g[l, 0:1], sh1, sc1, g1, wg, wu, wd, l, 0, seq)
        ng2 = norm_g[l, 1:2]
        p, side = _inproj(xt, ng2, sh2, sc2, w_main, gain, bd, w_side, l, seq, N_NORM_TILES)
        o_a = _mixer_dsa(p, side, bias_dsa, tri_before, batch, seq)
        o_b = _mixer_swa(p, sinks[l], bias_swa, batch, seq)
        cum = _fox_cum(side, fb_rows[l], tri_incl, batch, seq)
        o_c = _mixer_fox(p, cum, batch, seq)
        o_d = _mixer_stick(p, tri_after, batch, seq)
        xt = _outproj(xt, g2, (o_a, o_b, o_c, o_d), gg, wo, l, seq)
        xt = _ffn(xt, norm_g[l, 2:3], sh3, sc3, g3, wg, wu, wd, l, 1, seq)
    return xt.reshape(batch, seq, d)
```

```python
import functools
import math

import numpy as np
import jax
import jax.numpy as jnp
from jax import lax
from jax.experimental import pallas as pl
from jax.experimental.pallas import tpu as pltpu

F32 = jnp.float32
BF16 = jnp.bfloat16
I32 = jnp.int32

HEAD_DIM = 64
N_MIXERS = 4
GROUP_HEADS = 8
GROUP_WIDTH = GROUP_HEADS * HEAD_DIM
IDX_HEADS = 16
IDX_DIM = 64
TOPK_MAX = 256
WINDOW = 128
REL_BUCKETS = 32
REL_MAX_DIST = 128
EPS = 1e-6

LANES = 128
PAIRS = GROUP_HEADS // 2
NEG = -1e30
LOG2E = math.log2(math.e)
INT_MIN = -2 ** 31
STICK_EXIT = -110.0
VMEM_LIMIT = 56 * 1024 * 1024
MIX_DTYPE = BF16

PT = 512
T_AQ, T_BQ, T_CQ, T_CK, T_K2 = 0, 1, 2, 3, 4
N_NORM_TILES = 5
T_CV, T_CG, T_DQ, T_DK, T_DV, T_IQ, T_MISC = 5, 6, 7, 8, 9, 10, 12
N_TILES = 13
P_COLS = N_TILES * PT


def _cparams(sem, vmem=VMEM_LIMIT):
    return pltpu.CompilerParams(dimension_semantics=sem, vmem_limit_bytes=vmem)


def _nt_dot(a, b):
    return lax.dot_general(a, b, (((1,), (1,)), ((), ())), preferred_element_type=F32)


def _dot(a, b):
    return jnp.dot(a, b, preferred_element_type=F32)


def _sigmoid(x):
    return 1.0 / (1.0 + jnp.exp(-x))


def _lane_tile(x, n):
    return x if n == 1 else jnp.concatenate([x] * n, axis=1)


def _own_lane_mask(shape, h):
    lane = lax.broadcasted_iota(I32, shape, 1)
    return (lane < HEAD_DIM) if h % 2 == 0 else (lane >= HEAD_DIM)


def _mod_kernel(c_ref, w_ref, b_ref, o_ref):
    c = c_ref[...]
    cond = (c * _sigmoid(c)).astype(BF16)
    o_ref[0] = _dot(cond, w_ref[0].astype(BF16)) + b_ref[0]


def _modulation(c, w_ada, b_ada):
    depth, d, n = w_ada.shape
    b = c.shape[0]
    rows = 8
    tn = 1024
    c_pad = jnp.zeros((rows, d), F32).at[:b].set(c)
    out = pl.pallas_call(
        _mod_kernel,
        grid=(depth, n // tn),
        in_specs=[
            pl.BlockSpec((rows, d), lambda l, j: (0, 0)),
            pl.BlockSpec((1, d, tn), lambda l, j: (l, 0, j)),
            pl.BlockSpec((1, 1, tn), lambda l, j: (l, 0, j)),
        ],
        out_specs=pl.BlockSpec((1, rows, tn), lambda l, j: (l, 0, j)),
        out_shape=jax.ShapeDtypeStruct((depth, rows, n), F32),
        compiler_params=_cparams(("arbitrary", "arbitrary")),
        name="adaln_mod",
    )(c_pad, w_ada, b_ada.reshape(depth, 1, n))
    return out[:, :b]


def _norm_modulate(x, ng, sh, sc):
    ms = jnp.mean(x * x, axis=-1, keepdims=True)
    y = x * lax.rsqrt(ms + EPS) * ng
    return y * (1.0 + sc) + sh


def _ffn_kernel(x_ref, ng_ref, sh_ref, sc_ref, gt_ref, wg_ref, wu_ref, wd_ref, o_ref, h_ref, a_ref):
    f = pl.program_id(1)
    last = pl.num_programs(1) - 1

    def activation():
        h = h_ref[...]
        g = _dot(h, wg_ref[...])
        u = _dot(h, wu_ref[...])
        return (g * _sigmoid(g) * u).astype(BF16)

    @pl.when(f == 0)
    def _():
        h = _norm_modulate(x_ref[...], ng_ref[...], sh_ref[0], sc_ref[0])
        h_ref[...] = h.astype(BF16)
        o_ref[...] = jnp.zeros_like(o_ref)
        a_ref[...] = activation()

    @pl.when(jnp.logical_and(f > 0, f < last))
    def _():
        o_ref[...] += _dot(a_ref[...], wd_ref[...])
        a_ref[...] = activation()

    @pl.when(f == last)
    def _():
        acc = o_ref[...] + _dot(a_ref[...], wd_ref[...])
        o_ref[...] = x_ref[...] + 0.5 * gt_ref[0] * acc


def _ffn(x, ng, sh, sc, gt, wg, wu, wd, l, s, seq):
    t, d = x.shape
    tm, tf = min(1024, seq), 512
    nf = wg.shape[-1] // tf
    per_b = seq // tm
    vec = pl.BlockSpec((1, 1, d), lambda i, f: (i // per_b, 0, 0))
    up = lambda i, f: (l, s, 0, jnp.minimum(f, nf - 1))
    return pl.pallas_call(
        _ffn_kernel,
        grid=(t // tm, nf + 1),
        in_specs=[
            pl.BlockSpec((tm, d), lambda i, f: (i, 0), pipeline_mode=pl.Buffered(1)),
            pl.BlockSpec((1, d), lambda i, f: (0, 0)),
            vec, vec, vec,
            pl.BlockSpec((None, None, d, tf), up),
            pl.BlockSpec((None, None, d, tf), up),
            pl.BlockSpec((None, None, tf, d), lambda i, f: (l, s, jnp.maximum(f - 1, 0), 0)),
        ],
        out_specs=pl.BlockSpec((tm, d), lambda i, f: (i, 0)),
        out_shape=jax.ShapeDtypeStruct((t, d), F32),
        scratch_shapes=[pltpu.VMEM((tm, d), BF16), pltpu.VMEM((tm, tf), BF16)],
        compiler_params=_cparams(("arbitrary", "arbitrary")),
        name="ffn",
    )(x, ng, sh, sc, gt, wg, wu, wd)


def _inproj_kernel(x_ref, ng_ref, sh_ref, sc_ref, w_ref, gain_ref, bd_ref, ws_ref, o_ref, side_ref, h_ref,
                   *, n_norm):
    j = pl.program_id(1)

    @pl.when(j == 0)
    def _():
        h = _norm_modulate(x_ref[...], ng_ref[...], sh_ref[0], sc_ref[0]).astype(BF16)
        h_ref[...] = h
        side_ref[...] = _dot(h, ws_ref[...])

    y = _dot(h_ref[...], w_ref[...])

    @pl.when(j < n_norm)
    def _():
        ms = _dot((y * y).astype(BF16), bd_ref[...])
        o_ref[...] = (y * lax.rsqrt(ms + EPS) * gain_ref[...]).astype(o_ref.dtype)

    @pl.when(j >= n_norm)
    def _():
        o_ref[...] = y.astype(o_ref.dtype)


def _inproj(x, ng, sh, sc, w, gain, bd, w_side, l, seq, n_norm):
    t, d = x.shape
    n = w.shape[-1]
    tm = min(1024, seq)
    tn = bd.shape[0]
    per_b = seq // tm
    vec = pl.BlockSpec((1, 1, d), lambda i, j: (i // per_b, 0, 0))
    return pl.pallas_call(
        functools.partial(_inproj_kernel, n_norm=n_norm),
        grid=(t // tm, n // tn),
        in_specs=[
            pl.BlockSpec((tm, d), lambda i, j: (i, 0)),
            pl.BlockSpec((1, d), lambda i, j: (0, 0)),
            vec, vec,
            pl.BlockSpec((None, d, tn), lambda i, j: (l, 0, j)),
            pl.BlockSpec((None, 1, tn), lambda i, j: (l, 0, j)),
            pl.BlockSpec((tn, tn), lambda i, j: (0, 0)),
            pl.BlockSpec((None, d, LANES), lambda i, j: (l, 0, 0)),
        ],
        out_specs=[pl.BlockSpec((tm, tn), lambda i, j: (i, j)),
                   pl.BlockSpec((tm, LANES), lambda i, j: (i, 0))],
        out_shape=[jax.ShapeDtypeStruct((t, n), BF16), jax.ShapeDtypeStruct((t, LANES), F32)],
        scratch_shapes=[pltpu.VMEM((tm, d), BF16)],
        compiler_params=_cparams(("arbitrary", "arbitrary")),
        name="inproj",
    )(x, ng, sh, sc, w, gain, bd, w_side)


def _rel_bucket_np(dist):
    n = np.maximum(dist, 0)
    max_exact = REL_BUCKETS // 2
    nf = np.maximum(n, 1).astype(np.float32)
    large = max_exact + (np.log(nf / np.float32(max_exact)) / np.float32(math.log(REL_MAX_DIST / max_exact))
                         * np.float32(REL_BUCKETS - max_exact)).astype(np.int32)
    large = np.minimum(large, REL_BUCKETS - 1)
    return np.where(n < max_exact, n, large).astype(np.int32)


def _bias_kernel(tab_ref, bkt_ref, o_ref, *, sub_far):
    h = pl.program_id(0)
    far = tab_ref[REL_BUCKETS - 1, h]
    for which in range(2):
        bkt = bkt_ref[which]
        acc = jnp.zeros(bkt.shape, F32)
        for b in range(REL_BUCKETS):
            acc = jnp.where(bkt == b, tab_ref[b, h], acc)
        o_ref[0, which] = (acc - jnp.where(h < sub_far, far, 0.0)) * LOG2E


def _bias_windows(rel_table, tq):
    width = tq + LANES
    t_l = np.arange(tq)[:, None]
    w = np.arange(width)[None, :]
    bkt = np.stack([_rel_bucket_np(t_l - w), _rel_bucket_np(t_l + LANES - w)]).astype(np.int32)
    nh = rel_table.shape[1]
    return pl.pallas_call(
        functools.partial(_bias_kernel, sub_far=GROUP_HEADS),
        grid=(nh,),
        in_specs=[
            pl.BlockSpec(memory_space=pltpu.SMEM),
            pl.BlockSpec((2, tq, width), lambda h: (0, 0, 0)),
        ],
        out_specs=pl.BlockSpec((1, 2, tq, width), lambda h: (h, 0, 0, 0)),
        out_shape=jax.ShapeDtypeStruct((nh, 2, tq, width), F32),
        compiler_params=_cparams(("arbitrary",)),
        name="rel_bias_windows",
    )(rel_table, jnp.asarray(bkt))


def _softmax_step_stacked(s, v, m_ref, l_ref, acc_ref):
    nh, rows, width = s.shape
    reps = width // LANES
    m_prev = m_ref[...]
    m_new = jnp.maximum(m_prev, jnp.max(s, axis=2, keepdims=True))
    p = jnp.exp2(s - jnp.concatenate([m_new] * reps, axis=2))
    alpha = jnp.exp2(m_prev - m_new)
    l_ref[...] = alpha * l_ref[...] + jnp.sum(p, axis=2, keepdims=True)
    m_ref[...] = m_new
    pv = _dot(p.astype(BF16).reshape(nh * rows, width), v).reshape(nh, rows, LANES)
    acc_ref[...] = alpha * acc_ref[...] + pv


def _store_pairs(o_ref, per_head):
    for j in range(PAIRS):
        lane = lax.broadcasted_iota(I32, per_head[0].shape, 1)
        o_ref[:, j * LANES:(j + 1) * LANES] = jnp.where(
            lane < HEAD_DIM, per_head[2 * j], per_head[2 * j + 1]).astype(o_ref.dtype)


DSA_TQ = 256
DSA_NB = DSA_TQ // LANES
DSA_KC1 = 512
DSA_KC = 512
DSA_SUB = DSA_KC // LANES
DSA_VALUE_STEPS = 26
DSA_VALUE_BLIND_STEPS = 10


def _dsa_kernel(q_ref, iq_ref, iw_ref, k2_ref, v2_ref, ik2_ref, bias_ref, tri_ref, o_ref,
                key_ref, wb_ref, iqm_ref, qm_ref, t_ref, cand_ref, m_ref, l_ref, acc_ref, *, k_top):
    i = pl.program_id(1)
    tq = DSA_TQ
    row0 = i * tq
    n_valid_cols = row0 + tq

    iw = iw_ref[...]
    for h in range(IDX_HEADS):
        col = GROUP_HEADS + h
        wb_ref[h] = jnp.broadcast_to(iw[:, col:col + 1], (tq, LANES))
        blk = iq_ref[:, (h // 2) * LANES:(h // 2 + 1) * LANES]
        iqm_ref[h] = jnp.where(_own_lane_mask(blk.shape, h), blk, jnp.zeros_like(blk))
    for h in range(GROUP_HEADS):
        blk = q_ref[:, (h // 2) * LANES:(h // 2 + 1) * LANES]
        qm_ref[h] = jnp.where(_own_lane_mask(blk.shape, h), blk, jnp.zeros_like(blk))

    qpos = row0 + lax.broadcasted_iota(I32, (tq, DSA_KC1), 0)
    iq_all = iqm_ref[...].reshape(IDX_HEADS * tq, LANES)
    wb_all = jnp.concatenate([wb_ref[...]] * (DSA_KC1 // LANES), axis=2)
    n1 = (n_valid_cols + DSA_KC1 - 1) // DSA_KC1

    def score_chunk(c, carry):
        start = pl.multiple_of(c * DSA_KC1, DSA_KC1)
        ik = ik2_ref[pl.ds(start, DSA_KC1), :]
        r = _nt_dot(iq_all, ik).reshape(IDX_HEADS, tq, DSA_KC1)
        score = jnp.sum(jnp.maximum(r, 0.0) * wb_all, axis=0)
        bits = lax.bitcast_convert_type(score, I32)
        key = bits ^ ((bits >> 31) & 0x7FFFFFFF)
        key = jnp.where(score == 0.0, 0, key)
        kpos = start + lax.broadcasted_iota(I32, (tq, DSA_KC1), 1)
        valid = kpos <= qpos
        key = jnp.where(valid, key, INT_MIN)
        smax, smin = carry
        hi = jnp.where(valid, score, -jnp.inf)
        lo = jnp.where(valid, score, jnp.inf)
        for u in range(DSA_KC1 // LANES):
            key_ref[c * (DSA_KC1 // LANES) + u] = key[:, u * LANES:(u + 1) * LANES]
            smax = jnp.maximum(smax, hi[:, u * LANES:(u + 1) * LANES])
            smin = jnp.minimum(smin, lo[:, u * LANES:(u + 1) * LANES])
        return smax, smin

    smax, smin = lax.fori_loop(0, n1, score_chunk, (jnp.full((tq, LANES), -jnp.inf, F32),
                                                    jnp.full((tq, LANES), jnp.inf, F32)))
    nsel = (n_valid_cols + DSA_KC - 1) // DSA_KC
    n1_blocks = n1 * (DSA_KC1 // LANES)

    @pl.when(n1_blocks < nsel * DSA_SUB)
    def _():
        for u in range(DSA_KC1 // LANES):
            key_ref[n1_blocks + u] = jnp.full((tq, LANES), INT_MIN, I32)

    def load_keys(c):
        blk = key_ref[pl.ds(c * DSA_SUB, DSA_SUB)]
        return jnp.concatenate([blk[u] for u in range(DSA_SUB)], axis=1)

    def count_ge(thr, strict):
        cand_ref[...] = thr

        def body(c, acc):
            out = []
            for r in range(DSA_NB):
                rows = slice(r * LANES, (r + 1) * LANES)
                cand = cand_ref[rows, :]
                a = acc[rows]
                for u in range(DSA_SUB):
                    keys = key_ref[c * DSA_SUB + u, rows, :]
                    hit = (keys > cand) if strict else (keys >= cand)
                    a = jnp.where(hit, a + 1.0, a)
                out.append(a)
            return jnp.concatenate(out, axis=0)
        acc = lax.fori_loop(0, nsel, body, jnp.zeros((tq, LANES), F32))
        return jnp.broadcast_to(jnp.sum(acc, axis=1, keepdims=True), (tq, LANES))

    t_ref[...] = jnp.full((tq, LANES), INT_MIN, I32)
    kf = float(k_top)

    def float_key(x):
        bits = lax.bitcast_convert_type(x, I32)
        return jnp.where(x == 0.0, 0, bits ^ ((bits >> 31) & 0x7FFFFFFF))

    def bisect_values():
        lower = jnp.broadcast_to(jnp.min(smin, axis=1, keepdims=True), (tq, LANES))
        upper = jnp.broadcast_to(jnp.max(smax, axis=1, keepdims=True), (tq, LANES))
        n_causal = (row0 + 1 + lax.broadcasted_iota(I32, (tq, LANES), 0)).astype(F32)
        keep_all = n_causal <= kf
        open0 = jnp.where(keep_all, 0.0, 1.0)

        def step(lower, upper, still_open):
            mid = 0.5 * lower + 0.5 * upper
            cnt = count_ge(float_key(mid), False)
            active = still_open > 0.0
            go_up = jnp.logical_and(active, cnt >= kf)
            go_down = jnp.logical_and(active, cnt < kf)
            lower = jnp.where(go_up, mid, lower)
            upper = jnp.where(go_down, mid, upper)
            still_open = jnp.where(jnp.logical_and(go_up, cnt == kf), 0.0, still_open)
            return lower, upper, still_open

        lower, upper, still_open = lax.fori_loop(
            0, DSA_VALUE_BLIND_STEPS, lambda _, c: step(*c), (lower, upper, open0))

        def cond(carry):
            it, _, _, still_open = carry
            return jnp.logical_and(it < DSA_VALUE_STEPS, jnp.max(still_open) > 0.0)

        def body(carry):
            it, lower, upper, still_open = carry
            return (it + 1,) + step(lower, upper, still_open)

        _, lower, _, still_open = lax.while_loop(
            cond, body, (jnp.int32(DSA_VALUE_BLIND_STEPS), lower, upper, still_open))
        return jnp.where(keep_all, INT_MIN, float_key(lower)), jnp.max(still_open) == 0.0

    def exact_select():
        def bit_step(it, carry):
            thr, cnt_thr = carry
            cand = thr + jnp.left_shift(jnp.int32(1), 31 - it)
            cnt = count_ge(cand, False)
            take = cnt >= kf
            return jnp.where(take, cand, thr), jnp.where(take, cnt, cnt_thr)

        thr0 = jnp.full((tq, LANES), INT_MIN, I32)
        cnt0 = jnp.full((tq, LANES), float(2 ** 24), F32)
        thr, cnt_thr = lax.fori_loop(0, 32, bit_step, (thr0, cnt0))
        t_ref[...] = thr

        excess = jnp.where(thr > INT_MIN, cnt_thr - kf, 0.0)

        @pl.when(jnp.max(excess) > 0.0)
        def _():
            need = kf - count_ge(thr, True)

            def tie_chunk(c, seen):
                keys = load_keys(c)
                eq = keys == _lane_tile(thr, DSA_SUB)
                eqf = jnp.where(eq, 1.0, 0.0)
                rank = _dot(eqf.astype(BF16), tri_ref[...]) + _lane_tile(seen, DSA_SUB)
                dropped = jnp.where(rank >= _lane_tile(need, DSA_SUB), INT_MIN, keys)
                keys = jnp.where(eq, dropped, keys)
                for u in range(DSA_SUB):
                    key_ref[c * DSA_SUB + u] = keys[:, u * LANES:(u + 1) * LANES]
                return seen + jnp.broadcast_to(jnp.sum(eqf, axis=1, keepdims=True), (tq, LANES))

            lax.fori_loop(0, nsel, tie_chunk, jnp.zeros((tq, LANES), F32))

    @pl.when(n_valid_cols > k_top)
    def _():
        thr_fast, finished = bisect_values()
        t_ref[...] = thr_fast

        @pl.when(jnp.logical_not(finished))
        def _():
            exact_select()

    thr = jnp.maximum(t_ref[...], INT_MIN + 1)

    m_ref[...] = jnp.full(m_ref.shape, NEG, F32)
    l_ref[...] = jnp.zeros(l_ref.shape, F32)
    acc_ref[...] = jnp.zeros(acc_ref.shape, F32)
    q_all = qm_ref[...].reshape(GROUP_HEADS * tq, LANES)

    def attend(k_blk, v_blk, madd, bias):
        width = k_blk.shape[0]
        s = _nt_dot(q_all, k_blk).reshape(GROUP_HEADS, tq, width) + madd[None]
        if bias is not None:
            s = s + bias
        _softmax_step_stacked(s, v_blk, m_ref, l_ref, acc_ref)

    nwin = DSA_NB + 1
    win_blk = jnp.maximum(i * DSA_NB - 1, 0)
    win_start = pl.multiple_of(win_blk * LANES, LANES)
    near_keys = key_ref[pl.ds(win_blk, nwin)]
    near_keys = jnp.concatenate([near_keys[u] for u in range(nwin)], axis=1)
    near_madd = jnp.where(near_keys >= _lane_tile(thr, nwin), 0.0, NEG)
    attend(k2_ref[pl.ds(win_start, nwin * LANES), :], v2_ref[pl.ds(win_start, nwin * LANES), :],
           near_madd, bias_ref[jnp.minimum(i, 1)])
    for u in range(nwin):
        key_ref[win_blk + u] = jnp.full((tq, LANES), INT_MIN, I32)

    n_far = (win_blk * LANES + DSA_KC - 1) // DSA_KC

    def far_chunk(c, carry):
        start = pl.multiple_of(c * DSA_KC, DSA_KC)
        madd = jnp.where(load_keys(c) >= _lane_tile(thr, DSA_SUB), 0.0, NEG)
        attend(k2_ref[pl.ds(start, DSA_KC), :], v2_ref[pl.ds(start, DSA_KC), :], madd, None)
        return carry

    lax.fori_loop(0, n_far, far_chunk, 0)
    _store_pairs(o_ref, [acc_ref[h] / l_ref[h] for h in range(GROUP_HEADS)])


def _mixer_dsa(p, side, bias, tri, batch, seq):
    t = p.shape[0]
    tq = DSA_TQ
    nq = seq // tq
    k_top = min(TOPK_MAX, seq // 4)
    cw = PT // LANES
    return pl.pallas_call(
        functools.partial(_dsa_kernel, k_top=k_top),
        grid=(batch, nq),
        in_specs=[
            pl.BlockSpec((tq, PT), lambda b, i: (b * nq + i, T_AQ)),
            pl.BlockSpec((tq, 2 * PT), lambda b, i: (b * nq + i, T_IQ // 2)),
            pl.BlockSpec((tq, LANES), lambda b, i: (b * nq + i, 0)),
            pl.BlockSpec((seq, LANES), lambda b, i: (b, T_K2 * cw), pipeline_mode=pl.Buffered(1)),
            pl.BlockSpec((seq, LANES), lambda b, i: (b, T_MISC * cw), pipeline_mode=pl.Buffered(1)),
            pl.BlockSpec((seq, LANES), lambda b, i: (b, T_MISC * cw + 2), pipeline_mode=pl.Buffered(1)),
            pl.BlockSpec((2, GROUP_HEADS, tq, tq + LANES), lambda b, i: (0, 0, 0, 0),
                         pipeline_mode=pl.Buffered(1)),
            pl.BlockSpec((DSA_KC, DSA_KC), lambda b, i: (0, 0), pipeline_mode=pl.Buffered(1)),
        ],
        out_specs=pl.BlockSpec((tq, GROUP_WIDTH), lambda b, i: (b * nq + i, 0)),
        out_shape=jax.ShapeDtypeStruct((t, GROUP_WIDTH), MIX_DTYPE),
        scratch_shapes=[
            pltpu.VMEM((max(seq // LANES, DSA_SUB, DSA_NB + 1), tq, LANES), I32),
            pltpu.VMEM((IDX_HEADS, tq, LANES), F32),
            pltpu.VMEM((IDX_HEADS, tq, LANES), BF16),
            pltpu.VMEM((GROUP_HEADS, tq, LANES), BF16),
            pltpu.VMEM((tq, LANES), I32),
            pltpu.VMEM((tq, LANES), I32),
            pltpu.VMEM((GROUP_HEADS, tq, LANES), F32),
            pltpu.VMEM((GROUP_HEADS, tq, LANES), F32),
            pltpu.VMEM((GROUP_HEADS, tq, LANES), F32),
        ],
        compiler_params=_cparams(("arbitrary", "arbitrary")),
        name="mixer_dsa",
    )(p, p, side, p, p, p, bias, tri)


SWA_TQ = 128


def _swa_kernel(sink_ref, q_ref, kp_ref, kc_ref, vp_ref, vc_ref, bias_ref, o_ref):
    i = pl.program_id(1)
    tq = SWA_TQ
    nh = GROUP_HEADS
    t_l = lax.broadcasted_iota(I32, (tq, 2 * LANES), 0)
    w = lax.broadcasted_iota(I32, (tq, 2 * LANES), 1)
    first = jnp.where(i > 0, t_l + 1, LANES)
    madd = jnp.where(w >= first, jnp.where(w <= t_l + LANES, 0.0, NEG), NEG)
    q_heads = []
    sink_rows = []
    for h in range(nh):
        blk = q_ref[:, (h // 2) * LANES:(h // 2 + 1) * LANES]
        q_heads.append(jnp.where(_own_lane_mask(blk.shape, h), blk, jnp.zeros_like(blk)))
        sink_rows.append(jnp.full((1, tq, LANES), sink_ref[h] * LOG2E, F32))
    sinks = jnp.concatenate(sink_rows, axis=0)
    k_win = jnp.concatenate([kp_ref[...], kc_ref[...]], axis=0)
    v_win = jnp.concatenate([vp_ref[...], vc_ref[...]], axis=0)
    s = _nt_dot(jnp.concatenate(q_heads, axis=0), k_win).reshape(nh, tq, 2 * LANES) + bias_ref[...] + madd[None]
    m = jnp.maximum(jnp.max(s, axis=2, keepdims=True), sinks)
    p = jnp.exp2(s - jnp.concatenate([m, m], axis=2))
    l = jnp.sum(p, axis=2, keepdims=True) + jnp.exp2(sinks - m)
    out = _dot(p.astype(BF16).reshape(nh * tq, 2 * LANES), v_win).reshape(nh, tq, LANES) / l
    _store_pairs(o_ref, [out[h] for h in range(nh)])


def _mixer_swa(p, sinks, bias, batch, seq):
    t = p.shape[0]
    tq = SWA_TQ
    nq = seq // tq
    cw = PT // LANES
    cur = lambda col: (lambda b, i: (b * nq + i, col))
    prev = lambda col: (lambda b, i: (b * nq + jnp.maximum(i - 1, 0), col))
    return pl.pallas_call(
        _swa_kernel,
        grid=(batch, nq),
        in_specs=[
            pl.BlockSpec(memory_space=pltpu.SMEM),
            pl.BlockSpec((tq, PT), lambda b, i: (b * nq + i, T_BQ)),
            pl.BlockSpec((tq, LANES), prev(T_K2 * cw + 1)),
            pl.BlockSpec((tq, LANES), cur(T_K2 * cw + 1)),
            pl.BlockSpec((tq, LANES), prev(T_MISC * cw + 1)),
            pl.BlockSpec((tq, LANES), cur(T_MISC * cw + 1)),
            pl.BlockSpec((GROUP_HEADS, tq, 2 * LANES), lambda b, i: (0, 0, 0)),
        ],
        out_specs=pl.BlockSpec((tq, GROUP_WIDTH), lambda b, i: (b * nq + i, 0)),
        out_shape=jax.ShapeDtypeStruct((t, GROUP_WIDTH), MIX_DTYPE),
        compiler_params=_cparams(("arbitrary", "arbitrary")),
        name="mixer_swa",
    )(sinks, p, p, p, p, p, bias)


def _logsig(x):
    return jnp.minimum(x, 0.0) - jnp.log(1.0 + jnp.exp(-jnp.abs(x)))


def _foxcum_kernel(fb_ref, f_ref, tri_ref, o_ref, *, nchunk):
    def body(c, carry):
        start = pl.multiple_of(c * LANES, LANES)
        lf = _logsig(f_ref[pl.ds(start, LANES), :] + fb_ref[...]).T
        p1 = lf.astype(BF16)
        r1 = lf - p1.astype(F32)
        p2 = r1.astype(BF16)
        p3 = (r1 - p2.astype(F32)).astype(BF16)
        tri = tri_ref[...]
        cum = _dot(p1, tri) + _dot(p2, tri) + _dot(p3, tri) + carry
        o_ref[0, c] = cum[:GROUP_HEADS]
        return jnp.broadcast_to(cum[:, LANES - 1:LANES], cum.shape)

    lax.fori_loop(0, nchunk, body, jnp.zeros((LANES, LANES), F32))


def _fox_cum(side, fb_row, tri_incl, batch, seq):
    nchunk = seq // LANES
    return pl.pallas_call(
        functools.partial(_foxcum_kernel, nchunk=nchunk),
        grid=(batch,),
        in_specs=[
            pl.BlockSpec((1, LANES), lambda b: (0, 0)),
            pl.BlockSpec((seq, LANES), lambda b: (b, 0)),
            pl.BlockSpec((LANES, LANES), lambda b: (0, 0)),
        ],
        out_specs=pl.BlockSpec((1, nchunk, GROUP_HEADS, LANES), lambda b: (b, 0, 0, 0)),
        out_shape=jax.ShapeDtypeStruct((batch, nchunk, GROUP_HEADS, LANES), F32),
        compiler_params=_cparams(("arbitrary",)),
        name="fox_cumsum",
    )(fb_row, side, tri_incl)


FOX_TQ = 256
FOX_KC = 512
FOX_SUB = FOX_KC // LANES


def _fox_kernel(q_ref, g_ref, k_ref, v_ref, cum_ref, o_ref, qm_ref, m_ref, l_ref, acc_ref):
    i = pl.program_id(1)
    tq = FOX_TQ
    for h in range(GROUP_HEADS):
        blk = q_ref[:, (h // 2) * LANES:(h // 2 + 1) * LANES]
        qm_ref[h] = jnp.where(_own_lane_mask(blk.shape, h), blk, jnp.zeros_like(blk))
    m_ref[...] = jnp.full(m_ref.shape, NEG, F32)
    l_ref[...] = jnp.zeros(l_ref.shape, F32)
    acc_ref[...] = jnp.zeros(acc_ref.shape, F32)
    f_ref0 = cum_ref[0, i * (tq // LANES)][:, 0:1]

    def chunk(c, masked):
        start = pl.multiple_of(c * FOX_KC, FOX_KC)
        cum = cum_ref[0, pl.ds(c * FOX_SUB, FOX_SUB)]
        cum = jnp.concatenate([cum[u] for u in range(FOX_SUB)], axis=1)
        fbias = (f_ref0 - cum) * LOG2E
        per_head = []
        for pr in range(PAIRS):
            pair = slice(pr * LANES, (pr + 1) * LANES)
            q_pair = qm_ref[2 * pr:2 * pr + 2].reshape(2 * tq, LANES)
            s_pair = _nt_dot(q_pair, k_ref[pl.ds(start, FOX_KC), pair])
            for e in range(2):
                h = 2 * pr + e
                per_head.append(s_pair[e * tq:(e + 1) * tq] + fbias[h:h + 1, :])
        s = jnp.stack(per_head)
        if masked:
            shape = (GROUP_HEADS, tq, FOX_KC)
            ok = start + lax.broadcasted_iota(I32, shape, 2) <= i * tq + lax.broadcasted_iota(I32, shape, 1)
            s = jnp.where(ok, s, NEG)
        reps = FOX_KC // LANES
        m_prev = m_ref[...]
        m_new = jnp.maximum(m_prev, jnp.max(s, axis=2, keepdims=True))
        p = jnp.exp2(s - jnp.concatenate([m_new] * reps, axis=2))
        alpha = jnp.exp2(m_prev - m_new)
        l_ref[...] = alpha * l_ref[...] + jnp.sum(p, axis=2, keepdims=True)
        m_ref[...] = m_new
        p = p.astype(BF16)
        pvs = []
        for pr in range(PAIRS):
            p_pair = p[2 * pr:2 * pr + 2].reshape(2 * tq, FOX_KC)
            pvs.append(_dot(p_pair, v_ref[pl.ds(start, FOX_KC), pr * LANES:(pr + 1) * LANES]))
        acc_ref[...] = alpha * acc_ref[...] + jnp.concatenate(pvs, axis=0).reshape(GROUP_HEADS, tq, LANES)

    def body(c, carry):
        chunk(c, False)
        return carry

    n_full = (i * tq) // FOX_KC
    lax.fori_loop(0, n_full, body, 0)
    chunk(n_full, True)
    outs = []
    for h in range(GROUP_HEADS):
        gate = _sigmoid(g_ref[:, (h // 2) * LANES:(h // 2 + 1) * LANES].astype(F32))
        outs.append(acc_ref[h] / l_ref[h] * gate)
    _store_pairs(o_ref, outs)


def _mixer_fox(p, cum, batch, seq):
    t = p.shape[0]
    tq = FOX_TQ
    nq = seq // tq
    return pl.pallas_call(
        _fox_kernel,
        grid=(batch, nq),
        in_specs=[
            pl.BlockSpec((tq, PT), lambda b, i: (b * nq + i, T_CQ)),
            pl.BlockSpec((tq, PT), lambda b, i: (b * nq + i, T_CG)),
            pl.BlockSpec((seq, PT), lambda b, i: (b, T_CK)),
            pl.BlockSpec((seq, PT), lambda b, i: (b, T_CV)),
            pl.BlockSpec((1, seq // LANES, GROUP_HEADS, LANES), lambda b, i: (b, 0, 0, 0)),
        ],
        out_specs=pl.BlockSpec((tq, GROUP_WIDTH), lambda b, i: (b * nq + i, 0)),
        out_shape=jax.ShapeDtypeStruct((t, GROUP_WIDTH), MIX_DTYPE),
        scratch_shapes=[
            pltpu.VMEM((GROUP_HEADS, tq, LANES), BF16),
            pltpu.VMEM((GROUP_HEADS, tq, LANES), F32),
            pltpu.VMEM((GROUP_HEADS, tq, LANES), F32),
            pltpu.VMEM((GROUP_HEADS, tq, LANES), F32),
        ],
        compiler_params=_cparams(("arbitrary", "arbitrary")),
        name="mixer_fox",
    )(p, p, p, p, cum)


STK_TQ = 128


def _stick_kernel(q_ref, k_ref, v_ref, tri_ref, o_ref, qm_ref, acc_ref, r_ref):
    i = pl.program_id(1)
    tq = STK_TQ
    nh = GROUP_HEADS
    for h in range(nh):
        blk = q_ref[:, (h // 2) * LANES:(h // 2 + 1) * LANES]
        qm_ref[h] = jnp.where(_own_lane_mask(blk.shape, h), blk, jnp.zeros_like(blk))
    acc_ref[...] = jnp.zeros(acc_ref.shape, F32)
    r_ref[...] = jnp.zeros(r_ref.shape, F32)
    before = (lax.broadcasted_iota(I32, (nh, tq, LANES), 2) < lax.broadcasted_iota(I32, (nh, tq, LANES), 1))
    tri = tri_ref[...]

    def block(j, masked):
        start = pl.multiple_of(j * LANES, LANES)
        zs = []
        for pr in range(PAIRS):
            q_pair = qm_ref[2 * pr:2 * pr + 2].reshape(2 * tq, LANES)
            zs.append(_nt_dot(q_pair, k_ref[pl.ds(start, LANES), pr * LANES:(pr + 1) * LANES]))
        z = jnp.concatenate(zs, axis=0).reshape(nh, tq, LANES)
        lsz = _logsig(z)
        u = lsz - z
        if masked:
            u = jnp.where(before, u, 0.0)
        u2 = u.reshape(nh * tq, LANES)
        u_hi = u2.astype(BF16)
        u_lo = (u2 - u_hi.astype(F32)).astype(BF16)
        nearer = (_dot(u_hi, tri) + _dot(u_lo, tri)).reshape(nh, tq, LANES)
        run = r_ref[...]
        w = jnp.exp(lsz + nearer + run)
        if masked:
            w = jnp.where(before, w, 0.0)
        wb = w.astype(BF16)
        pvs = []
        for pr in range(PAIRS):
            w_pair = wb[2 * pr:2 * pr + 2].reshape(2 * tq, LANES)
            pvs.append(_dot(w_pair, v_ref[pl.ds(start, LANES), pr * LANES:(pr + 1) * LANES]))
        acc_ref[...] += jnp.concatenate(pvs, axis=0).reshape(nh, tq, LANES)
        run = run + jnp.sum(u, axis=2, keepdims=True)
        r_ref[...] = run
        return jnp.max(run)

    rmax = block(i, True)

    def cond(carry):
        j, rmax = carry
        return jnp.logical_and(j >= 0, rmax >= STICK_EXIT)

    def body(carry):
        j, _ = carry
        return j - 1, block(j, False)

    lax.while_loop(cond, body, (i - 1, rmax))
    _store_pairs(o_ref, [acc_ref[h] for h in range(nh)])


def _mixer_stick(p, tri_excl, batch, seq):
    t = p.shape[0]
    tq = STK_TQ
    nq = seq // tq
    return pl.pallas_call(
        _stick_kernel,
        grid=(batch, nq),
        in_specs=[
            pl.BlockSpec((tq, PT), lambda b, i: (b * nq + i, T_DQ)),
            pl.BlockSpec((seq, PT), lambda b, i: (b, T_DK)),
            pl.BlockSpec((seq, PT), lambda b, i: (b, T_DV)),
            pl.BlockSpec((LANES, LANES), lambda b, i: (0, 0)),
        ],
        out_specs=pl.BlockSpec((tq, GROUP_WIDTH), lambda b, i: (b * nq + i, 0)),
        out_shape=jax.ShapeDtypeStruct((t, GROUP_WIDTH), MIX_DTYPE),
        scratch_shapes=[pltpu.VMEM((GROUP_HEADS, tq, LANES), BF16),
                        pltpu.VMEM((GROUP_HEADS, tq, LANES), F32),
                        pltpu.VMEM((GROUP_HEADS, tq, LANES), F32)],
        compiler_params=_cparams(("arbitrary", "arbitrary")),
        name="mixer_stick",
    )(p, p, p, tri_excl)


def _outproj_kernel(x_ref, gt_ref, oa_ref, ob_ref, oc_ref, od_ref, gg_ref, w_ref, o_ref):
    acc = None
    for m, ref in enumerate((oa_ref, ob_ref, oc_ref, od_ref)):
        o = ref[...].astype(F32)
        y = o * lax.rsqrt(jnp.mean(o * o, axis=-1, keepdims=True) + EPS) * gg_ref[m:m + 1, :]
        part = _dot(y.astype(BF16), w_ref[m * GROUP_WIDTH:(m + 1) * GROUP_WIDTH, :])
        acc = part if acc is None else acc + part
    o_ref[...] = x_ref[...] + gt_ref[0] * acc


def _outproj(x, gt, outs, gg, w, l, seq):
    t, d = x.shape
    tm = 512
    per_b = seq // tm
    mix = pl.BlockSpec((tm, GROUP_WIDTH), lambda i: (i, 0))
    return pl.pallas_call(
        _outproj_kernel,
        grid=(t // tm,),
        in_specs=[
            pl.BlockSpec((tm, d), lambda i: (i, 0)),
            pl.BlockSpec((1, 1, d), lambda i: (i // per_b, 0, 0)),
            mix, mix, mix, mix,
            pl.BlockSpec((None, N_MIXERS, GROUP_WIDTH), lambda i: (l, 0, 0)),
            pl.BlockSpec((None, N_MIXERS * GROUP_WIDTH, d), lambda i: (l, 0, 0)),
        ],
        out_specs=pl.BlockSpec((tm, d), lambda i: (i, 0)),
        out_shape=jax.ShapeDtypeStruct((t, d), F32),
        compiler_params=_cparams(("arbitrary",)),
        name="outproj",
    )(x, gt, *outs, gg, w)


def _prep_in_weights(w_in, qk_g):
    depth, d, _ = w_in.shape
    sizes = (GROUP_WIDTH, HEAD_DIM, HEAD_DIM, IDX_HEADS * IDX_DIM, IDX_DIM, IDX_HEADS,
             GROUP_WIDTH, HEAD_DIM, HEAD_DIM,
             GROUP_WIDTH, GROUP_WIDTH, GROUP_WIDTH, GROUP_HEADS, GROUP_WIDTH,
             GROUP_WIDTH, GROUP_WIDTH, GROUP_WIDTH)
    pts = np.cumsum(sizes)[:-1].tolist()
    (a_q, a_k, a_v, a_iq, a_ik, a_iw, b_q, b_k, b_v,
     c_q, c_k, c_v, c_f, c_g, d_q, d_k, d_v) = jnp.split(w_in, pts, axis=-1)
    z = lambda n: jnp.zeros((depth, d, n), w_in.dtype)
    qscale = HEAD_DIM ** -0.5
    main = jnp.concatenate([
        a_q, b_q, c_q, c_k,
        a_k, a_k, b_k, b_k, z(2 * LANES),
        c_v, c_g, d_q * qscale, d_k, d_v, a_iq * (IDX_DIM ** -0.5),
        a_v, a_v, b_v, b_v, a_ik, a_ik, z(LANES),
    ], axis=-1).astype(BF16)
    side = jnp.concatenate([c_f, a_iw * (IDX_HEADS ** -0.5), z(LANES - GROUP_HEADS - IDX_HEADS)],
                           axis=-1).astype(BF16)
    rep = lambda g, n: jnp.tile(g, (1, n))
    sscale = qscale * LOG2E
    gain = jnp.concatenate([
        rep(qk_g[:, 0], 8) * sscale, rep(qk_g[:, 2], 8) * sscale, rep(qk_g[:, 4], 8) * sscale,
        rep(qk_g[:, 5], 8), rep(qk_g[:, 1], 2), rep(qk_g[:, 3], 2),
        jnp.zeros((depth, P_COLS - 4 * PT - 2 * LANES), F32),
    ], axis=-1).reshape(depth, 1, P_COLS)
    return main, side, gain


def kernel(x, c, w_ada, b_ada, norm_g, w_in, qk_g, forget_b, sinks, rel_table, group_g, w_out,
           w_ffn_gate, w_ffn_up, w_ffn_down):
    batch, seq, d = x.shape
    depth = w_ada.shape[0]
    t = batch * seq

    mod = _modulation(c, w_ada, b_ada)
    wg = w_ffn_gate.astype(BF16)
    wu = w_ffn_up.astype(BF16)
    wd = w_ffn_down.astype(BF16)
    wo = w_out.astype(BF16)
    w_main, w_side, gain = _prep_in_weights(w_in.astype(BF16), qk_g)

    r = np.arange(PT)
    bd = jnp.asarray((r[:, None] // HEAD_DIM == r[None, :] // HEAD_DIM).astype(np.float32) / HEAD_DIM, BF16)
    r = np.arange(LANES)
    tri_incl = jnp.asarray(r[:, None] <= r[None, :], BF16)
    tri_after = jnp.asarray(r[:, None] > r[None, :], BF16)
    r = np.arange(DSA_KC)
    tri_before = jnp.asarray(r[:, None] < r[None, :], BF16)

    bias_dsa = jnp.swapaxes(_bias_windows(rel_table, DSA_TQ)[:GROUP_HEADS], 0, 1)
    bias_swa = _bias_windows(rel_table, SWA_TQ)[GROUP_HEADS:, 1]
    fb_rows = jnp.zeros((depth, 1, LANES), F32).at[:, 0, :GROUP_HEADS].set(forget_b)
    gg = group_g.reshape(depth, N_MIXERS, GROUP_WIDTH)

    xt = x.reshape(t, d)
    for l in range(depth):
        parts = [m.reshape(batch, 1, d) for m in jnp.split(mod[l], 9, axis=-1)]
        sh1, sc1, g1, sh2, sc2, g2, sh3, sc3, g3 = parts
        xt = _ffn(xt, norm_g[l, 0:1], sh1, sc1, g1, wg, wu, wd, l, 0, seq)
        ng2 = norm_g[l, 1:2]
        p, side = _inproj(xt, ng2, sh2, sc2, w_main, gain, bd, w_side, l, seq, N_NORM_TILES)
        o_a = _mixer_dsa(p, side, bias_dsa, tri_before, batch, seq)
        o_b = _mixer_swa(p, sinks[l], bias_swa, batch, seq)
        cum = _fox_cum(side, fb_rows[l], tri_incl, batch, seq)
        o_c = _mixer_fox(p, cum, batch, seq)
        o_d = _mixer_stick(p, tri_after, batch, seq)
        xt = _outproj(xt, g2, (o_a, o_b, o_c, o_d), gg, wo, l, seq)
        xt = _ffn(xt, norm_g[l, 2:3], sh3, sc3, g3, wg, wu, wd, l, 1, seq)
    return xt.reshape(batch, seq, d)
```

```python
import functools
import math

import numpy as np
import jax
import jax.numpy as jnp
from jax import lax
from jax.experimental import pallas as pl
from jax.experimental.pallas import tpu as pltpu

F32 = jnp.float32
BF16 = jnp.bfloat16
I32 = jnp.int32

HEAD_DIM = 64
N_MIXERS = 4
GROUP_HEADS = 8
GROUP_WIDTH = GROUP_HEADS * HEAD_DIM
IDX_HEADS = 16
IDX_DIM = 64
TOPK_MAX = 256
WINDOW = 128
REL_BUCKETS = 32
REL_MAX_DIST = 128
EPS = 1e-6

LANES = 128
PAIRS = GROUP_HEADS // 2
NEG = -1e30
LOG2E = math.log2(math.e)
INT_MIN = -2 ** 31
STICK_EXIT = -110.0
VMEM_LIMIT = 56 * 1024 * 1024
MIX_DTYPE = BF16

PT = 512
T_AQ, T_BQ, T_CQ, T_CK, T_K2 = 0, 1, 2, 3, 4
N_NORM_TILES = 5
T_CV, T_CG, T_DQ, T_DK, T_DV, T_IQ, T_MISC = 5, 6, 7, 8, 9, 10, 12
N_TILES = 13
P_COLS = N_TILES * PT


def _cparams(sem, vmem=VMEM_LIMIT):
    return pltpu.CompilerParams(dimension_semantics=sem, vmem_limit_bytes=vmem)


def _nt_dot(a, b):
    return lax.dot_general(a, b, (((1,), (1,)), ((), ())), preferred_element_type=F32)


def _dot(a, b):
    return jnp.dot(a, b, preferred_element_type=F32)


def _sigmoid(x):
    return 1.0 / (1.0 + jnp.exp(-x))


def _lane_tile(x, n):
    return x if n == 1 else jnp.concatenate([x] * n, axis=1)


def _own_lane_mask(shape, h):
    lane = lax.broadcasted_iota(I32, shape, 1)
    return (lane < HEAD_DIM) if h % 2 == 0 else (lane >= HEAD_DIM)


def _mod_kernel(c_ref, w_ref, b_ref, o_ref):
    c = c_ref[...]
    cond = (c * _sigmoid(c)).astype(BF16)
    o_ref[0] = _dot(cond, w_ref[0].astype(BF16)) + b_ref[0]


def _modulation(c, w_ada, b_ada):
    depth, d, n = w_ada.shape
    b = c.shape[0]
    rows = 8
    tn = 1024
    c_pad = jnp.zeros((rows, d), F32).at[:b].set(c)
    out = pl.pallas_call(
        _mod_kernel,
        grid=(depth, n // tn),
        in_specs=[
            pl.BlockSpec((rows, d), lambda l, j: (0, 0)),
            pl.BlockSpec((1, d, tn), lambda l, j: (l, 0, j)),
            pl.BlockSpec((1, 1, tn), lambda l, j: (l, 0, j)),
        ],
        out_specs=pl.BlockSpec((1, rows, tn), lambda l, j: (l, 0, j)),
        out_shape=jax.ShapeDtypeStruct((depth, rows, n), F32),
        compiler_params=_cparams(("arbitrary", "arbitrary")),
        name="adaln_mod",
    )(c_pad, w_ada, b_ada.reshape(depth, 1, n))
    return out[:, :b]


def _norm_modulate(x, ng, sh, sc):
    ms = jnp.mean(x * x, axis=-1, keepdims=True)
    y = x * lax.rsqrt(ms + EPS) * ng
    return y * (1.0 + sc) + sh


def _ffn_kernel(x_ref, ng_ref, sh_ref, sc_ref, gt_ref, wg_ref, wu_ref, wd_ref, o_ref, h_ref, a_ref):
    f = pl.program_id(1)
    last = pl.num_programs(1) - 1

    def activation():
        h = h_ref[...]
        g = _dot(h, wg_ref[...])
        u = _dot(h, wu_ref[...])
        return (g * _sigmoid(g) * u).astype(BF16)

    @pl.when(f == 0)
    def _():
        h = _norm_modulate(x_ref[...], ng_ref[...], sh_ref[0], sc_ref[0])
        h_ref[...] = h.astype(BF16)
        o_ref[...] = jnp.zeros_like(o_ref)
        a_ref[...] = activation()

    @pl.when(jnp.logical_and(f > 0, f < last))
    def _():
        o_ref[...] += _dot(a_ref[...], wd_ref[...])
        a_ref[...] = activation()

    @pl.when(f == last)
    def _():
        acc = o_ref[...] + _dot(a_ref[...], wd_ref[...])
        o_ref[...] = x_ref[...] + 0.5 * gt_ref[0] * acc


def _ffn(x, ng, sh, sc, gt, wg, wu, wd, l, s, seq):
    t, d = x.shape
    tm, tf = min(1024, seq), 512
    nf = wg.shape[-1] // tf
    per_b = seq // tm
    vec = pl.BlockSpec((1, 1, d), lambda i, f: (i // per_b, 0, 0))
    up = lambda i, f: (l, s, 0, jnp.minimum(f, nf - 1))
    return pl.pallas_call(
        _ffn_kernel,
        grid=(t // tm, nf + 1),
        in_specs=[
            pl.BlockSpec((tm, d), lambda i, f: (i, 0), pipeline_mode=pl.Buffered(1)),
            pl.BlockSpec((1, d), lambda i, f: (0, 0)),
            vec, vec, vec,
            pl.BlockSpec((None, None, d, tf), up),
            pl.BlockSpec((None, None, d, tf), up),
            pl.BlockSpec((None, None, tf, d), lambda i, f: (l, s, jnp.maximum(f - 1, 0), 0)),
        ],
        out_specs=pl.BlockSpec((tm, d), lambda i, f: (i, 0)),
        out_shape=jax.ShapeDtypeStruct((t, d), F32),
        scratch_shapes=[pltpu.VMEM((tm, d), BF16), pltpu.VMEM((tm, tf), BF16)],
        compiler_params=_cparams(("arbitrary", "arbitrary")),
        name="ffn",
    )(x, ng, sh, sc, gt, wg, wu, wd)


def _inproj_kernel(x_ref, ng_ref, sh_ref, sc_ref, w_ref, gain_ref, bd_ref, ws_ref, o_ref, side_ref, h_ref,
                   *, n_norm):
    j = pl.program_id(1)

    @pl.when(j == 0)
    def _():
        h = _norm_modulate(x_ref[...], ng_ref[...], sh_ref[0], sc_ref[0]).astype(BF16)
        h_ref[...] = h
        side_ref[...] = _dot(h, ws_ref[...])

    y = _dot(h_ref[...], w_ref[...])

    @pl.when(j < n_norm)
    def _():
        ms = _dot((y * y).astype(BF16), bd_ref[...])
        o_ref[...] = (y * lax.rsqrt(ms + EPS) * gain_ref[...]).astype(o_ref.dtype)

    @pl.when(j >= n_norm)
    def _():
        o_ref[...] = y.astype(o_ref.dtype)


def _inproj(x, ng, sh, sc, w, gain, bd, w_side, l, seq, n_norm):
    t, d = x.shape
    n = w.shape[-1]
    tm = min(1024, seq)
    tn = bd.shape[0]
    per_b = seq // tm
    vec = pl.BlockSpec((1, 1, d), lambda i, j: (i // per_b, 0, 0))
    return pl.pallas_call(
        functools.partial(_inproj_kernel, n_norm=n_norm),
        grid=(t // tm, n // tn),
        in_specs=[
            pl.BlockSpec((tm, d), lambda i, j: (i, 0)),
            pl.BlockSpec((1, d), lambda i, j: (0, 0)),
            vec, vec,
            pl.BlockSpec((None, d, tn), lambda i, j: (l, 0, j)),
            pl.BlockSpec((None, 1, tn), lambda i, j: (l, 0, j)),
            pl.BlockSpec((tn, tn), lambda i, j: (0, 0)),
            pl.BlockSpec((None, d, LANES), lambda i, j: (l, 0, 0)),
        ],
        out_specs=[pl.BlockSpec((tm, tn), lambda i, j: (i, j)),
                   pl.BlockSpec((tm, LANES), lambda i, j: (i, 0))],
        out_shape=[jax.ShapeDtypeStruct((t, n), BF16), jax.ShapeDtypeStruct((t, LANES), F32)],
        scratch_shapes=[pltpu.VMEM((tm, d), BF16)],
        compiler_params=_cparams(("arbitrary", "arbitrary")),
        name="inproj",
    )(x, ng, sh, sc, w, gain, bd, w_side)


def _rel_bucket_np(dist):
    n = np.maximum(dist, 0)
    max_exact = REL_BUCKETS // 2
    nf = np.maximum(n, 1).astype(np.float32)
    large = max_exact + (np.log(nf / np.float32(max_exact)) / np.float32(math.log(REL_MAX_DIST / max_exact))
                         * np.float32(REL_BUCKETS - max_exact)).astype(np.int32)
    large = np.minimum(large, REL_BUCKETS - 1)
    return np.where(n < max_exact, n, large).astype(np.int32)


def _bias_kernel(tab_ref, bkt_ref, o_ref, *, sub_far):
    h = pl.program_id(0)
    far = tab_ref[REL_BUCKETS - 1, h]
    for which in range(2):
        bkt = bkt_ref[which]
        acc = jnp.zeros(bkt.shape, F32)
        for b in range(REL_BUCKETS):
            acc = jnp.where(bkt == b, tab_ref[b, h], acc)
        o_ref[0, which] = (acc - jnp.where(h < sub_far, far, 0.0)) * LOG2E


def _bias_windows(rel_table, tq):
    width = tq + LANES
    t_l = np.arange(tq)[:, None]
    w = np.arange(width)[None, :]
    bkt = np.stack([_rel_bucket_np(t_l - w), _rel_bucket_np(t_l + LANES - w)]).astype(np.int32)
    nh = rel_table.shape[1]
    return pl.pallas_call(
        functools.partial(_bias_kernel, sub_far=GROUP_HEADS),
        grid=(nh,),
        in_specs=[
            pl.BlockSpec(memory_space=pltpu.SMEM),
            pl.BlockSpec((2, tq, width), lambda h: (0, 0, 0)),
        ],
        out_specs=pl.BlockSpec((1, 2, tq, width), lambda h: (h, 0, 0, 0)),
        out_shape=jax.ShapeDtypeStruct((nh, 2, tq, width), F32),
        compiler_params=_cparams(("arbitrary",)),
        name="rel_bias_windows",
    )(rel_table, jnp.asarray(bkt))


def _softmax_step_stacked(s, v, m_ref, l_ref, acc_ref):
    nh, rows, width = s.shape
    reps = width // LANES
    m_prev = m_ref[...]
    m_new = jnp.maximum(m_prev, jnp.max(s, axis=2, keepdims=True))
    p = jnp.exp2(s - jnp.concatenate([m_new] * reps, axis=2))
    alpha = jnp.exp2(m_prev - m_new)
    l_ref[...] = alpha * l_ref[...] + jnp.sum(p, axis=2, keepdims=True)
    m_ref[...] = m_new
    pv = _dot(p.astype(BF16).reshape(nh * rows, width), v).reshape(nh, rows, LANES)
    acc_ref[...] = alpha * acc_ref[...] + pv


def _store_pairs(o_ref, per_head):
    for j in range(PAIRS):
        lane = lax.broadcasted_iota(I32, per_head[0].shape, 1)
        o_ref[:, j * LANES:(j + 1) * LANES] = jnp.where(
            lane < HEAD_DIM, per_head[2 * j], per_head[2 * j + 1]).astype(o_ref.dtype)


DSA_TQ = 256
DSA_NB = DSA_TQ // LANES
DSA_KC1 = 512
DSA_KC = 512
DSA_SUB = DSA_KC // LANES
DSA_VALUE_STEPS = 26
DSA_VALUE_BLIND_STEPS = 10


def _dsa_kernel(q_ref, iq_ref, iw_ref, k2_ref, v2_ref, ik2_ref, bias_ref, tri_ref, o_ref,
                key_ref, wb_ref, iqm_ref, qm_ref, t_ref, cand_ref, m_ref, l_ref, acc_ref, *, k_top):
    i = pl.program_id(1)
    tq = DSA_TQ
    row0 = i * tq
    n_valid_cols = row0 + tq

    iw = iw_ref[...]
    for h in range(IDX_HEADS):
        col = GROUP_HEADS + h
        wb_ref[h] = jnp.broadcast_to(iw[:, col:col + 1], (tq, LANES))
        blk = iq_ref[:, (h // 2) * LANES:(h // 2 + 1) * LANES]
        iqm_ref[h] = jnp.where(_own_lane_mask(blk.shape, h), blk, jnp.zeros_like(blk))
    for h in range(GROUP_HEADS):
        blk = q_ref[:, (h // 2) * LANES:(h // 2 + 1) * LANES]
        qm_ref[h] = jnp.where(_own_lane_mask(blk.shape, h), blk, jnp.zeros_like(blk))

    qpos = row0 + lax.broadcasted_iota(I32, (tq, DSA_KC1), 0)
    iq_all = iqm_ref[...].reshape(IDX_HEADS * tq, LANES)
    wb_all = jnp.concatenate([wb_ref[...]] * (DSA_KC1 // LANES), axis=2)
    n1 = (n_valid_cols + DSA_KC1 - 1) // DSA_KC1

    def score_chunk(c, carry):
        start = pl.multiple_of(c * DSA_KC1, DSA_KC1)
        ik = ik2_ref[pl.ds(start, DSA_KC1), :]
        r = _nt_dot(iq_all, ik).reshape(IDX_HEADS, tq, DSA_KC1)
        score = jnp.sum(jnp.maximum(r, 0.0) * wb_all, axis=0)
        bits = lax.bitcast_convert_type(score, I32)
        key = bits ^ ((bits >> 31) & 0x7FFFFFFF)
        key = jnp.where(score == 0.0, 0, key)
        kpos = start + lax.broadcasted_iota(I32, (tq, DSA_KC1), 1)
        valid = kpos <= qpos
        key = jnp.where(valid, key, INT_MIN)
        smax, smin = carry
        hi = jnp.where(valid, score, -jnp.inf)
        lo = jnp.where(valid, score, jnp.inf)
        for u in range(DSA_KC1 // LANES):
            key_ref[c * (DSA_KC1 // LANES) + u] = key[:, u * LANES:(u + 1) * LANES]
            smax = jnp.maximum(smax, hi[:, u * LANES:(u + 1) * LANES])
            smin = jnp.minimum(smin, lo[:, u * LANES:(u + 1) * LANES])
        return smax, smin

    smax, smin = lax.fori_loop(0, n1, score_chunk, (jnp.full((tq, LANES), -jnp.inf, F32),
                                                    jnp.full((tq, LANES), jnp.inf, F32)))
    nsel = (n_valid_cols + DSA_KC - 1) // DSA_KC
    n1_blocks = n1 * (DSA_KC1 // LANES)

    @pl.when(n1_blocks < nsel * DSA_SUB)
    def _():
        for u in range(DSA_KC1 // LANES):
            key_ref[n1_blocks + u] = jnp.full((tq, LANES), INT_MIN, I32)

    def load_keys(c):
        blk = key_ref[pl.ds(c * DSA_SUB, DSA_SUB)]
        return jnp.concatenate([blk[u] for u in range(DSA_SUB)], axis=1)

    def count_ge(thr, strict):
        cand_ref[...] = thr

        def body(c, acc):
            out = []
            for r in range(DSA_NB):
                rows = slice(r * LANES, (r + 1) * LANES)
                cand = cand_ref[rows, :]
                a = acc[rows]
                for u in range(DSA_SUB):
                    keys = key_ref[c * DSA_SUB + u, rows, :]
                    hit = (keys > cand) if strict else (keys >= cand)
                    a = jnp.where(hit, a + 1.0, a)
                out.append(a)
            return jnp.concatenate(out, axis=0)
        acc = lax.fori_loop(0, nsel, body, jnp.zeros((tq, LANES), F32))
        return jnp.broadcast_to(jnp.sum(acc, axis=1, keepdims=True), (tq, LANES))

    t_ref[...] = jnp.full((tq, LANES), INT_MIN, I32)
    kf = float(k_top)

    def float_key(x):
        bits = lax.bitcast_convert_type(x, I32)
        return jnp.where(x == 0.0, 0, bits ^ ((bits >> 31) & 0x7FFFFFFF))

    def bisect_values():
        lower = jnp.broadcast_to(jnp.min(smin, axis=1, keepdims=True), (tq, LANES))
        upper = jnp.broadcast_to(jnp.max(smax, axis=1, keepdims=True), (tq, LANES))
        n_causal = (row0 + 1 + lax.broadcasted_iota(I32, (tq, LANES), 0)).astype(F32)
        keep_all = n_causal <= kf
        open0 = jnp.where(keep_all, 0.0, 1.0)

        def step(lower, upper, still_open):
            mid = 0.5 * lower + 0.5 * upper
            cnt = count_ge(float_key(mid), False)
            active = still_open > 0.0
            go_up = jnp.logical_and(active, cnt >= kf)
            go_down = jnp.logical_and(active, cnt < kf)
            lower = jnp.where(go_up, mid, lower)
            upper = jnp.where(go_down, mid, upper)
            still_open = jnp.where(jnp.logical_and(go_up, cnt == kf), 0.0, still_open)
            return lower, upper, still_open

        lower, upper, still_open = lax.fori_loop(
            0, DSA_VALUE_BLIND_STEPS, lambda _, c: step(*c), (lower, upper, open0))

        def cond(carry):
            it, _, _, still_open = carry
            return jnp.logical_and(it < DSA_VALUE_STEPS, jnp.max(still_open) > 0.0)

        def body(carry):
            it, lower, upper, still_open = carry
            return (it + 1,) + step(lower, upper, still_open)

        _, lower, _, still_open = lax.while_loop(
            cond, body, (jnp.int32(DSA_VALUE_BLIND_STEPS), lower, upper, still_open))
        return jnp.where(keep_all, INT_MIN, float_key(lower)), jnp.max(still_open) == 0.0

    def exact_select():
        def bit_step(it, carry):
            thr, cnt_thr = carry
            cand = thr + jnp.left_shift(jnp.int32(1), 31 - it)
            cnt = count_ge(cand, False)
            take = cnt >= kf
            return jnp.where(take, cand, thr), jnp.where(take, cnt, cnt_thr)

        thr0 = jnp.full((tq, LANES), INT_MIN, I32)
        cnt0 = jnp.full((tq, LANES), float(2 ** 24), F32)
        thr, cnt_thr = lax.fori_loop(0, 32, bit_step, (thr0, cnt0))
        t_ref[...] = thr

        excess = jnp.where(thr > INT_MIN, cnt_thr - kf, 0.0)

        @pl.when(jnp.max(excess) > 0.0)
        def _():
            need = kf - count_ge(thr, True)

            def tie_chunk(c, seen):
                keys = load_keys(c)
                eq = keys == _lane_tile(thr, DSA_SUB)
                eqf = jnp.where(eq, 1.0, 0.0)
                rank = _dot(eqf.astype(BF16), tri_ref[...]) + _lane_tile(seen, DSA_SUB)
                dropped = jnp.where(rank >= _lane_tile(need, DSA_SUB), INT_MIN, keys)
                keys = jnp.where(eq, dropped, keys)
                for u in range(DSA_SUB):
                    key_ref[c * DSA_SUB + u] = keys[:, u * LANES:(u + 1) * LANES]
                return seen + jnp.broadcast_to(jnp.sum(eqf, axis=1, keepdims=True), (tq, LANES))

            lax.fori_loop(0, nsel, tie_chunk, jnp.zeros((tq, LANES), F32))

    @pl.when(n_valid_cols > k_top)
    def _():
        thr_fast, finished = bisect_values()
        t_ref[...] = thr_fast

        @pl.when(jnp.logical_not(finished))
        def _():
            exact_select()

    thr = jnp.maximum(t_ref[...], INT_MIN + 1)

    m_ref[...] = jnp.full(m_ref.shape, NEG, F32)
    l_ref[...] = jnp.zeros(l_ref.shape, F32)
    acc_ref[...] = jnp.zeros(acc_ref.shape, F32)
    q_all = qm_ref[...].reshape(GROUP_HEADS * tq, LANES)

    def attend(k_blk, v_blk, madd, bias):
        width = k_blk.shape[0]
        s = _nt_dot(q_all, k_blk).reshape(GROUP_HEADS, tq, width) + madd[None]
        if bias is not None:
            s = s + bias
        _softmax_step_stacked(s, v_blk, m_ref, l_ref, acc_ref)

    nwin = DSA_NB + 1
    win_blk = jnp.maximum(i * DSA_NB - 1, 0)
    win_start = pl.multiple_of(win_blk * LANES, LANES)
    near_keys = key_ref[pl.ds(win_blk, nwin)]
    near_keys = jnp.concatenate([near_keys[u] for u in range(nwin)], axis=1)
    near_madd = jnp.where(near_keys >= _lane_tile(thr, nwin), 0.0, NEG)
    attend(k2_ref[pl.ds(win_start, nwin * LANES), :], v2_ref[pl.ds(win_start, nwin * LANES), :],
           near_madd, bias_ref[jnp.minimum(i, 1)])
    for u in range(nwin):
        key_ref[win_blk + u] = jnp.full((tq, LANES), INT_MIN, I32)

    n_far = (win_blk * LANES + DSA_KC - 1) // DSA_KC

    def far_chunk(c, carry):
        start = pl.multiple_of(c * DSA_KC, DSA_KC)
        madd = jnp.where(load_keys(c) >= _lane_tile(thr, DSA_SUB), 0.0, NEG)
        attend(k2_ref[pl.ds(start, DSA_KC), :], v2_ref[pl.ds(start, DSA_KC), :], madd, None)
        return carry

    lax.fori_loop(0, n_far, far_chunk, 0)
    _store_pairs(o_ref, [acc_ref[h] / l_ref[h] for h in range(GROUP_HEADS)])


def _mixer_dsa(p, side, bias, tri, batch, seq):
    t = p.shape[0]
    tq = DSA_TQ
    nq = seq // tq
    k_top = min(TOPK_MAX, seq // 4)
    cw = PT // LANES
    return pl.pallas_call(
        functools.partial(_dsa_kernel, k_top=k_top),
        grid=(batch, nq),
        in_specs=[
            pl.BlockSpec((tq, PT), lambda b, i: (b * nq + i, T_AQ)),
            pl.BlockSpec((tq, 2 * PT), lambda b, i: (b * nq + i, T_IQ // 2)),
            pl.BlockSpec((tq, LANES), lambda b, i: (b * nq + i, 0)),
            pl.BlockSpec((seq, LANES), lambda b, i: (b, T_K2 * cw), pipeline_mode=pl.Buffered(1)),
            pl.BlockSpec((seq, LANES), lambda b, i: (b, T_MISC * cw), pipeline_mode=pl.Buffered(1)),
            pl.BlockSpec((seq, LANES), lambda b, i: (b, T_MISC * cw + 2), pipeline_mode=pl.Buffered(1)),
            pl.BlockSpec((2, GROUP_HEADS, tq, tq + LANES), lambda b, i: (0, 0, 0, 0),
                         pipeline_mode=pl.Buffered(1)),
            pl.BlockSpec((DSA_KC, DSA_KC), lambda b, i: (0, 0), pipeline_mode=pl.Buffered(1)),
        ],
        out_specs=pl.BlockSpec((tq, GROUP_WIDTH), lambda b, i: (b * nq + i, 0)),
        out_shape=jax.ShapeDtypeStruct((t, GROUP_WIDTH), MIX_DTYPE),
        scratch_shapes=[
            pltpu.VMEM((max(seq // LANES, DSA_SUB, DSA_NB + 1), tq, LANES), I32),
            pltpu.VMEM((IDX_HEADS, tq, LANES), F32),
            pltpu.VMEM((IDX_HEADS, tq, LANES), BF16),
            pltpu.VMEM((GROUP_HEADS, tq, LANES), BF16),
            pltpu.VMEM((tq, LANES), I32),
            pltpu.VMEM((tq, LANES), I32),
            pltpu.VMEM((GROUP_HEADS, tq, LANES), F32),
            pltpu.VMEM((GROUP_HEADS, tq, LANES), F32),
            pltpu.VMEM((GROUP_HEADS, tq, LANES), F32),
        ],
        compiler_params=_cparams(("arbitrary", "arbitrary")),
        name="mixer_dsa",
    )(p, p, side, p, p, p, bias, tri)


SWA_TQ = 128


def _swa_kernel(sink_ref, q_ref, kp_ref, kc_ref, vp_ref, vc_ref, bias_ref, o_ref):
    i = pl.program_id(1)
    tq = SWA_TQ
    nh = GROUP_HEADS
    t_l = lax.broadcasted_iota(I32, (tq, 2 * LANES), 0)
    w = lax.broadcasted_iota(I32, (tq, 2 * LANES), 1)
    first = jnp.where(i > 0, t_l + 1, LANES)
    madd = jnp.where(w >= first, jnp.where(w <= t_l + LANES, 0.0, NEG), NEG)
    q_heads = []
    sink_rows = []
    for h in range(nh):
        blk = q_ref[:, (h // 2) * LANES:(h // 2 + 1) * LANES]
        q_heads.append(jnp.where(_own_lane_mask(blk.shape, h), blk, jnp.zeros_like(blk)))
        sink_rows.append(jnp.full((1, tq, LANES), sink_ref[h] * LOG2E, F32))
    sinks = jnp.concatenate(sink_rows, axis=0)
    k_win = jnp.concatenate([kp_ref[...], kc_ref[...]], axis=0)
    v_win = jnp.concatenate([vp_ref[...], vc_ref[...]], axis=0)
    s = _nt_dot(jnp.concatenate(q_heads, axis=0), k_win).reshape(nh, tq, 2 * LANES) + bias_ref[...] + madd[None]
    m = jnp.maximum(jnp.max(s, axis=2, keepdims=True), sinks)
    p = jnp.exp2(s - jnp.concatenate([m, m], axis=2))
    l = jnp.sum(p, axis=2, keepdims=True) + jnp.exp2(sinks - m)
    out = _dot(p.astype(BF16).reshape(nh * tq, 2 * LANES), v_win).reshape(nh, tq, LANES) / l
    _store_pairs(o_ref, [out[h] for h in range(nh)])


def _mixer_swa(p, sinks, bias, batch, seq):
    t = p.shape[0]
    tq = SWA_TQ
    nq = seq // tq
    cw = PT // LANES
    cur = lambda col: (lambda b, i: (b * nq + i, col))
    prev = lambda col: (lambda b, i: (b * nq + jnp.maximum(i - 1, 0), col))
    return pl.pallas_call(
        _swa_kernel,
        grid=(batch, nq),
        in_specs=[
            pl.BlockSpec(memory_space=pltpu.SMEM),
            pl.BlockSpec((tq, PT), lambda b, i: (b * nq + i, T_BQ)),
            pl.BlockSpec((tq, LANES), prev(T_K2 * cw + 1)),
            pl.BlockSpec((tq, LANES), cur(T_K2 * cw + 1)),
            pl.BlockSpec((tq, LANES), prev(T_MISC * cw + 1)),
            pl.BlockSpec((tq, LANES), cur(T_MISC * cw + 1)),
            pl.BlockSpec((GROUP_HEADS, tq, 2 * LANES), lambda b, i: (0, 0, 0)),
        ],
        out_specs=pl.BlockSpec((tq, GROUP_WIDTH), lambda b, i: (b * nq + i, 0)),
        out_shape=jax.ShapeDtypeStruct((t, GROUP_WIDTH), MIX_DTYPE),
        compiler_params=_cparams(("arbitrary", "arbitrary")),
        name="mixer_swa",
    )(sinks, p, p, p, p, p, bias)


def _logsig(x):
    return jnp.minimum(x, 0.0) - jnp.log(1.0 + jnp.exp(-jnp.abs(x)))


def _foxcum_kernel(fb_ref, f_ref, tri_ref, o_ref, *, nchunk):
    def body(c, carry):
        start = pl.multiple_of(c * LANES, LANES)
        lf = _logsig(f_ref[pl.ds(start, LANES), :] + fb_ref[...]).T
        p1 = lf.astype(BF16)
        r1 = lf - p1.astype(F32)
        p2 = r1.astype(BF16)
        p3 = (r1 - p2.astype(F32)).astype(BF16)
        tri = tri_ref[...]
        cum = _dot(p1, tri) + _dot(p2, tri) + _dot(p3, tri) + carry
        o_ref[0, c] = cum[:GROUP_HEADS]
        return jnp.broadcast_to(cum[:, LANES - 1:LANES], cum.shape)

    lax.fori_loop(0, nchunk, body, jnp.zeros((LANES, LANES), F32))


def _fox_cum(side, fb_row, tri_incl, batch, seq):
    nchunk = seq // LANES
    return pl.pallas_call(
        functools.partial(_foxcum_kernel, nchunk=nchunk),
        grid=(batch,),
        in_specs=[
            pl.BlockSpec((1, LANES), lambda b: (0, 0)),
            pl.BlockSpec((seq, LANES), lambda b: (b, 0)),
            pl.BlockSpec((LANES, LANES), lambda b: (0, 0)),
        ],
        out_specs=pl.BlockSpec((1, nchunk, GROUP_HEADS, LANES), lambda b: (b, 0, 0, 0)),
        out_shape=jax.ShapeDtypeStruct((batch, nchunk, GROUP_HEADS, LANES), F32),
        compiler_params=_cparams(("arbitrary",)),
        name="fox_cumsum",
    )(fb_row, side, tri_incl)


FOX_TQ = 256
FOX_KC = 512
FOX_SUB = FOX_KC // LANES


def _fox_kernel(q_ref, g_ref, k_ref, v_ref, cum_ref, o_ref, qm_ref, m_ref, l_ref, acc_ref):
    i = pl.program_id(1)
    tq = FOX_TQ
    for h in range(GROUP_HEADS):
        blk = q_ref[:, (h // 2) * LANES:(h // 2 + 1) * LANES]
        qm_ref[h] = jnp.where(_own_lane_mask(blk.shape, h), blk, jnp.zeros_like(blk))
    m_ref[...] = jnp.full(m_ref.shape, NEG, F32)
    l_ref[...] = jnp.zeros(l_ref.shape, F32)
    acc_ref[...] = jnp.zeros(acc_ref.shape, F32)
    f_ref0 = cum_ref[0, i * (tq // LANES)][:, 0:1]

    def chunk(c, masked, width=FOX_KC):
        start = pl.multiple_of(c * FOX_KC, FOX_KC)
        cum = cum_ref[0, pl.ds(c * FOX_SUB, width // LANES)]
        cum = jnp.concatenate([cum[u] for u in range(width // LANES)], axis=1)
        fbias = (f_ref0 - cum) * LOG2E
        per_head = []
        for pr in range(PAIRS):
            pair = slice(pr * LANES, (pr + 1) * LANES)
            q_pair = qm_ref[2 * pr:2 * pr + 2].reshape(2 * tq, LANES)
            s_pair = _nt_dot(q_pair, k_ref[pl.ds(start, width), pair])
            for e in range(2):
                h = 2 * pr + e
                per_head.append(s_pair[e * tq:(e + 1) * tq] + fbias[h:h + 1, :])
        s = jnp.stack(per_head)
        if masked:
            shape = (GROUP_HEADS, tq, width)
            ok = start + lax.broadcasted_iota(I32, shape, 2) <= i * tq + lax.broadcasted_iota(I32, shape, 1)
            s = jnp.where(ok, s, NEG)
        reps = width // LANES
        m_prev = m_ref[...]
        m_new = jnp.maximum(m_prev, jnp.max(s, axis=2, keepdims=True))
        p = jnp.exp2(s - jnp.concatenate([m_new] * reps, axis=2))
        alpha = jnp.exp2(m_prev - m_new)
        l_ref[...] = alpha * l_ref[...] + jnp.sum(p, axis=2, keepdims=True)
        m_ref[...] = m_new
        p = p.astype(BF16)
        pvs = []
        for pr in range(PAIRS):
            p_pair = p[2 * pr:2 * pr + 2].reshape(2 * tq, width)
            pvs.append(_dot(p_pair, v_ref[pl.ds(start, width), pr * LANES:(pr + 1) * LANES]))
        acc_ref[...] = alpha * acc_ref[...] + jnp.concatenate(pvs, axis=0).reshape(GROUP_HEADS, tq, LANES)

    def body(c, carry):
        chunk(c, False)
        return carry

    n_full = (i * tq) // FOX_KC
    lax.fori_loop(0, n_full, body, 0)
    visible = (i + 1) * tq - n_full * FOX_KC
    for width in range(tq, FOX_KC + 1, tq):
        pl.when(visible == width)(functools.partial(chunk, n_full, True, width))
    outs = []
    for h in range(GROUP_HEADS):
        gate = _sigmoid(g_ref[:, (h // 2) * LANES:(h // 2 + 1) * LANES].astype(F32))
        outs.append(acc_ref[h] / l_ref[h] * gate)
    _store_pairs(o_ref, outs)


def _mixer_fox(p, cum, batch, seq):
    t = p.shape[0]
    tq = FOX_TQ
    nq = seq // tq
    return pl.pallas_call(
        _fox_kernel,
        grid=(batch, nq),
        in_specs=[
            pl.BlockSpec((tq, PT), lambda b, i: (b * nq + i, T_CQ)),
            pl.BlockSpec((tq, PT), lambda b, i: (b * nq + i, T_CG)),
            pl.BlockSpec((seq, PT), lambda b, i: (b, T_CK)),
            pl.BlockSpec((seq, PT), lambda b, i: (b, T_CV)),
            pl.BlockSpec((1, seq // LANES, GROUP_HEADS, LANES), lambda b, i: (b, 0, 0, 0)),
        ],
        out_specs=pl.BlockSpec((tq, GROUP_WIDTH), lambda b, i: (b * nq + i, 0)),
        out_shape=jax.ShapeDtypeStruct((t, GROUP_WIDTH), MIX_DTYPE),
        scratch_shapes=[
            pltpu.VMEM((GROUP_HEADS, tq, LANES), BF16),
            pltpu.VMEM((GROUP_HEADS, tq, LANES), F32),
            pltpu.VMEM((GROUP_HEADS, tq, LANES), F32),
            pltpu.VMEM((GROUP_HEADS, tq, LANES), F32),
        ],
        compiler_params=_cparams(("arbitrary", "arbitrary")),
        name="mixer_fox",
    )(p, p, p, p, cum)


STK_TQ = 128


def _stick_kernel(q_ref, k_ref, v_ref, tri_ref, o_ref, qm_ref, acc_ref, r_ref):
    i = pl.program_id(1)
    tq = STK_TQ
    nh = GROUP_HEADS
    for h in range(nh):
        blk = q_ref[:, (h // 2) * LANES:(h // 2 + 1) * LANES]
        qm_ref[h] = jnp.where(_own_lane_mask(blk.shape, h), blk, jnp.zeros_like(blk))
    acc_ref[...] = jnp.zeros(acc_ref.shape, F32)
    r_ref[...] = jnp.zeros(r_ref.shape, F32)
    before = (lax.broadcasted_iota(I32, (nh, tq, LANES), 2) < lax.broadcasted_iota(I32, (nh, tq, LANES), 1))
    tri = tri_ref[...]

    def block(j, masked):
        start = pl.multiple_of(j * LANES, LANES)
        zs = []
        for pr in range(PAIRS):
            q_pair = qm_ref[2 * pr:2 * pr + 2].reshape(2 * tq, LANES)
            zs.append(_nt_dot(q_pair, k_ref[pl.ds(start, LANES), pr * LANES:(pr + 1) * LANES]))
        z = jnp.concatenate(zs, axis=0).reshape(nh, tq, LANES)
        lsz = _logsig(z)
        u = lsz - z
        if masked:
            u = jnp.where(before, u, 0.0)
        u2 = u.reshape(nh * tq, LANES)
        u_hi = u2.astype(BF16)
        u_lo = (u2 - u_hi.astype(F32)).astype(BF16)
        nearer = (_dot(u_hi, tri) + _dot(u_lo, tri)).reshape(nh, tq, LANES)
        run = r_ref[...]
        w = jnp.exp(lsz + nearer + run)
        if masked:
            w = jnp.where(before, w, 0.0)
        wb = w.astype(BF16)
        pvs = []
        for pr in range(PAIRS):
            w_pair = wb[2 * pr:2 * pr + 2].reshape(2 * tq, LANES)
            pvs.append(_dot(w_pair, v_ref[pl.ds(start, LANES), pr * LANES:(pr + 1) * LANES]))
        acc_ref[...] += jnp.concatenate(pvs, axis=0).reshape(nh, tq, LANES)
        run = run + jnp.sum(u, axis=2, keepdims=True)
        r_ref[...] = run
        return jnp.max(run)

    rmax = block(i, True)

    def cond(carry):
        j, rmax = carry
        return jnp.logical_and(j >= 0, rmax >= STICK_EXIT)

    def body(carry):
        j, _ = carry
        return j - 1, block(j, False)

    lax.while_loop(cond, body, (i - 1, rmax))
    _store_pairs(o_ref, [acc_ref[h] for h in range(nh)])


def _mixer_stick(p, tri_excl, batch, seq):
    t = p.shape[0]
    tq = STK_TQ
    nq = seq // tq
    return pl.pallas_call(
        _stick_kernel,
        grid=(batch, nq),
        in_specs=[
            pl.BlockSpec((tq, PT), lambda b, i: (b * nq + i, T_DQ)),
            pl.BlockSpec((seq, PT), lambda b, i: (b, T_DK)),
            pl.BlockSpec((seq, PT), lambda b, i: (b, T_DV)),
            pl.BlockSpec((LANES, LANES), lambda b, i: (0, 0)),
        ],
        out_specs=pl.BlockSpec((tq, GROUP_WIDTH), lambda b, i: (b * nq + i, 0)),
        out_shape=jax.ShapeDtypeStruct((t, GROUP_WIDTH), MIX_DTYPE),
        scratch_shapes=[pltpu.VMEM((GROUP_HEADS, tq, LANES), BF16),
                        pltpu.VMEM((GROUP_HEADS, tq, LANES), F32),
                        pltpu.VMEM((GROUP_HEADS, tq, LANES), F32)],
        compiler_params=_cparams(("arbitrary", "arbitrary")),
        name="mixer_stick",
    )(p, p, p, tri_excl)


def _outproj_kernel(x_ref, gt_ref, oa_ref, ob_ref, oc_ref, od_ref, gg_ref, w_ref, o_ref):
    acc = None
    for m, ref in enumerate((oa_ref, ob_ref, oc_ref, od_ref)):
        o = ref[...].astype(F32)
        y = o * lax.rsqrt(jnp.mean(o * o, axis=-1, keepdims=True) + EPS) * gg_ref[m:m + 1, :]
        part = _dot(y.astype(BF16), w_ref[m * GROUP_WIDTH:(m + 1) * GROUP_WIDTH, :])
        acc = part if acc is None else acc + part
    o_ref[...] = x_ref[...] + gt_ref[0] * acc


def _outproj(x, gt, outs, gg, w, l, seq):
    t, d = x.shape
    tm = 512
    per_b = seq // tm
    mix = pl.BlockSpec((tm, GROUP_WIDTH), lambda i: (i, 0))
    return pl.pallas_call(
        _outproj_kernel,
        grid=(t // tm,),
        in_specs=[
            pl.BlockSpec((tm, d), lambda i: (i, 0)),
            pl.BlockSpec((1, 1, d), lambda i: (i // per_b, 0, 0)),
            mix, mix, mix, mix,
            pl.BlockSpec((None, N_MIXERS, GROUP_WIDTH), lambda i: (l, 0, 0)),
            pl.BlockSpec((None, N_MIXERS * GROUP_WIDTH, d), lambda i: (l, 0, 0)),
        ],
        out_specs=pl.BlockSpec((tm, d), lambda i: (i, 0)),
        out_shape=jax.ShapeDtypeStruct((t, d), F32),
        compiler_params=_cparams(("arbitrary",)),
        name="outproj",
    )(x, gt, *outs, gg, w)


def _prep_in_weights(w_in, qk_g):
    depth, d, _ = w_in.shape
    sizes = (GROUP_WIDTH, HEAD_DIM, HEAD_DIM, IDX_HEADS * IDX_DIM, IDX_DIM, IDX_HEADS,
             GROUP_WIDTH, HEAD_DIM, HEAD_DIM,
             GROUP_WIDTH, GROUP_WIDTH, GROUP_WIDTH, GROUP_HEADS, GROUP_WIDTH,
             GROUP_WIDTH, GROUP_WIDTH, GROUP_WIDTH)
    pts = np.cumsum(sizes)[:-1].tolist()
    (a_q, a_k, a_v, a_iq, a_ik, a_iw, b_q, b_k, b_v,
     c_q, c_k, c_v, c_f, c_g, d_q, d_k, d_v) = jnp.split(w_in, pts, axis=-1)
    z = lambda n: jnp.zeros((depth, d, n), w_in.dtype)
    qscale = HEAD_DIM ** -0.5
    main = jnp.concatenate([
        a_q, b_q, c_q, c_k,
        a_k, a_k, b_k, b_k, z(2 * LANES),
        c_v, c_g, d_q * qscale, d_k, d_v, a_iq * (IDX_DIM ** -0.5),
        a_v, a_v, b_v, b_v, a_ik, a_ik, z(LANES),
    ], axis=-1).astype(BF16)
    side = jnp.concatenate([c_f, a_iw * (IDX_HEADS ** -0.5), z(LANES - GROUP_HEADS - IDX_HEADS)],
                           axis=-1).astype(BF16)
    rep = lambda g, n: jnp.tile(g, (1, n))
    sscale = qscale * LOG2E
    gain = jnp.concatenate([
        rep(qk_g[:, 0], 8) * sscale, rep(qk_g[:, 2], 8) * sscale, rep(qk_g[:, 4], 8) * sscale,
        rep(qk_g[:, 5], 8), rep(qk_g[:, 1], 2), rep(qk_g[:, 3], 2),
        jnp.zeros((depth, P_COLS - 4 * PT - 2 * LANES), F32),
    ], axis=-1).reshape(depth, 1, P_COLS)
    return main, side, gain


def kernel(x, c, w_ada, b_ada, norm_g, w_in, qk_g, forget_b, sinks, rel_table, group_g, w_out,
           w_ffn_gate, w_ffn_up, w_ffn_down):
    batch, seq, d = x.shape
    depth = w_ada.shape[0]
    t = batch * seq

    mod = _modulation(c, w_ada, b_ada)
    wg = w_ffn_gate.astype(BF16)
    wu = w_ffn_up.astype(BF16)
    wd = w_ffn_down.astype(BF16)
    wo = w_out.astype(BF16)
    w_main, w_side, gain = _prep_in_weights(w_in.astype(BF16), qk_g)

    r = np.arange(PT)
    bd = jnp.asarray((r[:, None] // HEAD_DIM == r[None, :] // HEAD_DIM).astype(np.float32) / HEAD_DIM, BF16)
    r = np.arange(LANES)
    tri_incl = jnp.asarray(r[:, None] <= r[None, :], BF16)
    tri_after = jnp.asarray(r[:, None] > r[None, :], BF16)
    r = np.arange(DSA_KC)
    tri_before = jnp.asarray(r[:, None] < r[None, :], BF16)

    bias_dsa = jnp.swapaxes(_bias_windows(rel_table, DSA_TQ)[:GROUP_HEADS], 0, 1)
    bias_swa = _bias_windows(rel_table, SWA_TQ)[GROUP_HEADS:, 1]
    fb_rows = jnp.zeros((depth, 1, LANES), F32).at[:, 0, :GROUP_HEADS].set(forget_b)
    gg = group_g.reshape(depth, N_MIXERS, GROUP_WIDTH)

    xt = x.reshape(t, d)
    for l in range(depth):
        parts = [m.reshape(batch, 1, d) for m in jnp.split(mod[l], 9, axis=-1)]
        sh1, sc1, g1, sh2, sc2, g2, sh3, sc3, g3 = parts
        xt = _ffn(xt, norm_g[l, 0:1], sh1, sc1, g1, wg, wu, wd, l, 0, seq)
        ng2 = norm_g[l, 1:2]
        p, side = _inproj(xt, ng2, sh2, sc2, w_main, gain, bd, w_side, l, seq, N_NORM_TILES)
        o_a = _mixer_dsa(p, side, bias_dsa, tri_before, batch, seq)
        o_b = _mixer_swa(p, sinks[l], bias_swa, batch, seq)
        cum = _fox_cum(side, fb_rows[l], tri_incl, batch, seq)
        o_c = _mixer_fox(p, cum, batch, seq)
        o_d = _mixer_stick(p, tri_after, batch, seq)
        xt = _outproj(xt, g2, (o_a, o_b, o_c, o_d), gg, wo, l, seq)
        xt = _ffn(xt, norm_g[l, 2:3], sh3, sc3, g3, wg, wu, wd, l, 1, seq)
    return xt.reshape(batch, seq, d)
```

```python
import functools
import math

import numpy as np
import jax
import jax.numpy as jnp
from jax import lax
from jax.experimental import pallas as pl
from jax.experimental.pallas import tpu as pltpu

F32 = jnp.float32
BF16 = jnp.bfloat16
I32 = jnp.int32

HEAD_DIM = 64
N_MIXERS = 4
GROUP_HEADS = 8
GROUP_WIDTH = GROUP_HEADS * HEAD_DIM
IDX_HEADS = 16
IDX_DIM = 64
TOPK_MAX = 256
WINDOW = 128
REL_BUCKETS = 32
REL_MAX_DIST = 128
EPS = 1e-6

LANES = 128
PAIRS = GROUP_HEADS // 2
NEG = -1e30
LOG2E = math.log2(math.e)
INT_MIN = -2 ** 31
STICK_EXIT = -110.0
VMEM_LIMIT = 56 * 1024 * 1024
MIX_DTYPE = BF16

PT = 512
T_AQ, T_BQ, T_CQ, T_CK, T_K2 = 0, 1, 2, 3, 4
N_NORM_TILES = 5
T_CV, T_CG, T_DQ, T_DK, T_DV, T_IQ, T_MISC = 5, 6, 7, 8, 9, 10, 12
N_TILES = 13
P_COLS = N_TILES * PT


def _cparams(sem, vmem=VMEM_LIMIT):
    return pltpu.CompilerParams(dimension_semantics=sem, vmem_limit_bytes=vmem)


def _nt_dot(a, b):
    return lax.dot_general(a, b, (((1,), (1,)), ((), ())), preferred_element_type=F32)


def _dot(a, b):
    return jnp.dot(a, b, preferred_element_type=F32)


def _sigmoid(x):
    return 1.0 / (1.0 + jnp.exp(-x))


def _lane_tile(x, n):
    return x if n == 1 else jnp.concatenate([x] * n, axis=1)


def _own_lane_mask(shape, h):
    lane = lax.broadcasted_iota(I32, shape, 1)
    return (lane < HEAD_DIM) if h % 2 == 0 else (lane >= HEAD_DIM)


def _mod_kernel(c_ref, w_ref, b_ref, o_ref):
    c = c_ref[...]
    cond = (c * _sigmoid(c)).astype(BF16)
    o_ref[0] = _dot(cond, w_ref[0].astype(BF16)) + b_ref[0]


def _modulation(c, w_ada, b_ada):
    depth, d, n = w_ada.shape
    b = c.shape[0]
    rows = 8
    tn = 1024
    c_pad = jnp.zeros((rows, d), F32).at[:b].set(c)
    out = pl.pallas_call(
        _mod_kernel,
        grid=(depth, n // tn),
        in_specs=[
            pl.BlockSpec((rows, d), lambda l, j: (0, 0)),
            pl.BlockSpec((1, d, tn), lambda l, j: (l, 0, j)),
            pl.BlockSpec((1, 1, tn), lambda l, j: (l, 0, j)),
        ],
        out_specs=pl.BlockSpec((1, rows, tn), lambda l, j: (l, 0, j)),
        out_shape=jax.ShapeDtypeStruct((depth, rows, n), F32),
        compiler_params=_cparams(("arbitrary", "arbitrary")),
        name="adaln_mod",
    )(c_pad, w_ada, b_ada.reshape(depth, 1, n))
    return out[:, :b]


def _norm_modulate(x, ng, sh, sc):
    ms = jnp.mean(x * x, axis=-1, keepdims=True)
    y = x * lax.rsqrt(ms + EPS) * ng
    return y * (1.0 + sc) + sh


def _ffn_kernel(x_ref, ng_ref, sh_ref, sc_ref, gt_ref, wg_ref, wu_ref, wd_ref, o_ref, h_ref, a_ref):
    f = pl.program_id(1)
    last = pl.num_programs(1) - 1

    def activation():
        h = h_ref[...]
        g = _dot(h, wg_ref[...])
        u = _dot(h, wu_ref[...])
        return (g * _sigmoid(g) * u).astype(BF16)

    @pl.when(f == 0)
    def _():
        h = _norm_modulate(x_ref[...], ng_ref[...], sh_ref[0], sc_ref[0])
        h_ref[...] = h.astype(BF16)
        o_ref[...] = jnp.zeros_like(o_ref)
        a_ref[...] = activation()

    @pl.when(jnp.logical_and(f > 0, f < last))
    def _():
        o_ref[...] += _dot(a_ref[...], wd_ref[...])
        a_ref[...] = activation()

    @pl.when(f == last)
    def _():
        acc = o_ref[...] + _dot(a_ref[...], wd_ref[...])
        o_ref[...] = x_ref[...] + 0.5 * gt_ref[0] * acc


def _ffn(x, ng, sh, sc, gt, wg, wu, wd, l, s, seq):
    t, d = x.shape
    tm, tf = min(1024, seq), 512
    nf = wg.shape[-1] // tf
    per_b = seq // tm
    vec = pl.BlockSpec((1, 1, d), lambda i, f: (i // per_b, 0, 0))
    up = lambda i, f: (l, s, 0, jnp.minimum(f, nf - 1))
    return pl.pallas_call(
        _ffn_kernel,
        grid=(t // tm, nf + 1),
        in_specs=[
            pl.BlockSpec((tm, d), lambda i, f: (i, 0), pipeline_mode=pl.Buffered(1)),
            pl.BlockSpec((1, d), lambda i, f: (0, 0)),
            vec, vec, vec,
            pl.BlockSpec((None, None, d, tf), up),
            pl.BlockSpec((None, None, d, tf), up),
            pl.BlockSpec((None, None, tf, d), lambda i, f: (l, s, jnp.maximum(f - 1, 0), 0)),
        ],
        out_specs=pl.BlockSpec((tm, d), lambda i, f: (i, 0)),
        out_shape=jax.ShapeDtypeStruct((t, d), F32),
        scratch_shapes=[pltpu.VMEM((tm, d), BF16), pltpu.VMEM((tm, tf), BF16)],
        compiler_params=_cparams(("arbitrary", "arbitrary")),
        name="ffn",
    )(x, ng, sh, sc, gt, wg, wu, wd)


def _inproj_kernel(x_ref, ng_ref, sh_ref, sc_ref, w_ref, gain_ref, bd_ref, ws_ref, o_ref, side_ref, h_ref,
                   *, n_norm):
    j = pl.program_id(1)

    @pl.when(j == 0)
    def _():
        h = _norm_modulate(x_ref[...], ng_ref[...], sh_ref[0], sc_ref[0]).astype(BF16)
        h_ref[...] = h
        side_ref[...] = _dot(h, ws_ref[...])

    y = _dot(h_ref[...], w_ref[...])

    @pl.when(j < n_norm)
    def _():
        ms = _dot((y * y).astype(BF16), bd_ref[...])
        o_ref[...] = (y * lax.rsqrt(ms + EPS) * gain_ref[...]).astype(o_ref.dtype)

    @pl.when(j >= n_norm)
    def _():
        o_ref[...] = y.astype(o_ref.dtype)


def _inproj(x, ng, sh, sc, w, gain, bd, w_side, l, seq, n_norm):
    t, d = x.shape
    n = w.shape[-1]
    tm = min(1024, seq)
    tn = bd.shape[0]
    per_b = seq // tm
    vec = pl.BlockSpec((1, 1, d), lambda i, j: (i // per_b, 0, 0))
    return pl.pallas_call(
        functools.partial(_inproj_kernel, n_norm=n_norm),
        grid=(t // tm, n // tn),
        in_specs=[
            pl.BlockSpec((tm, d), lambda i, j: (i, 0)),
            pl.BlockSpec((1, d), lambda i, j: (0, 0)),
            vec, vec,
            pl.BlockSpec((None, d, tn), lambda i, j: (l, 0, j)),
            pl.BlockSpec((None, 1, tn), lambda i, j: (l, 0, j)),
            pl.BlockSpec((tn, tn), lambda i, j: (0, 0)),
            pl.BlockSpec((None, d, LANES), lambda i, j: (l, 0, 0)),
        ],
        out_specs=[pl.BlockSpec((tm, tn), lambda i, j: (i, j)),
                   pl.BlockSpec((tm, LANES), lambda i, j: (i, 0))],
        out_shape=[jax.ShapeDtypeStruct((t, n), BF16), jax.ShapeDtypeStruct((t, LANES), F32)],
        scratch_shapes=[pltpu.VMEM((tm, d), BF16)],
        compiler_params=_cparams(("arbitrary", "arbitrary")),
        name="inproj",
    )(x, ng, sh, sc, w, gain, bd, w_side)


def _rel_bucket_np(dist):
    n = np.maximum(dist, 0)
    max_exact = REL_BUCKETS // 2
    nf = np.maximum(n, 1).astype(np.float32)
    large = max_exact + (np.log(nf / np.float32(max_exact)) / np.float32(math.log(REL_MAX_DIST / max_exact))
                         * np.float32(REL_BUCKETS - max_exact)).astype(np.int32)
    large = np.minimum(large, REL_BUCKETS - 1)
    return np.where(n < max_exact, n, large).astype(np.int32)


def _bias_kernel(tab_ref, bkt_ref, o_ref, *, sub_far):
    h = pl.program_id(0)
    far = tab_ref[REL_BUCKETS - 1, h]
    for which in range(2):
        bkt = bkt_ref[which]
        acc = jnp.zeros(bkt.shape, F32)
        for b in range(REL_BUCKETS):
            acc = jnp.where(bkt == b, tab_ref[b, h], acc)
        o_ref[0, which] = (acc - jnp.where(h < sub_far, far, 0.0)) * LOG2E


def _bias_windows(rel_table, tq):
    width = tq + LANES
    t_l = np.arange(tq)[:, None]
    w = np.arange(width)[None, :]
    bkt = np.stack([_rel_bucket_np(t_l - w), _rel_bucket_np(t_l + LANES - w)]).astype(np.int32)
    nh = rel_table.shape[1]
    return pl.pallas_call(
        functools.partial(_bias_kernel, sub_far=GROUP_HEADS),
        grid=(nh,),
        in_specs=[
            pl.BlockSpec(memory_space=pltpu.SMEM),
            pl.BlockSpec((2, tq, width), lambda h: (0, 0, 0)),
        ],
        out_specs=pl.BlockSpec((1, 2, tq, width), lambda h: (h, 0, 0, 0)),
        out_shape=jax.ShapeDtypeStruct((nh, 2, tq, width), F32),
        compiler_params=_cparams(("arbitrary",)),
        name="rel_bias_windows",
    )(rel_table, jnp.asarray(bkt))


def _softmax_step_stacked(s, v, m_ref, l_ref, acc_ref):
    nh, rows, width = s.shape
    reps = width // LANES
    m_prev = m_ref[...]
    m_new = jnp.maximum(m_prev, jnp.max(s, axis=2, keepdims=True))
    p = jnp.exp2(s - jnp.concatenate([m_new] * reps, axis=2))
    alpha = jnp.exp2(m_prev - m_new)
    l_ref[...] = alpha * l_ref[...] + jnp.sum(p, axis=2, keepdims=True)
    m_ref[...] = m_new
    pv = _dot(p.astype(BF16).reshape(nh * rows, width), v).reshape(nh, rows, LANES)
    acc_ref[...] = alpha * acc_ref[...] + pv


def _store_pairs(o_ref, per_head):
    for j in range(PAIRS):
        lane = lax.broadcasted_iota(I32, per_head[0].shape, 1)
        o_ref[:, j * LANES:(j + 1) * LANES] = jnp.where(
            lane < HEAD_DIM, per_head[2 * j], per_head[2 * j + 1]).astype(o_ref.dtype)


DSA_TQ = 256
DSA_NB = DSA_TQ // LANES
DSA_KC1 = 512
DSA_KC = 512
DSA_SUB = DSA_KC // LANES
DSA_VALUE_STEPS = 26
DSA_VALUE_BLIND_STEPS = 10


def _dsa_kernel(q_ref, iq_ref, iw_ref, k2_ref, v2_ref, ik2_ref, bias_ref, tri_ref, o_ref,
                key_ref, wb_ref, iqm_ref, qm_ref, t_ref, cand_ref, m_ref, l_ref, acc_ref, *, k_top):
    i = pl.program_id(1)
    tq = DSA_TQ
    row0 = i * tq
    n_valid_cols = row0 + tq

    iw = iw_ref[...]
    for h in range(IDX_HEADS):
        col = GROUP_HEADS + h
        wb_ref[h] = jnp.broadcast_to(iw[:, col:col + 1], (tq, LANES))
        blk = iq_ref[:, (h // 2) * LANES:(h // 2 + 1) * LANES]
        iqm_ref[h] = jnp.where(_own_lane_mask(blk.shape, h), blk, jnp.zeros_like(blk))
    for h in range(GROUP_HEADS):
        blk = q_ref[:, (h // 2) * LANES:(h // 2 + 1) * LANES]
        qm_ref[h] = jnp.where(_own_lane_mask(blk.shape, h), blk, jnp.zeros_like(blk))

    qpos = row0 + lax.broadcasted_iota(I32, (tq, DSA_KC1), 0)
    iq_all = iqm_ref[...].reshape(IDX_HEADS * tq, LANES)
    wb_all = jnp.concatenate([wb_ref[...]] * (DSA_KC1 // LANES), axis=2)
    n1 = (n_valid_cols + DSA_KC1 - 1) // DSA_KC1

    def score_chunk(c, carry):
        start = pl.multiple_of(c * DSA_KC1, DSA_KC1)
        ik = ik2_ref[pl.ds(start, DSA_KC1), :]
        r = _nt_dot(iq_all, ik).reshape(IDX_HEADS, tq, DSA_KC1)
        score = jnp.sum(jnp.maximum(r, 0.0) * wb_all, axis=0)
        bits = lax.bitcast_convert_type(score, I32)
        key = bits ^ ((bits >> 31) & 0x7FFFFFFF)
        key = jnp.where(score == 0.0, 0, key)
        kpos = start + lax.broadcasted_iota(I32, (tq, DSA_KC1), 1)
        valid = kpos <= qpos
        key = jnp.where(valid, key, INT_MIN)
        smax, smin = carry
        hi = jnp.where(valid, score, -jnp.inf)
        lo = jnp.where(valid, score, jnp.inf)
        for u in range(DSA_KC1 // LANES):
            key_ref[c * (DSA_KC1 // LANES) + u] = key[:, u * LANES:(u + 1) * LANES]
            smax = jnp.maximum(smax, hi[:, u * LANES:(u + 1) * LANES])
            smin = jnp.minimum(smin, lo[:, u * LANES:(u + 1) * LANES])
        return smax, smin

    smax, smin = lax.fori_loop(0, n1, score_chunk, (jnp.full((tq, LANES), -jnp.inf, F32),
                                                    jnp.full((tq, LANES), jnp.inf, F32)))
    nsel = (n_valid_cols + DSA_KC - 1) // DSA_KC
    n1_blocks = n1 * (DSA_KC1 // LANES)

    @pl.when(n1_blocks < nsel * DSA_SUB)
    def _():
        for u in range(DSA_KC1 // LANES):
            key_ref[n1_blocks + u] = jnp.full((tq, LANES), INT_MIN, I32)

    def load_keys(c):
        blk = key_ref[pl.ds(c * DSA_SUB, DSA_SUB)]
        return jnp.concatenate([blk[u] for u in range(DSA_SUB)], axis=1)

    def count_ge(thr, strict):
        cand_ref[...] = thr

        def body(c, acc):
            out = []
            for r in range(DSA_NB):
                rows = slice(r * LANES, (r + 1) * LANES)
                cand = cand_ref[rows, :]
                a = acc[rows]
                for u in range(DSA_SUB):
                    keys = key_ref[c * DSA_SUB + u, rows, :]
                    hit = (keys > cand) if strict else (keys >= cand)
                    a = jnp.where(hit, a + 1.0, a)
                out.append(a)
            return jnp.concatenate(out, axis=0)
        acc = lax.fori_loop(0, nsel, body, jnp.zeros((tq, LANES), F32))
        return jnp.broadcast_to(jnp.sum(acc, axis=1, keepdims=True), (tq, LANES))

    t_ref[...] = jnp.full((tq, LANES), INT_MIN, I32)
    kf = float(k_top)

    def float_key(x):
        bits = lax.bitcast_convert_type(x, I32)
        return jnp.where(x == 0.0, 0, bits ^ ((bits >> 31) & 0x7FFFFFFF))

    def bisect_values():
        lower = jnp.broadcast_to(jnp.min(smin, axis=1, keepdims=True), (tq, LANES))
        upper = jnp.broadcast_to(jnp.max(smax, axis=1, keepdims=True), (tq, LANES))
        n_causal = (row0 + 1 + lax.broadcasted_iota(I32, (tq, LANES), 0)).astype(F32)
        keep_all = n_causal <= kf
        open0 = jnp.where(keep_all, 0.0, 1.0)

        def step(lower, upper, still_open):
            mid = 0.5 * lower + 0.5 * upper
            cnt = count_ge(float_key(mid), False)
            active = still_open > 0.0
            go_up = jnp.logical_and(active, cnt >= kf)
            go_down = jnp.logical_and(active, cnt < kf)
            lower = jnp.where(go_up, mid, lower)
            upper = jnp.where(go_down, mid, upper)
            still_open = jnp.where(jnp.logical_and(go_up, cnt == kf), 0.0, still_open)
            return lower, upper, still_open

        lower, upper, still_open = lax.fori_loop(
            0, DSA_VALUE_BLIND_STEPS, lambda _, c: step(*c), (lower, upper, open0))

        def cond(carry):
            it, _, _, still_open = carry
            return jnp.logical_and(it < DSA_VALUE_STEPS, jnp.max(still_open) > 0.0)

        def body(carry):
            it, lower, upper, still_open = carry
            return (it + 1,) + step(lower, upper, still_open)

        _, lower, _, still_open = lax.while_loop(
            cond, body, (jnp.int32(DSA_VALUE_BLIND_STEPS), lower, upper, still_open))
        return jnp.where(keep_all, INT_MIN, float_key(lower)), jnp.max(still_open) == 0.0

    def exact_select():
        def bit_step(it, carry):
            thr, cnt_thr = carry
            cand = thr + jnp.left_shift(jnp.int32(1), 31 - it)
            cnt = count_ge(cand, False)
            take = cnt >= kf
            return jnp.where(take, cand, thr), jnp.where(take, cnt, cnt_thr)

        thr0 = jnp.full((tq, LANES), INT_MIN, I32)
        cnt0 = jnp.full((tq, LANES), float(2 ** 24), F32)
        thr, cnt_thr = lax.fori_loop(0, 32, bit_step, (thr0, cnt0))
        t_ref[...] = thr

        excess = jnp.where(thr > INT_MIN, cnt_thr - kf, 0.0)

        @pl.when(jnp.max(excess) > 0.0)
        def _():
            need = kf - count_ge(thr, True)

            def tie_chunk(c, seen):
                keys = load_keys(c)
                eq = keys == _lane_tile(thr, DSA_SUB)
                eqf = jnp.where(eq, 1.0, 0.0)
                rank = _dot(eqf.astype(BF16), tri_ref[...]) + _lane_tile(seen, DSA_SUB)
                dropped = jnp.where(rank >= _lane_tile(need, DSA_SUB), INT_MIN, keys)
                keys = jnp.where(eq, dropped, keys)
                for u in range(DSA_SUB):
                    key_ref[c * DSA_SUB + u] = keys[:, u * LANES:(u + 1) * LANES]
                return seen + jnp.broadcast_to(jnp.sum(eqf, axis=1, keepdims=True), (tq, LANES))

            lax.fori_loop(0, nsel, tie_chunk, jnp.zeros((tq, LANES), F32))

    @pl.when(n_valid_cols > k_top)
    def _():
        thr_fast, finished = bisect_values()
        t_ref[...] = thr_fast

        @pl.when(jnp.logical_not(finished))
        def _():
            exact_select()

    thr = jnp.maximum(t_ref[...], INT_MIN + 1)

    m_ref[...] = jnp.full(m_ref.shape, NEG, F32)
    l_ref[...] = jnp.zeros(l_ref.shape, F32)
    acc_ref[...] = jnp.zeros(acc_ref.shape, F32)
    q_all = qm_ref[...].reshape(GROUP_HEADS * tq, LANES)

    def attend(k_blk, v_blk, madd, bias):
        width = k_blk.shape[0]
        s = _nt_dot(q_all, k_blk).reshape(GROUP_HEADS, tq, width) + madd[None]
        if bias is not None:
            s = s + bias
        _softmax_step_stacked(s, v_blk, m_ref, l_ref, acc_ref)

    nwin = DSA_NB + 1
    win_blk = jnp.maximum(i * DSA_NB - 1, 0)
    win_start = pl.multiple_of(win_blk * LANES, LANES)
    near_keys = key_ref[pl.ds(win_blk, nwin)]
    near_keys = jnp.concatenate([near_keys[u] for u in range(nwin)], axis=1)
    near_madd = jnp.where(near_keys >= _lane_tile(thr, nwin), 0.0, NEG)
    attend(k2_ref[pl.ds(win_start, nwin * LANES), :], v2_ref[pl.ds(win_start, nwin * LANES), :],
           near_madd, bias_ref[jnp.minimum(i, 1)])
    for u in range(nwin):
        key_ref[win_blk + u] = jnp.full((tq, LANES), INT_MIN, I32)

    n_far = (win_blk * LANES + DSA_KC - 1) // DSA_KC

    def far_chunk(c, carry):
        start = pl.multiple_of(c * DSA_KC, DSA_KC)
        madd = jnp.where(load_keys(c) >= _lane_tile(thr, DSA_SUB), 0.0, NEG)
        attend(k2_ref[pl.ds(start, DSA_KC), :], v2_ref[pl.ds(start, DSA_KC), :], madd, None)
        return carry

    lax.fori_loop(0, n_far, far_chunk, 0)
    _store_pairs(o_ref, [acc_ref[h] / l_ref[h] for h in range(GROUP_HEADS)])


def _mixer_dsa(p, side, bias, tri, batch, seq):
    t = p.shape[0]
    tq = DSA_TQ
    nq = seq // tq
    k_top = min(TOPK_MAX, seq // 4)
    cw = PT // LANES
    return pl.pallas_call(
        functools.partial(_dsa_kernel, k_top=k_top),
        grid=(batch, nq),
        in_specs=[
            pl.BlockSpec((tq, PT), lambda b, i: (b * nq + i, T_AQ)),
            pl.BlockSpec((tq, 2 * PT), lambda b, i: (b * nq + i, T_IQ // 2)),
            pl.BlockSpec((tq, LANES), lambda b, i: (b * nq + i, 0)),
            pl.BlockSpec((seq, LANES), lambda b, i: (b, T_K2 * cw), pipeline_mode=pl.Buffered(1)),
            pl.BlockSpec((seq, LANES), lambda b, i: (b, T_MISC * cw), pipeline_mode=pl.Buffered(1)),
            pl.BlockSpec((seq, LANES), lambda b, i: (b, T_MISC * cw + 2), pipeline_mode=pl.Buffered(1)),
            pl.BlockSpec((2, GROUP_HEADS, tq, tq + LANES), lambda b, i: (0, 0, 0, 0),
                         pipeline_mode=pl.Buffered(1)),
            pl.BlockSpec((DSA_KC, DSA_KC), lambda b, i: (0, 0), pipeline_mode=pl.Buffered(1)),
        ],
        out_specs=pl.BlockSpec((tq, GROUP_WIDTH), lambda b, i: (b * nq + i, 0)),
        out_shape=jax.ShapeDtypeStruct((t, GROUP_WIDTH), MIX_DTYPE),
        scratch_shapes=[
            pltpu.VMEM((max(seq // LANES, DSA_SUB, DSA_NB + 1), tq, LANES), I32),
            pltpu.VMEM((IDX_HEADS, tq, LANES), F32),
            pltpu.VMEM((IDX_HEADS, tq, LANES), BF16),
            pltpu.VMEM((GROUP_HEADS, tq, LANES), BF16),
            pltpu.VMEM((tq, LANES), I32),
            pltpu.VMEM((tq, LANES), I32),
            pltpu.VMEM((GROUP_HEADS, tq, LANES), F32),
            pltpu.VMEM((GROUP_HEADS, tq, LANES), F32),
            pltpu.VMEM((GROUP_HEADS, tq, LANES), F32),
        ],
        compiler_params=_cparams(("arbitrary", "arbitrary")),
        name="mixer_dsa",
    )(p, p, side, p, p, p, bias, tri)


SWA_TQ = 128


def _swa_kernel(sink_ref, q_ref, kp_ref, kc_ref, vp_ref, vc_ref, bias_ref, o_ref):
    i = pl.program_id(1)
    tq = SWA_TQ
    nh = GROUP_HEADS
    t_l = lax.broadcasted_iota(I32, (tq, 2 * LANES), 0)
    w = lax.broadcasted_iota(I32, (tq, 2 * LANES), 1)
    first = jnp.where(i > 0, t_l + 1, LANES)
    madd = jnp.where(w >= first, jnp.where(w <= t_l + LANES, 0.0, NEG), NEG)
    q_heads = []
    sink_rows = []
    for h in range(nh):
        blk = q_ref[:, (h // 2) * LANES:(h // 2 + 1) * LANES]
        q_heads.append(jnp.where(_own_lane_mask(blk.shape, h), blk, jnp.zeros_like(blk)))
        sink_rows.append(jnp.full((1, tq, LANES), sink_ref[h] * LOG2E, F32))
    sinks = jnp.concatenate(sink_rows, axis=0)
    k_win = jnp.concatenate([kp_ref[...], kc_ref[...]], axis=0)
    v_win = jnp.concatenate([vp_ref[...], vc_ref[...]], axis=0)
    s = _nt_dot(jnp.concatenate(q_heads, axis=0), k_win).reshape(nh, tq, 2 * LANES) + bias_ref[...] + madd[None]
    m = jnp.maximum(jnp.max(s, axis=2, keepdims=True), sinks)
    p = jnp.exp2(s - jnp.concatenate([m, m], axis=2))
    l = jnp.sum(p, axis=2, keepdims=True) + jnp.exp2(sinks - m)
    out = _dot(p.astype(BF16).reshape(nh * tq, 2 * LANES), v_win).reshape(nh, tq, LANES) / l
    _store_pairs(o_ref, [out[h] for h in range(nh)])


def _mixer_swa(p, sinks, bias, batch, seq):
    t = p.shape[0]
    tq = SWA_TQ
    nq = seq // tq
    cw = PT // LANES
    cur = lambda col: (lambda b, i: (b * nq + i, col))
    prev = lambda col: (lambda b, i: (b * nq + jnp.maximum(i - 1, 0), col))
    return pl.pallas_call(
        _swa_kernel,
        grid=(batch, nq),
        in_specs=[
            pl.BlockSpec(memory_space=pltpu.SMEM),
            pl.BlockSpec((tq, PT), lambda b, i: (b * nq + i, T_BQ)),
            pl.BlockSpec((tq, LANES), prev(T_K2 * cw + 1)),
            pl.BlockSpec((tq, LANES), cur(T_K2 * cw + 1)),
            pl.BlockSpec((tq, LANES), prev(T_MISC * cw + 1)),
            pl.BlockSpec((tq, LANES), cur(T_MISC * cw + 1)),
            pl.BlockSpec((GROUP_HEADS, tq, 2 * LANES), lambda b, i: (0, 0, 0)),
        ],
        out_specs=pl.BlockSpec((tq, GROUP_WIDTH), lambda b, i: (b * nq + i, 0)),
        out_shape=jax.ShapeDtypeStruct((t, GROUP_WIDTH), MIX_DTYPE),
        compiler_params=_cparams(("arbitrary", "arbitrary")),
        name="mixer_swa",
    )(sinks, p, p, p, p, p, bias)


def _logsig(x):
    return jnp.minimum(x, 0.0) - jnp.log(1.0 + jnp.exp(-jnp.abs(x)))


CUM_GROUP = 4


def _foxcum_kernel(fb_ref, f_ref, tri_ref, o_ref, *, nchunk):
    tri = tri_ref[...]

    def local(g, carry):
        for u in range(CUM_GROUP):
            c = g * CUM_GROUP + u
            start = pl.multiple_of(c * LANES, LANES)
            lf = _logsig(f_ref[pl.ds(start, LANES), :] + fb_ref[...]).T
            p1 = lf.astype(BF16)
            r1 = lf - p1.astype(F32)
            p2 = r1.astype(BF16)
            p3 = (r1 - p2.astype(F32)).astype(BF16)
            cum = (_dot(p1, tri) + _dot(p2, tri) + _dot(p3, tri))[:GROUP_HEADS]
            o_ref[0, c] = cum
        return carry

    lax.fori_loop(0, nchunk // CUM_GROUP, local, 0)

    def offset(c, run):
        cum = o_ref[0, c] + run
        o_ref[0, c] = cum
        return jnp.broadcast_to(cum[:, LANES - 1:LANES], cum.shape)

    lax.fori_loop(0, nchunk, offset, jnp.zeros((GROUP_HEADS, LANES), F32))


def _fox_cum(side, fb_row, tri_incl, batch, seq):
    nchunk = seq // LANES
    return pl.pallas_call(
        functools.partial(_foxcum_kernel, nchunk=nchunk),
        grid=(batch,),
        in_specs=[
            pl.BlockSpec((1, LANES), lambda b: (0, 0)),
            pl.BlockSpec((seq, LANES), lambda b: (b, 0)),
            pl.BlockSpec((LANES, LANES), lambda b: (0, 0)),
        ],
        out_specs=pl.BlockSpec((1, nchunk, GROUP_HEADS, LANES), lambda b: (b, 0, 0, 0)),
        out_shape=jax.ShapeDtypeStruct((batch, nchunk, GROUP_HEADS, LANES), F32),
        compiler_params=_cparams(("arbitrary",)),
        name="fox_cumsum",
    )(fb_row, side, tri_incl)


FOX_TQ = 256
FOX_KC = 512
FOX_SUB = FOX_KC // LANES


def _fox_kernel(q_ref, g_ref, k_ref, v_ref, cum_ref, o_ref, qm_ref, m_ref, l_ref, acc_ref):
    i = pl.program_id(1)
    tq = FOX_TQ
    for h in range(GROUP_HEADS):
        blk = q_ref[:, (h // 2) * LANES:(h // 2 + 1) * LANES]
        qm_ref[h] = jnp.where(_own_lane_mask(blk.shape, h), blk, jnp.zeros_like(blk))
    m_ref[...] = jnp.full(m_ref.shape, NEG, F32)
    l_ref[...] = jnp.zeros(l_ref.shape, F32)
    acc_ref[...] = jnp.zeros(acc_ref.shape, F32)
    f_ref0 = cum_ref[0, i * (tq // LANES)][:, 0:1]

    def chunk(c, masked):
        start = pl.multiple_of(c * FOX_KC, FOX_KC)
        cum = cum_ref[0, pl.ds(c * FOX_SUB, FOX_SUB)]
        cum = jnp.concatenate([cum[u] for u in range(FOX_SUB)], axis=1)
        fbias = (f_ref0 - cum) * LOG2E
        per_head = []
        for pr in range(PAIRS):
            pair = slice(pr * LANES, (pr + 1) * LANES)
            q_pair = qm_ref[2 * pr:2 * pr + 2].reshape(2 * tq, LANES)
            s_pair = _nt_dot(q_pair, k_ref[pl.ds(start, FOX_KC), pair])
            for e in range(2):
                h = 2 * pr + e
                per_head.append(s_pair[e * tq:(e + 1) * tq] + fbias[h:h + 1, :])
        s = jnp.stack(per_head)
        if masked:
            shape = (GROUP_HEADS, tq, FOX_KC)
            ok = start + lax.broadcasted_iota(I32, shape, 2) <= i * tq + lax.broadcasted_iota(I32, shape, 1)
            s = jnp.where(ok, s, NEG)
        reps = FOX_KC // LANES
        m_prev = m_ref[...]
        m_new = jnp.maximum(m_prev, jnp.max(s, axis=2, keepdims=True))
        p = jnp.exp2(s - jnp.concatenate([m_new] * reps, axis=2))
        alpha = jnp.exp2(m_prev - m_new)
        l_ref[...] = alpha * l_ref[...] + jnp.sum(p, axis=2, keepdims=True)
        m_ref[...] = m_new
        p = p.astype(BF16)
        pvs = []
        for pr in range(PAIRS):
            p_pair = p[2 * pr:2 * pr + 2].reshape(2 * tq, FOX_KC)
            pvs.append(_dot(p_pair, v_ref[pl.ds(start, FOX_KC), pr * LANES:(pr + 1) * LANES]))
        acc_ref[...] = alpha * acc_ref[...] + jnp.concatenate(pvs, axis=0).reshape(GROUP_HEADS, tq, LANES)

    def body(c, carry):
        chunk(c, False)
        return carry

    n_full = (i * tq) // FOX_KC
    lax.fori_loop(0, n_full, body, 0)
    chunk(n_full, True)
    outs = []
    for h in range(GROUP_HEADS):
        gate = _sigmoid(g_ref[:, (h // 2) * LANES:(h // 2 + 1) * LANES].astype(F32))
        outs.append(acc_ref[h] / l_ref[h] * gate)
    _store_pairs(o_ref, outs)


def _mixer_fox(p, cum, batch, seq):
    t = p.shape[0]
    tq = FOX_TQ
    nq = seq // tq
    return pl.pallas_call(
        _fox_kernel,
        grid=(batch, nq),
        in_specs=[
            pl.BlockSpec((tq, PT), lambda b, i: (b * nq + i, T_CQ)),
            pl.BlockSpec((tq, PT), lambda b, i: (b * nq + i, T_CG)),
            pl.BlockSpec((seq, PT), lambda b, i: (b, T_CK)),
            pl.BlockSpec((seq, PT), lambda b, i: (b, T_CV)),
            pl.BlockSpec((1, seq // LANES, GROUP_HEADS, LANES), lambda b, i: (b, 0, 0, 0)),
        ],
        out_specs=pl.BlockSpec((tq, GROUP_WIDTH), lambda b, i: (b * nq + i, 0)),
        out_shape=jax.ShapeDtypeStruct((t, GROUP_WIDTH), MIX_DTYPE),
        scratch_shapes=[
            pltpu.VMEM((GROUP_HEADS, tq, LANES), BF16),
            pltpu.VMEM((GROUP_HEADS, tq, LANES), F32),
            pltpu.VMEM((GROUP_HEADS, tq, LANES), F32),
            pltpu.VMEM((GROUP_HEADS, tq, LANES), F32),
        ],
        compiler_params=_cparams(("arbitrary", "arbitrary")),
        name="mixer_fox",
    )(p, p, p, p, cum)


STK_TQ = 128


def _stick_kernel(q_ref, k_ref, v_ref, tri_ref, o_ref, qm_ref, acc_ref, r_ref):
    i = pl.program_id(1)
    tq = STK_TQ
    nh = GROUP_HEADS
    for h in range(nh):
        blk = q_ref[:, (h // 2) * LANES:(h // 2 + 1) * LANES]
        qm_ref[h] = jnp.where(_own_lane_mask(blk.shape, h), blk, jnp.zeros_like(blk))
    acc_ref[...] = jnp.zeros(acc_ref.shape, F32)
    r_ref[...] = jnp.zeros(r_ref.shape, F32)
    before = (lax.broadcasted_iota(I32, (nh, tq, LANES), 2) < lax.broadcasted_iota(I32, (nh, tq, LANES), 1))
    tri = tri_ref[...]

    def block(j, masked):
        start = pl.multiple_of(j * LANES, LANES)
        zs = []
        for pr in range(PAIRS):
            q_pair = qm_ref[2 * pr:2 * pr + 2].reshape(2 * tq, LANES)
            zs.append(_nt_dot(q_pair, k_ref[pl.ds(start, LANES), pr * LANES:(pr + 1) * LANES]))
        z = jnp.concatenate(zs, axis=0).reshape(nh, tq, LANES)
        lsz = _logsig(z)
        u = lsz - z
        if masked:
            u = jnp.where(before, u, 0.0)
        u2 = u.reshape(nh * tq, LANES)
        u_hi = u2.astype(BF16)
        u_lo = (u2 - u_hi.astype(F32)).astype(BF16)
        nearer = (_dot(u_hi, tri) + _dot(u_lo, tri)).reshape(nh, tq, LANES)
        run = r_ref[...]
        w = jnp.exp(lsz + nearer + run)
        if masked:
            w = jnp.where(before, w, 0.0)
        wb = w.astype(BF16)
        pvs = []
        for pr in range(PAIRS):
            w_pair = wb[2 * pr:2 * pr + 2].reshape(2 * tq, LANES)
            pvs.append(_dot(w_pair, v_ref[pl.ds(start, LANES), pr * LANES:(pr + 1) * LANES]))
        acc_ref[...] += jnp.concatenate(pvs, axis=0).reshape(nh, tq, LANES)
        run = run + jnp.sum(u, axis=2, keepdims=True)
        r_ref[...] = run
        return jnp.max(run)

    rmax = block(i, True)

    def cond(carry):
        j, rmax = carry
        return jnp.logical_and(j >= 0, rmax >= STICK_EXIT)

    def body(carry):
        j, _ = carry
        return j - 1, block(j, False)

    lax.while_loop(cond, body, (i - 1, rmax))
    _store_pairs(o_ref, [acc_ref[h] for h in range(nh)])


def _mixer_stick(p, tri_excl, batch, seq):
    t = p.shape[0]
    tq = STK_TQ
    nq = seq // tq
    return pl.pallas_call(
        _stick_kernel,
        grid=(batch, nq),
        in_specs=[
            pl.BlockSpec((tq, PT), lambda b, i: (b * nq + i, T_DQ)),
            pl.BlockSpec((seq, PT), lambda b, i: (b, T_DK)),
            pl.BlockSpec((seq, PT), lambda b, i: (b, T_DV)),
            pl.BlockSpec((LANES, LANES), lambda b, i: (0, 0)),
        ],
        out_specs=pl.BlockSpec((tq, GROUP_WIDTH), lambda b, i: (b * nq + i, 0)),
        out_shape=jax.ShapeDtypeStruct((t, GROUP_WIDTH), MIX_DTYPE),
        scratch_shapes=[pltpu.VMEM((GROUP_HEADS, tq, LANES), BF16),
                        pltpu.VMEM((GROUP_HEADS, tq, LANES), F32),
                        pltpu.VMEM((GROUP_HEADS, tq, LANES), F32)],
        compiler_params=_cparams(("arbitrary", "arbitrary")),
        name="mixer_stick",
    )(p, p, p, tri_excl)


def _outproj_kernel(x_ref, gt_ref, oa_ref, ob_ref, oc_ref, od_ref, gg_ref, w_ref, o_ref):
    acc = None
    for m, ref in enumerate((oa_ref, ob_ref, oc_ref, od_ref)):
        o = ref[...].astype(F32)
        y = o * lax.rsqrt(jnp.mean(o * o, axis=-1, keepdims=True) + EPS) * gg_ref[m:m + 1, :]
        part = _dot(y.astype(BF16), w_ref[m * GROUP_WIDTH:(m + 1) * GROUP_WIDTH, :])
        acc = part if acc is None else acc + part
    o_ref[...] = x_ref[...] + gt_ref[0] * acc


def _outproj(x, gt, outs, gg, w, l, seq):
    t, d = x.shape
    tm = 512
    per_b = seq // tm
    mix = pl.BlockSpec((tm, GROUP_WIDTH), lambda i: (i, 0))
    return pl.pallas_call(
        _outproj_kernel,
        grid=(t // tm,),
        in_specs=[
            pl.BlockSpec((tm, d), lambda i: (i, 0)),
            pl.BlockSpec((1, 1, d), lambda i: (i // per_b, 0, 0)),
            mix, mix, mix, mix,
            pl.BlockSpec((None, N_MIXERS, GROUP_WIDTH), lambda i: (l, 0, 0)),
            pl.BlockSpec((None, N_MIXERS * GROUP_WIDTH, d), lambda i: (l, 0, 0)),
        ],
        out_specs=pl.BlockSpec((tm, d), lambda i: (i, 0)),
        out_shape=jax.ShapeDtypeStruct((t, d), F32),
        compiler_params=_cparams(("arbitrary",)),
        name="outproj",
    )(x, gt, *outs, gg, w)


def _prep_in_weights(w_in, qk_g):
    depth, d, _ = w_in.shape
    sizes = (GROUP_WIDTH, HEAD_DIM, HEAD_DIM, IDX_HEADS * IDX_DIM, IDX_DIM, IDX_HEADS,
             GROUP_WIDTH, HEAD_DIM, HEAD_DIM,
             GROUP_WIDTH, GROUP_WIDTH, GROUP_WIDTH, GROUP_HEADS, GROUP_WIDTH,
             GROUP_WIDTH, GROUP_WIDTH, GROUP_WIDTH)
    pts = np.cumsum(sizes)[:-1].tolist()
    (a_q, a_k, a_v, a_iq, a_ik, a_iw, b_q, b_k, b_v,
     c_q, c_k, c_v, c_f, c_g, d_q, d_k, d_v) = jnp.split(w_in, pts, axis=-1)
    z = lambda n: jnp.zeros((depth, d, n), w_in.dtype)
    qscale = HEAD_DIM ** -0.5
    main = jnp.concatenate([
        a_q, b_q, c_q, c_k,
        a_k, a_k, b_k, b_k, z(2 * LANES),
        c_v, c_g, d_q * qscale, d_k, d_v, a_iq * (IDX_DIM ** -0.5),
        a_v, a_v, b_v, b_v, a_ik, a_ik, z(LANES),
    ], axis=-1).astype(BF16)
    side = jnp.concatenate([c_f, a_iw * (IDX_HEADS ** -0.5), z(LANES - GROUP_HEADS - IDX_HEADS)],
                           axis=-1).astype(BF16)
    rep = lambda g, n: jnp.tile(g, (1, n))
    sscale = qscale * LOG2E
    gain = jnp.concatenate([
        rep(qk_g[:, 0], 8) * sscale, rep(qk_g[:, 2], 8) * sscale, rep(qk_g[:, 4], 8) * sscale,
        rep(qk_g[:, 5], 8), rep(qk_g[:, 1], 2), rep(qk_g[:, 3], 2),
        jnp.zeros((depth, P_COLS - 4 * PT - 2 * LANES), F32),
    ], axis=-1).reshape(depth, 1, P_COLS)
    return main, side, gain


def kernel(x, c, w_ada, b_ada, norm_g, w_in, qk_g, forget_b, sinks, rel_table, group_g, w_out,
           w_ffn_gate, w_ffn_up, w_ffn_down):
    batch, seq, d = x.shape
    depth = w_ada.shape[0]
    t = batch * seq

    mod = _modulation(c, w_ada, b_ada)
    wg = w_ffn_gate.astype(BF16)
    wu = w_ffn_up.astype(BF16)
    wd = w_ffn_down.astype(BF16)
    wo = w_out.astype(BF16)
    w_main, w_side, gain = _prep_in_weights(w_in.astype(BF16), qk_g)

    r = np.arange(PT)
    bd = jnp.asarray((r[:, None] // HEAD_DIM == r[None, :] // HEAD_DIM).astype(np.float32) / HEAD_DIM, BF16)
    r = np.arange(LANES)
    tri_incl = jnp.asarray(r[:, None] <= r[None, :], BF16)
    tri_after = jnp.asarray(r[:, None] > r[None, :], BF16)
    r = np.arange(DSA_KC)
    tri_before = jnp.asarray(r[:, None] < r[None, :], BF16)

    bias_dsa = jnp.swapaxes(_bias_windows(rel_table, DSA_TQ)[:GROUP_HEADS], 0, 1)
    bias_swa = _bias_windows(rel_table, SWA_TQ)[GROUP_HEADS:, 1]
    fb_rows = jnp.zeros((depth, 1, LANES), F32).at[:, 0, :GROUP_HEADS].set(forget_b)
    gg = group_g.reshape(depth, N_MIXERS, GROUP_WIDTH)

    xt = x.reshape(t, d)
    for l in range(depth):
        parts = [m.reshape(batch, 1, d) for m in jnp.split(mod[l], 9, axis=-1)]
        sh1, sc1, g1, sh2, sc2, g2, sh3, sc3, g3 = parts
        xt = _ffn(xt, norm_g[l, 0:1], sh1, sc1, g1, wg, wu, wd, l, 0, seq)
        ng2 = norm_g[l, 1:2]
        p, side = _inproj(xt, ng2, sh2, sc2, w_main, gain, bd, w_side, l, seq, N_NORM_TILES)
        o_a = _mixer_dsa(p, side, bias_dsa, tri_before, batch, seq)
        o_b = _mixer_swa(p, sinks[l], bias_swa, batch, seq)
        cum = _fox_cum(side, fb_rows[l], tri_incl, batch, seq)
        o_c = _mixer_fox(p, cum, batch, seq)
        o_d = _mixer_stick(p, tri_after, batch, seq)
        xt = _outproj(xt, g2, (o_a, o_b, o_c, o_d), gg, wo, l, seq)
        xt = _ffn(xt, norm_g[l, 2:3], sh3, sc3, g3, wg, wu, wd, l, 1, seq)
    return xt.reshape(batch, seq, d)
```
